```python
import math
import jax, jax.numpy as jnp
from jax import lax
import numpy as np

D_MODEL = 2048
BATCH = 8
SEQ = 8192
DEPTH = 2

GDN_HEADS = 8
GDN_DK = 128
GDN_DV = 128
RET_HEADS = 8
RET_DK = 128
RET_DV = 128
CONV_WIDTH = 4
LA_CHUNK = 64
ROPE_BASE = 10000.0
GDN_QK = GDN_HEADS * GDN_DK
GDN_V = GDN_HEADS * GDN_DV
RET_QK = RET_HEADS * RET_DK
RET_V = RET_HEADS * RET_DV
MIX_IN = 2 * GDN_QK + 2 * GDN_V + 2 * GDN_HEADS + 2 * RET_QK + 2 * RET_V
MIX_OUT = GDN_V + RET_V
SG_CHUNK = 128
SG_GROUPS = 8
SG_WIDTH = 2 * D_MODEL
SG_GROUP_DIM = SG_WIDTH // SG_GROUPS
FFN_HIDDEN = 4 * D_MODEL
EPS = 1e-6
N_EVEN = (DEPTH + 1) // 2
N_ODD = DEPTH // 2

kernel_name = "hybrid_gdn_retention_gmlp_block"


def rmsnorm(x, w):
    xf = x.astype(jnp.float32)
    y = xf * lax.rsqrt(jnp.mean(jnp.square(xf), axis=-1, keepdims=True) + EPS)
    return (y * w.astype(jnp.float32)).astype(x.dtype)


def head_rmsnorm(x):
    return x * lax.rsqrt(jnp.mean(jnp.square(x), axis=-1, keepdims=True) + EPS)


def layernorm(x, w, b):
    xf = x.astype(jnp.float32)
    mu = jnp.mean(xf, axis=-1, keepdims=True)
    xc = xf - mu
    var = jnp.mean(jnp.square(xc), axis=-1, keepdims=True)
    y = xc * lax.rsqrt(var + EPS) * w.astype(jnp.float32) + b.astype(jnp.float32)
    return y.astype(x.dtype)


def l2norm(x):
    return x * lax.rsqrt(jnp.sum(jnp.square(x), axis=-1, keepdims=True) + 1e-6)


def causal_conv(x, w):
    k_w = w.shape[-1]
    kern = jnp.transpose(w)[:, None, :].astype(x.dtype)
    return lax.conv_general_dilated(x, kern, window_strides=(1,), padding=[(k_w - 1, 0)],
                                    dimension_numbers=('NWC', 'WIO', 'NWC'),
                                    feature_group_count=x.shape[-1])


def rotary(x, pos):
    half = x.shape[-1] // 2
    inv_freq = 1.0 / (ROPE_BASE ** jnp.linspace(0.0, 1.0, half, dtype=jnp.float32))
    ang = pos[:, None] * inv_freq[None, :]
    cos = jnp.cos(ang)[None, :, None, :]
    sin = jnp.sin(ang)[None, :, None, :]
    x1, x2 = x[..., :half], x[..., half:]
    return jnp.concatenate([x1 * cos - x2 * sin, x2 * cos + x1 * sin], axis=-1)


def to_chunks(t, c):
    b_, l_ = t.shape[:2]
    t = t.reshape(b_, l_ // c, c, *t.shape[2:])
    return jnp.swapaxes(t, 2, 3)


def from_chunks(t):
    t = jnp.swapaxes(t, 2, 3)
    return t.reshape(t.shape[0], -1, *t.shape[3:])


def gated_delta_rule(q, k, v, beta, g):
    c = LA_CHUNK
    dk = q.shape[-1]
    dv = v.shape[-1]
    q, k, v, beta, g = (to_chunks(t, c) for t in (q * dk ** -0.5, k, v, beta, g))
    g = jnp.cumsum(g, axis=-1)
    causal = jnp.tril(jnp.ones((c, c), dtype=bool))
    strict = jnp.tril(jnp.ones((c, c), dtype=bool), k=-1)
    decay = jnp.exp(jnp.where(causal, g[..., :, None] - g[..., None, :], -jnp.inf))
    kb = k * beta[..., None]
    a = jnp.where(strict, jnp.einsum('bnhtk,bnhsk->bnhts', kb, k) * decay, 0.0)
    eye = jnp.eye(c, dtype=q.dtype)
    t_inv = lax.linalg.triangular_solve(a + eye, jnp.broadcast_to(eye, a.shape), left_side=True,
                                        lower=True, unit_diagonal=True)
    u = jnp.einsum('bnhts,bnhsv->bnhtv', t_inv, v * beta[..., None])
    w = jnp.einsum('bnhts,bnhsk->bnhtk', t_inv, kb * jnp.exp(g)[..., None])
    qk = jnp.where(causal, jnp.einsum('bnhtk,bnhsk->bnhts', q, k) * decay, 0.0)
    q_dec = q * jnp.exp(g)[..., None]
    k_tail = k * jnp.exp(g[..., -1:] - g)[..., None]
    chunk_decay = jnp.exp(g[..., -1])

    def step(state, xs):
        q_c, k_c, u_c, w_c, qk_c, d_c = xs
        v_new = u_c - jnp.einsum('bhtk,bhkv->bhtv', w_c, state)
        o = jnp.einsum('bhtk,bhkv->bhtv', q_c, state) + jnp.einsum('bhts,bhsv->bhtv', qk_c, v_new)
        state = state * d_c[..., None, None] + jnp.einsum('bhtk,bhtv->bhkv', k_c, v_new)
        return state, o

    b_, _, h_ = g.shape[:3]
    state0 = jnp.zeros((b_, h_, dk, dv), q.dtype)
    xs = tuple(jnp.moveaxis(t, 1, 0) for t in (q_dec, k_tail, u, w, qk, chunk_decay))
    _, o = lax.scan(step, state0, xs)
    return from_chunks(jnp.moveaxis(o, 0, 1))


def retention(q, k, v, log_gamma):
    c = LA_CHUNK
    dk = q.shape[-1]
    dv = v.shape[-1]
    q, k, v = (to_chunks(t, c) for t in (q, k, v))
    pos = jnp.arange(c, dtype=jnp.float32)
    causal = jnp.tril(jnp.ones((c, c), dtype=bool))
    lg = log_gamma[:, None]
    d_mat = jnp.exp(jnp.where(causal, (pos[:, None] - pos[None, :]) * log_gamma[:, None, None], -jnp.inf))
    inner = jnp.einsum('bnhts,bnhsv->bnhtv', jnp.einsum('bnhtk,bnhsk->bnhts', q, k) * d_mat, v)
    q_dec = q * jnp.exp((pos + 1.0) * lg)[..., None]
    k_dec = k * jnp.exp((c - 1.0 - pos) * lg)[..., None]
    chunk_decay = jnp.exp(c * log_gamma)[:, None, None]

    def step(state, xs):
        q_c, k_c, v_c = xs
        o = jnp.einsum('bhtk,bhkv->bhtv', q_c, state)
        state = state * chunk_decay + jnp.einsum('bhtk,bhtv->bhkv', k_c, v_c)
        return state, o

    state0 = jnp.zeros((q.shape[0], q.shape[2], dk, dv), q.dtype)
    xs = tuple(jnp.moveaxis(t, 1, 0) for t in (q_dec, k_dec, v))
    _, cross = lax.scan(step, state0, xs)
    return from_chunks(inner + jnp.moveaxis(cross, 0, 1))


def linear_attention_mixer(h, w_in, conv_w, a_log, dt_bias, out_norm_w, w_out):
    f32 = jnp.float32
    b_, l_, _ = h.shape
    proj = h @ w_in
    sizes = [GDN_QK, GDN_QK, GDN_V, GDN_V, GDN_HEADS, GDN_HEADS, RET_QK, RET_QK, RET_V, RET_V]
    cuts = [int(s) for s in np.cumsum(sizes)[:-1]]
    gq, gk, gv, gz, gb, ga, rq, rk, rv, rg = jnp.split(proj, cuts, axis=-1)

    qkv = jax.nn.silu(causal_conv(jnp.concatenate([gq, gk, gv], axis=-1), conv_w))
    gq, gk, gv = jnp.split(qkv, [GDN_QK, 2 * GDN_QK], axis=-1)
    q = l2norm(gq.astype(f32).reshape(b_, l_, GDN_HEADS, GDN_DK))
    k = l2norm(gk.astype(f32).reshape(b_, l_, GDN_HEADS, GDN_DK))
    v = gv.astype(f32).reshape(b_, l_, GDN_HEADS, GDN_DV)
    beta = jax.nn.sigmoid(gb.astype(f32))
    g = -jnp.exp(a_log.astype(f32)) * jax.nn.softplus(ga.astype(f32) + dt_bias.astype(f32))
    o_a = gated_delta_rule(q, k, v, beta, g)
    z = gz.astype(f32).reshape(b_, l_, GDN_HEADS, GDN_DV)
    o_a = head_rmsnorm(o_a) * out_norm_w.astype(f32) * jax.nn.silu(z)
    o_a = o_a.reshape(b_, l_, GDN_V)

    pos = jnp.arange(l_, dtype=f32)
    log_gamma = jnp.log1p(-jnp.power(2.0, -5.0 - jnp.arange(RET_HEADS, dtype=f32)))
    q = rotary(rq.astype(f32).reshape(b_, l_, RET_HEADS, RET_DK), pos)
    k = rotary(rk.astype(f32).reshape(b_, l_, RET_HEADS, RET_DK), pos) * RET_DK ** -0.5
    v = rv.astype(f32).reshape(b_, l_, RET_HEADS, RET_DV)
    o_b = head_rmsnorm(retention(q, k, v, log_gamma)).reshape(b_, l_, RET_V)
    o_b = jax.nn.silu(rg.astype(f32)) * o_b

    o = jnp.concatenate([o_a, o_b], axis=-1).astype(h.dtype)
    return o @ w_out


def spatial_gating_mixer(h, w_in, ln_w, ln_b, w_s, b_s, w_out):
    b_, l_, _ = h.shape
    proj = jax.nn.gelu(h @ w_in)
    u, v = jnp.split(proj, 2, axis=-1)
    v = layernorm(v, ln_w, ln_b)
    vc = v.reshape(b_, l_ // SG_CHUNK, SG_CHUNK, SG_GROUPS, SG_GROUP_DIM)
    causal = jnp.tril(jnp.ones((SG_CHUNK, SG_CHUNK), dtype=bool))
    ws = jnp.where(causal, w_s, 0.0)
    s = jnp.einsum('gts,bnsgd->bntgd', ws, vc) + jnp.swapaxes(b_s, 0, 1)[:, :, None]
    s = s.reshape(b_, l_, SG_WIDTH)
    return (u * s) @ w_out


def squared_relu_mlp(h, w_up, w_down):
    return jnp.square(jax.nn.relu(h @ w_up)) @ w_down


def _fwd_setup_inputs(seed: int = 0) -> dict:
    key = jax.random.key(seed)
    ks = jax.random.split(key, 16)
    f32 = jnp.float32

    def dense(k, shape, fan_in):
        return jax.random.normal(k, shape, f32) * fan_in ** -0.5

    x = jax.random.normal(ks[0], (BATCH, SEQ, D_MODEL), f32)
    norm_w = 1.0 + 0.1 * jax.random.normal(ks[1], (DEPTH, 4, D_MODEL), f32)
    la_w_in = dense(ks[2], (N_EVEN, D_MODEL, MIX_IN), D_MODEL)
    la_conv_w = dense(ks[3], (N_EVEN, 2 * GDN_QK + GDN_V, CONV_WIDTH), CONV_WIDTH)
    la_a_log = jnp.log(jax.random.uniform(ks[4], (N_EVEN, GDN_HEADS), f32, 1.0, 16.0))
    dt = jnp.exp(jax.random.uniform(ks[5], (N_EVEN, GDN_HEADS), f32) * (math.log(0.1) - math.log(0.001))
                 + math.log(0.001))
    la_dt_bias = dt + jnp.log(-jnp.expm1(-dt))
    la_out_norm_w = 1.0 + 0.1 * jax.random.normal(ks[6], (N_EVEN, GDN_DV), f32)
    la_w_out = dense(ks[7], (N_EVEN, MIX_OUT, D_MODEL), MIX_OUT)
    sg_w_in = dense(ks[8], (N_ODD, D_MODEL, 2 * SG_WIDTH), D_MODEL)
    sg_ln_w = 1.0 + 0.1 * jax.random.normal(ks[9], (N_ODD, SG_WIDTH), f32)
    sg_ln_b = 0.02 * jax.random.normal(ks[10], (N_ODD, SG_WIDTH), f32)
    sg_w_s = dense(ks[11], (N_ODD, SG_GROUPS, SG_CHUNK, SG_CHUNK), SG_CHUNK)
    sg_b_s = 1.0 + 0.1 * jax.random.normal(ks[12], (N_ODD, SG_GROUPS, SG_CHUNK), f32)
    sg_w_out = dense(ks[13], (N_ODD, SG_WIDTH, D_MODEL), SG_WIDTH)
    ffn_w_up = dense(ks[14], (DEPTH, D_MODEL, FFN_HIDDEN), D_MODEL)
    ffn_w_down = dense(ks[15], (DEPTH, FFN_HIDDEN, D_MODEL), FFN_HIDDEN)
    return {"x": x, "norm_w": norm_w, "la_w_in": la_w_in, "la_conv_w": la_conv_w,
            "la_a_log": la_a_log, "la_dt_bias": la_dt_bias, "la_out_norm_w": la_out_norm_w,
            "la_w_out": la_w_out, "sg_w_in": sg_w_in, "sg_ln_w": sg_ln_w, "sg_ln_b": sg_ln_b,
            "sg_w_s": sg_w_s, "sg_b_s": sg_b_s, "sg_w_out": sg_w_out,
            "ffn_w_up": ffn_w_up, "ffn_w_down": ffn_w_down}


def _fwd_reference(x, norm_w, la_w_in, la_conv_w, la_a_log, la_dt_bias, la_out_norm_w, la_w_out,
              sg_w_in, sg_ln_w, sg_ln_b, sg_w_s, sg_b_s, sg_w_out, ffn_w_up, ffn_w_down):
    h = x
    for layer in range(DEPTH):
        i = layer // 2
        y = rmsnorm(h, norm_w[layer, 0])
        if layer % 2 == 0:
            y = linear_attention_mixer(y, la_w_in[i], la_conv_w[i], la_a_log[i], la_dt_bias[i],
                                       la_out_norm_w[i], la_w_out[i])
        else:
            y = spatial_gating_mixer(y, sg_w_in[i], sg_ln_w[i], sg_ln_b[i], sg_w_s[i], sg_b_s[i],
                                     sg_w_out[i])
        h = h + rmsnorm(y, norm_w[layer, 1])
        y = squared_relu_mlp(rmsnorm(h, norm_w[layer, 2]), ffn_w_up[layer], ffn_w_down[layer])
        h = h + rmsnorm(y, norm_w[layer, 3])
    return h


import jax as _jax
import jax.numpy as _jnp

TWIN_FORMAT = 'train_step'
FWD_PARAMS = ['x', 'norm_w', 'la_w_in', 'la_conv_w', 'la_a_log', 'la_dt_bias', 'la_out_norm_w', 'la_w_out', 'sg_w_in', 'sg_ln_w', 'sg_ln_b', 'sg_w_s', 'sg_b_s', 'sg_w_out', 'ffn_w_up', 'ffn_w_down']
TWIN_WEIGHTS = ['norm_w', 'la_w_in', 'la_conv_w', 'la_a_log', 'la_dt_bias', 'la_out_norm_w', 'la_w_out', 'sg_w_in', 'sg_ln_w', 'sg_ln_b', 'sg_w_s', 'sg_b_s', 'sg_w_out', 'ffn_w_up', 'ffn_w_down']
TWIN_DIFF_INPUT = 'x'
TWIN_INPUTS = ['x', 'norm_w', 'la_w_in', 'la_conv_w', 'la_a_log', 'la_dt_bias', 'la_out_norm_w', 'la_w_out', 'sg_w_in', 'sg_ln_w', 'sg_ln_b', 'sg_w_s', 'sg_b_s', 'sg_w_out', 'ffn_w_up', 'ffn_w_down', 'loss_target', 'm_norm_w', 'm_la_w_in', 'm_la_conv_w', 'm_la_a_log', 'm_la_dt_bias', 'm_la_out_norm_w', 'm_la_w_out', 'm_sg_w_in', 'm_sg_ln_w', 'm_sg_ln_b', 'm_sg_w_s', 'm_sg_b_s', 'm_sg_w_out', 'm_ffn_w_up', 'm_ffn_w_down', 'v_norm_w', 'v_la_w_in', 'v_la_conv_w', 'v_la_a_log', 'v_la_dt_bias', 'v_la_out_norm_w', 'v_la_w_out', 'v_sg_w_in', 'v_sg_ln_w', 'v_sg_ln_b', 'v_sg_w_s', 'v_sg_b_s', 'v_sg_w_out', 'v_ffn_w_up', 'v_ffn_w_down']
TWIN_OUTPUTS = ['loss', 'grad_x', 'grad_norm_w', 'grad_la_w_in', 'grad_la_conv_w', 'grad_la_a_log', 'grad_la_dt_bias', 'grad_la_out_norm_w', 'grad_la_w_out', 'grad_sg_w_in', 'grad_sg_ln_w', 'grad_sg_ln_b', 'grad_sg_w_s', 'grad_sg_b_s', 'grad_sg_w_out', 'grad_ffn_w_up', 'grad_ffn_w_down', 'delta_norm_w', 'delta_la_w_in', 'delta_la_conv_w', 'delta_la_a_log', 'delta_la_dt_bias', 'delta_la_out_norm_w', 'delta_la_w_out', 'delta_sg_w_in', 'delta_sg_ln_w', 'delta_sg_ln_b', 'delta_sg_w_s', 'delta_sg_b_s', 'delta_sg_w_out', 'delta_ffn_w_up', 'delta_ffn_w_down', 'new_m_norm_w', 'new_m_la_w_in', 'new_m_la_conv_w', 'new_m_la_a_log', 'new_m_la_dt_bias', 'new_m_la_out_norm_w', 'new_m_la_w_out', 'new_m_sg_w_in', 'new_m_sg_ln_w', 'new_m_sg_ln_b', 'new_m_sg_w_s', 'new_m_sg_b_s', 'new_m_sg_w_out', 'new_m_ffn_w_up', 'new_m_ffn_w_down', 'new_v_norm_w', 'new_v_la_w_in', 'new_v_la_conv_w', 'new_v_la_a_log', 'new_v_la_dt_bias', 'new_v_la_out_norm_w', 'new_v_la_w_out', 'new_v_sg_w_in', 'new_v_sg_ln_w', 'new_v_sg_ln_b', 'new_v_sg_w_s', 'new_v_sg_b_s', 'new_v_sg_w_out', 'new_v_ffn_w_up', 'new_v_ffn_w_down']
TWIN_LEAF_KINDS = {'loss': 'loss', 'grad_x': 'grad_x', 'grad_norm_w': 'grad_w', 'grad_la_w_in': 'grad_w', 'grad_la_conv_w': 'grad_w', 'grad_la_a_log': 'grad_w', 'grad_la_dt_bias': 'grad_w', 'grad_la_out_norm_w': 'grad_w', 'grad_la_w_out': 'grad_w', 'grad_sg_w_in': 'grad_w', 'grad_sg_ln_w': 'grad_w', 'grad_sg_ln_b': 'grad_w', 'grad_sg_w_s': 'grad_w', 'grad_sg_b_s': 'grad_w', 'grad_sg_w_out': 'grad_w', 'grad_ffn_w_up': 'grad_w', 'grad_ffn_w_down': 'grad_w', 'delta_norm_w': 'delta_w', 'delta_la_w_in': 'delta_w', 'delta_la_conv_w': 'delta_w', 'delta_la_a_log': 'delta_w', 'delta_la_dt_bias': 'delta_w', 'delta_la_out_norm_w': 'delta_w', 'delta_la_w_out': 'delta_w', 'delta_sg_w_in': 'delta_w', 'delta_sg_ln_w': 'delta_w', 'delta_sg_ln_b': 'delta_w', 'delta_sg_w_s': 'delta_w', 'delta_sg_b_s': 'delta_w', 'delta_sg_w_out': 'delta_w', 'delta_ffn_w_up': 'delta_w', 'delta_ffn_w_down': 'delta_w', 'new_m_norm_w': 'new_m', 'new_m_la_w_in': 'new_m', 'new_m_la_conv_w': 'new_m', 'new_m_la_a_log': 'new_m', 'new_m_la_dt_bias': 'new_m', 'new_m_la_out_norm_w': 'new_m', 'new_m_la_w_out': 'new_m', 'new_m_sg_w_in': 'new_m', 'new_m_sg_ln_w': 'new_m', 'new_m_sg_ln_b': 'new_m', 'new_m_sg_w_s': 'new_m', 'new_m_sg_b_s': 'new_m', 'new_m_sg_w_out': 'new_m', 'new_m_ffn_w_up': 'new_m', 'new_m_ffn_w_down': 'new_m', 'new_v_norm_w': 'new_v', 'new_v_la_w_in': 'new_v', 'new_v_la_conv_w': 'new_v', 'new_v_la_a_log': 'new_v', 'new_v_la_dt_bias': 'new_v', 'new_v_la_out_norm_w': 'new_v', 'new_v_la_w_out': 'new_v', 'new_v_sg_w_in': 'new_v', 'new_v_sg_ln_w': 'new_v', 'new_v_sg_ln_b': 'new_v', 'new_v_sg_w_s': 'new_v', 'new_v_sg_b_s': 'new_v', 'new_v_sg_w_out': 'new_v', 'new_v_ffn_w_up': 'new_v', 'new_v_ffn_w_down': 'new_v'}


def _forward(args):
    return _fwd_reference(*[args[k] for k in FWD_PARAMS])


def _output_shape():
    def fwd():
        inp = _fwd_setup_inputs(0)
        return _fwd_reference(*[inp[k] for k in FWD_PARAMS])
    out = _jax.eval_shape(fwd)
    return out.shape, out.dtype

N_MICROBATCH = 1
ADAM_LR = 0.001
ADAM_B1 = 0.9
ADAM_B2 = 0.999
ADAM_EPS = 1e-08
ADAM_WD = 0.01
ADAM_STEP = 10
PER_EXAMPLE_BATCH_AXIS = {'x': 0, 'loss_target': 0}
SHARED_INPUTS = []
_WEIGHT_DTYPES = {'norm_w': _jnp.float32, 'la_w_in': _jnp.float32, 'la_conv_w': _jnp.float32, 'la_a_log': _jnp.float32, 'la_dt_bias': _jnp.float32, 'la_out_norm_w': _jnp.float32, 'la_w_out': _jnp.float32, 'sg_w_in': _jnp.float32, 'sg_ln_w': _jnp.float32, 'sg_ln_b': _jnp.float32, 'sg_w_s': _jnp.float32, 'sg_b_s': _jnp.float32, 'sg_w_out': _jnp.float32, 'ffn_w_up': _jnp.float32, 'ffn_w_down': _jnp.float32}
MOMENT_SCALE = {'norm_w': 2.392707e+01, 'la_w_in': 5.025435e-01, 'la_conv_w': 1.975762e+00, 'la_a_log': 1.364039e+01, 'la_dt_bias': 1.289026e+01, 'la_out_norm_w': 1.715275e+01, 'la_w_out': 3.631384e+00, 'sg_w_in': 1.650965e+00, 'sg_ln_w': 1.467883e-01, 'sg_ln_b': 2.577248e-01, 'sg_w_s': 2.612931e-01, 'sg_b_s': 5.063824e-01, 'sg_w_out': 8.422098e+00, 'ffn_w_up': 1.368467e+00, 'ffn_w_down': 8.916791e+00}


def _to_microbatches(a, axis):
    t = _jnp.moveaxis(a, axis, 0)
    t = t.reshape((N_MICROBATCH, t.shape[0] // N_MICROBATCH) + t.shape[1:])
    return _jnp.moveaxis(t, 1, axis + 1)


def setup_inputs(seed: int = 0) -> dict:
    inp = _fwd_setup_inputs(seed)
    key = _jax.random.fold_in(_jax.random.key(seed), 7919)
    shape, _ = _output_shape()
    out = dict(inp)
    out["loss_target"] = _jax.random.normal(_jax.random.fold_in(key, 0), shape, _jnp.float32)
    for i, name in enumerate(TWIN_WEIGHTS):
        w = inp[name].astype(_jnp.float32)
        if MOMENT_SCALE is None:
            s = _jnp.sqrt(_jnp.mean(_jnp.square(w)) + 1e-30)
        else:
            s = MOMENT_SCALE[name]
        km, kv = _jax.random.split(_jax.random.fold_in(key, i + 1))
        out[name] = w
        out["m_" + name] = s * _jax.random.normal(km, w.shape, _jnp.float32)
        out["v_" + name] = (s * s) * _jax.random.uniform(kv, w.shape, _jnp.float32, 0.5, 1.5)
    if N_MICROBATCH > 1:
        for name, axis in PER_EXAMPLE_BATCH_AXIS.items():
            out[name] = _to_microbatches(out[name], axis)
    return {'x': out['x'], 'norm_w': out['norm_w'], 'la_w_in': out['la_w_in'], 'la_conv_w': out['la_conv_w'], 'la_a_log': out['la_a_log'], 'la_dt_bias': out['la_dt_bias'], 'la_out_norm_w': out['la_out_norm_w'], 'la_w_out': out['la_w_out'], 'sg_w_in': out['sg_w_in'], 'sg_ln_w': out['sg_ln_w'], 'sg_ln_b': out['sg_ln_b'], 'sg_w_s': out['sg_w_s'], 'sg_b_s': out['sg_b_s'], 'sg_w_out': out['sg_w_out'], 'ffn_w_up': out['ffn_w_up'], 'ffn_w_down': out['ffn_w_down'], 'loss_target': out['loss_target'], 'm_norm_w': out['m_norm_w'], 'm_la_w_in': out['m_la_w_in'], 'm_la_conv_w': out['m_la_conv_w'], 'm_la_a_log': out['m_la_a_log'], 'm_la_dt_bias': out['m_la_dt_bias'], 'm_la_out_norm_w': out['m_la_out_norm_w'], 'm_la_w_out': out['m_la_w_out'], 'm_sg_w_in': out['m_sg_w_in'], 'm_sg_ln_w': out['m_sg_ln_w'], 'm_sg_ln_b': out['m_sg_ln_b'], 'm_sg_w_s': out['m_sg_w_s'], 'm_sg_b_s': out['m_sg_b_s'], 'm_sg_w_out': out['m_sg_w_out'], 'm_ffn_w_up': out['m_ffn_w_up'], 'm_ffn_w_down': out['m_ffn_w_down'], 'v_norm_w': out['v_norm_w'], 'v_la_w_in': out['v_la_w_in'], 'v_la_conv_w': out['v_la_conv_w'], 'v_la_a_log': out['v_la_a_log'], 'v_la_dt_bias': out['v_la_dt_bias'], 'v_la_out_norm_w': out['v_la_out_norm_w'], 'v_la_w_out': out['v_la_w_out'], 'v_sg_w_in': out['v_sg_w_in'], 'v_sg_ln_w': out['v_sg_ln_w'], 'v_sg_ln_b': out['v_sg_ln_b'], 'v_sg_w_s': out['v_sg_w_s'], 'v_sg_b_s': out['v_sg_b_s'], 'v_sg_w_out': out['v_sg_w_out'], 'v_ffn_w_up': out['v_ffn_w_up'], 'v_ffn_w_down': out['v_ffn_w_down']}


def _loss(weights, diff, rest, loss_target):
    with _jax.named_scope("forward"):
        args = {**rest, TWIN_DIFF_INPUT: diff, **{k: w.astype(_WEIGHT_DTYPES[k]) for k, w in weights.items()}}
        y = _forward(args)
    with _jax.named_scope("loss_head"):
        err = _jnp.square(y.astype(_jnp.float32) - loss_target)
        return 0.5 * _jnp.sum(_jnp.mean(err, axis=-1)) if err.ndim else 0.5 * err


def _adamw(w, g, m, v):
    m = ADAM_B1 * m + (1.0 - ADAM_B1) * g
    v = ADAM_B2 * v + (1.0 - ADAM_B2) * _jnp.square(g)
    m_hat = m / (1.0 - ADAM_B1 ** ADAM_STEP)
    v_hat = v / (1.0 - ADAM_B2 ** ADAM_STEP)
    delta = -ADAM_LR * (m_hat / (_jnp.sqrt(v_hat) + ADAM_EPS) + ADAM_WD * w)
    return delta, m, v


def reference(x, norm_w, la_w_in, la_conv_w, la_a_log, la_dt_bias, la_out_norm_w, la_w_out, sg_w_in, sg_ln_w, sg_ln_b, sg_w_s, sg_b_s, sg_w_out, ffn_w_up, ffn_w_down, loss_target, m_norm_w, m_la_w_in, m_la_conv_w, m_la_a_log, m_la_dt_bias, m_la_out_norm_w, m_la_w_out, m_sg_w_in, m_sg_ln_w, m_sg_ln_b, m_sg_w_s, m_sg_b_s, m_sg_w_out, m_ffn_w_up, m_ffn_w_down, v_norm_w, v_la_w_in, v_la_conv_w, v_la_a_log, v_la_dt_bias, v_la_out_norm_w, v_la_w_out, v_sg_w_in, v_sg_ln_w, v_sg_ln_b, v_sg_w_s, v_sg_b_s, v_sg_w_out, v_ffn_w_up, v_ffn_w_down):
    given = dict(x=x, norm_w=norm_w, la_w_in=la_w_in, la_conv_w=la_conv_w, la_a_log=la_a_log, la_dt_bias=la_dt_bias, la_out_norm_w=la_out_norm_w, la_w_out=la_w_out, sg_w_in=sg_w_in, sg_ln_w=sg_ln_w, sg_ln_b=sg_ln_b, sg_w_s=sg_w_s, sg_b_s=sg_b_s, sg_w_out=sg_w_out, ffn_w_up=ffn_w_up, ffn_w_down=ffn_w_down, loss_target=loss_target, m_norm_w=m_norm_w, m_la_w_in=m_la_w_in, m_la_conv_w=m_la_conv_w, m_la_a_log=m_la_a_log, m_la_dt_bias=m_la_dt_bias, m_la_out_norm_w=m_la_out_norm_w, m_la_w_out=m_la_w_out, m_sg_w_in=m_sg_w_in, m_sg_ln_w=m_sg_ln_w, m_sg_ln_b=m_sg_ln_b, m_sg_w_s=m_sg_w_s, m_sg_b_s=m_sg_b_s, m_sg_w_out=m_sg_w_out, m_ffn_w_up=m_ffn_w_up, m_ffn_w_down=m_ffn_w_down, v_norm_w=v_norm_w, v_la_w_in=v_la_w_in, v_la_conv_w=v_la_conv_w, v_la_a_log=v_la_a_log, v_la_dt_bias=v_la_dt_bias, v_la_out_norm_w=v_la_out_norm_w, v_la_w_out=v_la_w_out, v_sg_w_in=v_sg_w_in, v_sg_ln_w=v_sg_ln_w, v_sg_ln_b=v_sg_ln_b, v_sg_w_s=v_sg_w_s, v_sg_b_s=v_sg_b_s, v_sg_w_out=v_sg_w_out, v_ffn_w_up=v_ffn_w_up, v_ffn_w_down=v_ffn_w_down)
    weights = {n: given[n] for n in TWIN_WEIGHTS}
    shared = {n: given[n] for n in SHARED_INPUTS}
    per_example = {n: given[n] for n in ['x']}
    grad_fn = _jax.value_and_grad(_loss, argnums=(0, 1))

    def one_microbatch(ex, loss_target):
        ex = dict(ex)
        diff = ex.pop(TWIN_DIFF_INPUT)
        return grad_fn(weights, diff, {**shared, **ex}, loss_target)

    if N_MICROBATCH == 1:
        loss, (grad_w, grad_x) = one_microbatch(per_example, given["loss_target"])
    else:
        def body(carry, xs):
            loss_sum, grad_sum = carry
            l_k, (gw_k, gx_k) = one_microbatch(xs[0], xs[1])
            with _jax.named_scope("update"):
                return (loss_sum + l_k, _jax.tree.map(_jnp.add, grad_sum, gw_k)), gx_k

        init = (_jnp.zeros((), _jnp.float32), _jax.tree.map(_jnp.zeros_like, weights))
        (loss, grad_w), grad_x = _jax.lax.scan(body, init, (per_example, given["loss_target"]))
    with _jax.named_scope("update"):
        delta_w, new_m, new_v = {}, {}, {}
        for n in TWIN_WEIGHTS:
            delta_w[n], new_m[n], new_v[n] = _adamw(weights[n], grad_w[n], given["m_" + n], given["v_" + n])
    return (loss, grad_x, *[grad_w[n] for n in TWIN_WEIGHTS], *[delta_w[n] for n in TWIN_WEIGHTS],
            *[new_m[n] for n in TWIN_WEIGHTS], *[new_v[n] for n in TWIN_WEIGHTS])
```

```python
import functools
import math

import jax
import jax.numpy as jnp
import numpy as np
from jax import lax
from jax.experimental import pallas as pl
from jax.experimental.pallas import tpu as pltpu

F32, BF16 = jnp.float32, jnp.bfloat16
_pallas_call = pl.pallas_call

N_DEV = 8
LANES = 128
V7X_VMEM_BYTES = 64 * 1024 * 1024
VMEM_LIMIT = (V7X_VMEM_BYTES * 3) // 4
HEAD_DIM = 128
LA_CHUNK = 64
SG_CHUNK = 128
SG_GROUPS = 8
CONV_WIDTH = 4
HALO = 8
ROPE_BASE = 10000.0
EPS = 1e-6
ADAM_LR, ADAM_B1, ADAM_B2, ADAM_EPS, ADAM_WD, ADAM_STEP = 0.001, 0.9, 0.999, 1e-08, 0.01, 10
HIGHEST = lax.Precision.HIGHEST
MESH_ID = pl.DeviceIdType.MESH


def _params(*sem):
    return pltpu.CompilerParams(dimension_semantics=sem or None, vmem_limit_bytes=VMEM_LIMIT)


def _tile(dim, pref):
    if dim <= pref:
        return dim
    t = (pref // LANES) * LANES
    while dim % t:
        t -= LANES
    return t


def _row_tile(rows, pref):
    t = min(rows, pref)
    while rows % t:
        t -= 8
    return t


def _all_gather(shard, name):
    def body(x_ref, out_ref, send_sems, recv_sems, local_sem):
        x, y, c = lax.axis_index("x"), lax.axis_index("y"), lax.axis_index("c")
        me, sibling = (x, y, c), (x, y, 1 - c)
        chips = [(1 - x, y), (x, 1 - y), (1 - x, 1 - y)]

        def rows(px, py, pc):
            return out_ref.at[4 * px + 2 * py + pc]

        def copy(k, block, to, src=None):
            return pltpu.make_async_remote_copy(
                src_ref=rows(*block) if src is None else src, dst_ref=rows(*block),
                send_sem=send_sems.at[k], recv_sem=recv_sems.at[k], device_id=to, device_id_type=MESH_ID)

        mine = pltpu.make_async_copy(x_ref, rows(*me), local_sem)
        mine.start()
        first = [copy(0, me, sibling, src=x_ref)]
        first += [copy(1 + j, me, (*chip, c), src=x_ref) for j, chip in enumerate(chips)]
        for cp in first:
            cp.start()
        passed = [copy(4 + j, (*chip, c), sibling) for j, chip in enumerate(chips)]
        for j, chip in enumerate(chips):
            copy(1 + j, (*chip, c), me).wait_recv()
            passed[j].start()
        copy(0, sibling, me).wait_recv()
        for j, chip in enumerate(chips):
            copy(4 + j, (*chip, 1 - c), me).wait_recv()
        for cp in first + passed:
            cp.wait_send()
        mine.wait()

    return _pallas_call(
        body, name=name,
        out_shape=jax.ShapeDtypeStruct((N_DEV,) + shard.shape, shard.dtype),
        in_specs=[pl.BlockSpec(memory_space=pl.ANY)],
        out_specs=pl.BlockSpec(memory_space=pl.ANY),
        scratch_shapes=[pltpu.SemaphoreType.DMA((7,)), pltpu.SemaphoreType.DMA((7,)), pltpu.SemaphoreType.DMA],
    )(shard)


def _all_to_all(blocks, name):
    def body(g_ref, out_ref, send_sems, recv_sems, local_sem):
        x, y, c = lax.axis_index("x"), lax.axis_index("y"), lax.axis_index("c")
        me = 4 * x + 2 * y + c
        mine = pltpu.make_async_copy(g_ref.at[me], out_ref.at[me], local_sem)
        mine.start()
        copies = []
        for k in range(1, N_DEV):
            px, py, pc = x ^ ((k >> 2) & 1), y ^ ((k >> 1) & 1), c ^ (k & 1)
            peer = 4 * px + 2 * py + pc
            copies.append(pltpu.make_async_remote_copy(
                src_ref=g_ref.at[peer], dst_ref=out_ref.at[me],
                send_sem=send_sems.at[k - 1], recv_sem=recv_sems.at[k - 1],
                device_id=(px, py, pc), device_id_type=MESH_ID))
        for cp in copies:
            cp.start()
        for cp in copies:
            cp.wait_recv()
        for cp in copies:
            cp.wait_send()
        mine.wait()

    return _pallas_call(
        body, name=name,
        out_shape=jax.ShapeDtypeStruct(blocks.shape, blocks.dtype),
        in_specs=[pl.BlockSpec(memory_space=pl.ANY)],
        out_specs=pl.BlockSpec(memory_space=pl.ANY),
        scratch_shapes=[pltpu.SemaphoreType.DMA((7,)), pltpu.SemaphoreType.DMA((7,)), pltpu.SemaphoreType.DMA],
    )(blocks)


def _matmul(a, b, mode, out_dtype, name):
    if mode == "nn":
        (m, k), (_, n) = a.shape, b.shape
    elif mode == "nt":
        (m, k), (n, _) = a.shape, b.shape
    else:
        (k, m), (_, n) = a.shape, b.shape
    tm, tn, tk = _tile(m, 1024), _tile(n, 1024), _tile(k, 512)
    nk = k // tk
    dims = {"nn": (((1,), (0,)), ((), ())), "nt": (((1,), (1,)), ((), ())), "tn": (((0,), (0,)), ((), ()))}[mode]

    def body(a_ref, b_ref, o_ref, acc):
        kk = pl.program_id(2)

        @pl.when(kk == 0)
        def _():
            acc[...] = jnp.zeros_like(acc)

        acc[...] += lax.dot_general(a_ref[...].astype(BF16), b_ref[...].astype(BF16), dims,
                                    preferred_element_type=F32)

        @pl.when(kk == nk - 1)
        def _():
            o_ref[...] = acc[...].astype(o_ref.dtype)

    a_spec = (pl.BlockSpec((tk, tm), lambda i, j, kk: (kk, i)) if mode == "tn"
              else pl.BlockSpec((tm, tk), lambda i, j, kk: (i, kk)))
    b_spec = (pl.BlockSpec((tn, tk), lambda i, j, kk: (j, kk)) if mode == "nt"
              else pl.BlockSpec((tk, tn), lambda i, j, kk: (kk, j)))
    return _pallas_call(
        body, name=name, grid=(m // tm, n // tn, nk),
        in_specs=[a_spec, b_spec],
        out_specs=pl.BlockSpec((tm, tn), lambda i, j, kk: (i, j)),
        out_shape=jax.ShapeDtypeStruct((m, n), out_dtype),
        scratch_shapes=[pltpu.VMEM((tm, tn), F32)],
        compiler_params=_params("parallel", "parallel", "arbitrary"),
    )(a, b)


def _col_spec(tb, spec):
    if isinstance(spec, tuple):
        arr, blk, width = spec
        return arr, pl.BlockSpec((tb, width), lambda i, blk=blk: (i, blk))
    return spec, pl.BlockSpec((tb, spec.shape[1]), lambda i: (i, 0))


def _full_spec(p):
    return pl.BlockSpec(p.shape, lambda i, nd=p.ndim: (0,) * nd)


def _rowwise(fn, xs, params, outs, tb, name):
    arrs, specs = zip(*[_col_spec(tb, s) for s in xs])
    t = arrs[0].shape[0]
    nx, npar = len(xs), len(params)

    def body(*refs):
        res = fn(*[r[...] for r in refs[:nx + npar]])
        for o_ref, r in zip(refs[nx + npar:], res):
            o_ref[...] = r.astype(o_ref.dtype)

    return _pallas_call(
        body, name=name, grid=(t // tb,),
        in_specs=list(specs) + [_full_spec(p) for p in params],
        out_specs=[pl.BlockSpec((tb, w), lambda i: (i, 0)) for w, _ in outs],
        out_shape=[jax.ShapeDtypeStruct((t, w), d) for w, d in outs],
        compiler_params=_params("parallel"),
    )(*arrs, *params)


def _rowwise_vjp(fn, xs, params, cts, dx_dtypes, tb, name, adds=None, primal_out=None, primal_width=1):
    arrs, specs = zip(*[_col_spec(tb, s) for s in xs])
    t = arrs[0].shape[0]
    nx, npar, nct = len(xs), len(params), len(cts)
    adds = adds or [None] * nx
    add_ix = [i for i in range(nx) if adds[i] is not None]
    dx_ix = [i for i in range(nx) if dx_dtypes[i] is not None]
    widths = [s.block_shape[1] for s in specs]

    def body(*refs):
        x_refs, p_refs = refs[:nx], refs[nx:nx + npar]
        ct_refs = refs[nx + npar:nx + npar + nct]
        add_refs = refs[nx + npar + nct:nx + npar + nct + len(add_ix)]
        o = nx + npar + nct + len(add_ix)
        dx_refs, dp_refs = refs[o:o + len(dx_ix)], refs[o + len(dx_ix):o + len(dx_ix) + npar]
        prim, vjp = jax.vjp(fn, *[r[...] for r in x_refs], *[r[...] for r in p_refs])
        grads = vjp(tuple(c[...].astype(p.dtype) for c, p in zip(ct_refs, prim)))
        for ref, i in zip(dx_refs, dx_ix):
            g = grads[i].astype(F32)
            if adds[i] is not None:
                g = g + add_refs[add_ix.index(i)][...].astype(F32)
            ref[...] = g.astype(ref.dtype)

        @pl.when(pl.program_id(0) == 0)
        def _():
            for ref in dp_refs:
                ref[...] = jnp.zeros_like(ref)

        for ref, g in zip(dp_refs, grads[nx:]):
            ref[...] += g.astype(F32)
        if primal_out is not None:
            refs[-1][...] = prim[primal_out].astype(refs[-1].dtype)

    ct_specs = [pl.BlockSpec((tb, c.shape[1]), lambda i: (i, 0)) for c in cts]
    add_specs = [pl.BlockSpec((tb, widths[i]), lambda i_: (i_, 0)) for i in add_ix]
    out_specs = [pl.BlockSpec((tb, widths[i]), lambda i_: (i_, 0)) for i in dx_ix]
    out_shape = [jax.ShapeDtypeStruct((t, widths[i]), dx_dtypes[i]) for i in dx_ix]
    out_specs += [_full_spec(p) for p in params]
    out_shape += [jax.ShapeDtypeStruct(p.shape, F32) for p in params]
    if primal_out is not None:
        out_specs.append(pl.BlockSpec((tb, primal_width), lambda i: (i, 0)))
        out_shape.append(jax.ShapeDtypeStruct((t, primal_width), F32))
    res = _pallas_call(
        body, name=name, grid=(t // tb,),
        in_specs=list(specs) + [_full_spec(p) for p in params] + ct_specs + add_specs,
        out_specs=out_specs, out_shape=out_shape,
        compiler_params=_params("arbitrary"),
    )(*arrs, *params, *cts, *[adds[i] for i in add_ix])
    ndx = len(dx_ix)
    out = (list(res[:ndx]), list(res[ndx:ndx + npar]))
    return out + (res[-1],) if primal_out is not None else out


def _rms(x, w):
    xf = x.astype(F32)
    return xf * lax.rsqrt(jnp.mean(xf * xf, axis=-1, keepdims=True) + EPS) * w


def _seg_norm(x, w):
    return (_rms(x, w).astype(BF16),)


def _seg_residual(h, y, w_post, w_pre):
    h2 = h.astype(F32) + _rms(y, w_post)
    return h2, _rms(h2, w_pre).astype(BF16)


def _seg_relu2(u):
    return (jnp.square(jax.nn.relu(u.astype(F32))).astype(BF16),)


def _seg_loss(h, y, tgt, w_post):
    err = h.astype(F32) + _rms(y, w_post) - tgt
    return (0.5 * jnp.mean(err * err, axis=-1, keepdims=True),)


def _seg_gelu_ln(pu, pv, ln_w, ln_b):
    u = jax.nn.gelu(pu.astype(F32))
    v = jax.nn.gelu(pv.astype(F32))
    mu = jnp.mean(v, axis=-1, keepdims=True)
    vc = v - mu
    var = jnp.mean(vc * vc, axis=-1, keepdims=True)
    return u.astype(BF16), (vc * lax.rsqrt(var + EPS) * ln_w + ln_b).astype(BF16)


def _make_seg_gates(n_heads):
    def seg(raw, a_log, dt_bias):
        lane = lax.broadcasted_iota(jnp.int32, raw.shape, 1)
        beta = jax.nn.sigmoid(raw)
        g = -jnp.exp(a_log) * jax.nn.softplus(raw + dt_bias)
        return (jnp.where(lane < n_heads, beta, jnp.where(lane < 2 * n_heads, g, 0.0)),)
    return seg


def _gate_fn(u, v, ws, bs):
    r = lax.broadcasted_iota(jnp.int32, (SG_CHUNK, SG_CHUNK), 0)
    c = lax.broadcasted_iota(jnp.int32, (SG_CHUNK, SG_CHUNK), 1)
    w = jnp.where(r >= c, ws[0], 0.0).astype(BF16)
    s = jnp.dot(w, v, preferred_element_type=F32) + bs[0]
    return (u.astype(F32) * s).astype(BF16)


def _sg_specs(gd):
    x_spec = pl.BlockSpec((SG_CHUNK, gd), lambda g, i: (i, g))
    ws_spec = pl.BlockSpec((1, SG_CHUNK, SG_CHUNK), lambda g, i: (g, 0, 0))
    bs_spec = pl.BlockSpec((1, SG_CHUNK, 1), lambda g, i: (g, 0, 0))
    return x_spec, ws_spec, bs_spec


def _spatial_gate(u, v, ws, bs, name):
    t, w = u.shape
    gd = w // SG_GROUPS
    x_spec, ws_spec, bs_spec = _sg_specs(gd)

    def body(u_ref, v_ref, ws_ref, bs_ref, o_ref):
        o_ref[...] = _gate_fn(u_ref[...], v_ref[...], ws_ref[...], bs_ref[...])

    return _pallas_call(
        body, name=name, grid=(SG_GROUPS, t // SG_CHUNK),
        in_specs=[x_spec, x_spec, ws_spec, bs_spec], out_specs=x_spec,
        out_shape=jax.ShapeDtypeStruct((t, w), BF16),
        compiler_params=_params("parallel", "parallel"),
    )(u, v, ws, bs)


def _spatial_gate_vjp(u, v, ws, bs, ct, name):
    t, w = u.shape
    gd = w // SG_GROUPS
    x_spec, ws_spec, bs_spec = _sg_specs(gd)

    def body(u_ref, v_ref, ws_ref, bs_ref, ct_ref, du_ref, dv_ref, dws_ref, dbs_ref):
        _, vjp = jax.vjp(_gate_fn, u_ref[...], v_ref[...], ws_ref[...], bs_ref[...])
        du, dv, dws, dbs = vjp(ct_ref[...])
        du_ref[...] = du
        dv_ref[...] = dv

        @pl.when(pl.program_id(1) == 0)
        def _():
            dws_ref[...] = jnp.zeros_like(dws_ref)
            dbs_ref[...] = jnp.zeros_like(dbs_ref)

        dws_ref[...] += dws
        dbs_ref[...] += dbs

    return _pallas_call(
        body, name=name, grid=(SG_GROUPS, t // SG_CHUNK),
        in_specs=[x_spec, x_spec, ws_spec, bs_spec, x_spec],
        out_specs=[x_spec, x_spec, ws_spec, bs_spec],
        out_shape=[jax.ShapeDtypeStruct((t, w), BF16), jax.ShapeDtypeStruct((t, w), BF16),
                   jax.ShapeDtypeStruct(ws.shape, F32), jax.ShapeDtypeStruct(bs.shape, F32)],
        compiler_params=_params("parallel", "arbitrary"),
    )(u, v, ws, bs, ct)


def _conv_silu(proj, w, width, name):
    t = proj.shape[0]
    tb = _row_tile(t, 256)
    hb = tb // HALO

    def body(prev_ref, x_ref, w_ref, o_ref, ext):
        keep = (pl.program_id(0) > 0).astype(F32)
        ext[0:HALO, :] = prev_ref[...].astype(F32) * keep
        ext[HALO:HALO + tb, :] = x_ref[...].astype(F32)
        acc = jnp.zeros((tb, width), F32)
        for j in range(CONV_WIDTH):
            o = HALO - (CONV_WIDTH - 1) + j
            acc = acc + w_ref[j:j + 1, :] * ext[o:o + tb, :]
        o_ref[...] = jax.nn.silu(acc).astype(o_ref.dtype)

    return _pallas_call(
        body, name=name, grid=(t // tb,),
        in_specs=[pl.BlockSpec((HALO, width), lambda i: (jnp.maximum(i * hb - 1, 0), 0)),
                  pl.BlockSpec((tb, width), lambda i: (i, 0)),
                  pl.BlockSpec((CONV_WIDTH, width), lambda i: (0, 0))],
        out_specs=pl.BlockSpec((tb, width), lambda i: (i, 0)),
        out_shape=jax.ShapeDtypeStruct((t, width), BF16),
        scratch_shapes=[pltpu.VMEM((tb + HALO, width), F32)],
        compiler_params=_params("parallel"),
    )(proj, proj, w)


def _conv_silu_vjp(proj, w, dy, width, name):
    t = proj.shape[0]
    tb = _row_tile(t, 256)
    hb = tb // HALO
    nb = t // tb
    te = tb + HALO

    def body(prev_ref, x_ref, next_ref, w_ref, dy_ref, dyn_ref, dx_ref, dw_ref, ext, dpre):
        i = pl.program_id(0)
        first = (i > 0).astype(F32)
        last = (i < nb - 1).astype(F32)
        ext[0:HALO, :] = prev_ref[...].astype(F32) * first
        ext[HALO:HALO + tb, :] = x_ref[...].astype(F32)
        ext[HALO + tb:, :] = next_ref[...].astype(F32) * last
        pre = jnp.zeros((te, width), F32)
        for j in range(CONV_WIDTH):
            o = HALO - (CONV_WIDTH - 1) + j
            pre = pre + w_ref[j:j + 1, :] * ext[o:o + te, :]
        sig = jax.nn.sigmoid(pre)
        dsilu = sig * (1.0 + pre * (1.0 - sig))
        dpre[0:tb, :] = dy_ref[...].astype(F32) * dsilu[0:tb, :]
        dpre[tb:, :] = dyn_ref[...].astype(F32) * last * dsilu[tb:, :]
        dx = jnp.zeros((tb, width), F32)
        for j in range(CONV_WIDTH):
            o = CONV_WIDTH - 1 - j
            dx = dx + w_ref[j:j + 1, :] * dpre[o:o + tb, :]
        dx_ref[...] = dx.astype(dx_ref.dtype)

        @pl.when(i == 0)
        def _():
            dw_ref[...] = jnp.zeros_like(dw_ref)

        own = dpre[0:tb, :]
        for j in range(CONV_WIDTH):
            o = HALO - (CONV_WIDTH - 1) + j
            dw_ref[j:j + 1, :] += jnp.sum(own * ext[o:o + tb, :], axis=0, keepdims=True)

    halo_prev = lambda i: (jnp.maximum(i * hb - 1, 0), 0)
    halo_next = lambda i: (jnp.minimum((i + 1) * hb, t // HALO - 1), 0)
    return _pallas_call(
        body, name=name, grid=(nb,),
        in_specs=[pl.BlockSpec((HALO, width), halo_prev),
                  pl.BlockSpec((tb, width), lambda i: (i, 0)),
                  pl.BlockSpec((HALO, width), halo_next),
                  pl.BlockSpec((CONV_WIDTH, width), lambda i: (0, 0)),
                  pl.BlockSpec((tb, width), lambda i: (i, 0)),
                  pl.BlockSpec((HALO, width), halo_next)],
        out_specs=[pl.BlockSpec((tb, width), lambda i: (i, 0)),
                   pl.BlockSpec((CONV_WIDTH, width), lambda i: (0, 0))],
        out_shape=[jax.ShapeDtypeStruct((t, width), BF16), jax.ShapeDtypeStruct((CONV_WIDTH, width), F32)],
        scratch_shapes=[pltpu.VMEM((tb + 2 * HALO, width), F32), pltpu.VMEM((te, width), F32)],
        compiler_params=_params("arbitrary"),
    )(proj, proj, proj, w, dy, dy)


def _bdot(a, b, ca, cb, precision=None):
    if precision is None:
        a, b = a.astype(BF16), b.astype(BF16)
    return lax.dot_general(a, b, (((ca,), (cb,)), ((0,), (0,))), preferred_element_type=F32, precision=precision)


def _bmm(a, b):
    return _bdot(a, b, 2, 1)


def _bmm_nt(a, b):
    return _bdot(a, b, 2, 2)


def _bmm_tn(a, b):
    return _bdot(a, b, 1, 1)


def _head_rms(o):
    return o * lax.rsqrt(jnp.mean(o * o, axis=-1, keepdims=True) + EPS)


def _l2norm(x):
    return x * lax.rsqrt(jnp.sum(x * x, axis=-1, keepdims=True) + 1e-6)


def _gdn_chunk(state, qc, kc, vc, z, beta_r, g_r, onw):
    cs = LA_CHUNK
    r = lax.broadcasted_iota(jnp.int32, (1, cs, cs), 1)
    c = lax.broadcasted_iota(jnp.int32, (1, cs, cs), 2)
    eye = (r == c).astype(F32)
    causal, strict = r >= c, r > c
    q = _l2norm(qc.astype(F32)) * (HEAD_DIM ** -0.5)
    k = _l2norm(kc.astype(F32))
    v = vc.astype(F32)
    g_col = jnp.sum(eye * g_r, axis=-1, keepdims=True)
    beta_col = jnp.sum(eye * beta_r, axis=-1, keepdims=True)
    gc_col = jnp.sum(causal.astype(F32) * g_r, axis=-1, keepdims=True)
    gc_row = jnp.sum((r <= c).astype(F32) * g_col, axis=-2, keepdims=True)
    gc_last = jnp.sum(g_r, axis=-1, keepdims=True)
    decay = jnp.where(causal, jnp.exp(jnp.where(causal, gc_col - gc_row, 0.0)), 0.0)
    kb = k * beta_col
    a = jnp.where(strict, _bmm_nt(kb, k) * decay, 0.0)
    t_inv = eye - a
    p = a
    for _ in range(int(math.log2(cs)) - 1):
        p = _bdot(p, p, 2, 1, HIGHEST)
        t_inv = t_inv + _bdot(t_inv, p, 2, 1, HIGHEST)
    eg = jnp.exp(gc_col)
    u = _bmm(t_inv, v * beta_col)
    w = _bmm(t_inv, kb * eg)
    qk = jnp.where(causal, _bmm_nt(q, k) * decay, 0.0)
    v_new = u - _bmm(w, state)
    o = _bmm(q * eg, state) + _bmm(qk, v_new)
    new_state = state * jnp.exp(gc_last) + _bmm_tn(k * jnp.exp(gc_last - gc_col), v_new)
    o = _head_rms(o) * onw * jax.nn.silu(z.astype(F32))
    return new_state, o


@jax.custom_vjp
def _swap_halves(x):
    return pltpu.roll(x, HEAD_DIM // 2, x.ndim - 1)


_swap_halves.defvjp(lambda x: (_swap_halves(x), None), lambda _, g: (_swap_halves(g),))


def _make_ret_chunk(cos2, sin2, d_mat, dec_q, dec_k, dec_c):
    def chunk(state, rq, rk, rv, rg):
        qf, kf = rq.astype(F32), rk.astype(F32)
        q = qf * cos2 + _swap_halves(qf) * sin2
        k = (kf * cos2 + _swap_halves(kf) * sin2) * (HEAD_DIM ** -0.5)
        v = rv.astype(F32)
        inner = _bmm(_bmm_nt(q, k) * d_mat, v)
        cross = _bmm(q * dec_q, state)
        new_state = state * dec_c + _bmm_tn(k * dec_k, v)
        return new_state, jax.nn.silu(rg.astype(F32)) * _head_rms(inner + cross)
    return chunk


def _split_heads(x, n_heads):
    return jnp.stack([x[:, h * HEAD_DIM:(h + 1) * HEAD_DIM] for h in range(n_heads)], axis=0)


def _store_heads(ref, col0, val):
    for h in range(val.shape[0]):
        ref[:, col0 + h * HEAD_DIM:col0 + (h + 1) * HEAD_DIM] = val[h].astype(ref.dtype)


def _scan_fwd(make_chunk, seqs, rows, consts, n_heads, name):
    cs, hk = LA_CHUNK, n_heads * HEAD_DIM
    t = seqs[0][0].shape[0]
    n = t // cs
    ns, nr, nc = len(seqs), len(rows), len(consts)

    def body(*refs):
        seq_refs, row_refs, const_refs = refs[:ns], refs[ns:ns + nr], refs[ns + nr:ns + nr + nc]
        o_ref, st_ref, state = refs[ns + nr + nc:]

        @pl.when(pl.program_id(0) == 0)
        def _():
            state[...] = jnp.zeros_like(state)

        st_ref[0] = state[...]
        chunk = make_chunk([r[0] for r in row_refs], [r[...] for r in const_refs])
        new_state, o = chunk(state[...], *[_split_heads(r[...], n_heads) for r in seq_refs])
        state[...] = new_state
        _store_heads(o_ref, 0, o)

    return _pallas_call(
        body, name=name, grid=(n,),
        in_specs=[pl.BlockSpec((cs, hk), lambda i, b=b: (i, b)) for _, b in seqs]
        + [pl.BlockSpec((1,) + a.shape[1:], lambda i, nd=a.ndim: (i,) + (0,) * (nd - 1)) for a in rows]
        + [_full_spec(a) for a in consts],
        out_specs=[pl.BlockSpec((cs, hk), lambda i: (i, 0)),
                   pl.BlockSpec((1, n_heads, HEAD_DIM, HEAD_DIM), lambda i: (i, 0, 0, 0))],
        out_shape=[jax.ShapeDtypeStruct((t, hk), BF16),
                   jax.ShapeDtypeStruct((n, n_heads, HEAD_DIM, HEAD_DIM), F32)],
        scratch_shapes=[pltpu.VMEM((n_heads, HEAD_DIM, HEAD_DIM), F32)],
        compiler_params=_params("arbitrary"),
    )(*[a for a, _ in seqs], *rows, *consts)


def _scan_bwd(make_chunk, seqs, rows, consts, states, d_out, n_heads, n_row_grads, n_const_grads, name):
    cs, hk = LA_CHUNK, n_heads * HEAD_DIM
    t = seqs[0][0].shape[0]
    n = t // cs
    ns, nr, nc = len(seqs), len(rows), len(consts)

    def body(*refs):
        seq_refs, row_refs, const_refs = refs[:ns], refs[ns:ns + nr], refs[ns + nr:ns + nr + nc]
        st_ref, do_ref = refs[ns + nr + nc:ns + nr + nc + 2]
        o = ns + nr + nc + 2
        dseq_ref = refs[o]
        drow_refs = refs[o + 1:o + 1 + n_row_grads]
        dconst_refs = refs[o + 1 + n_row_grads:o + 1 + n_row_grads + n_const_grads]
        d_state = refs[-1]

        @pl.when(pl.program_id(0) == 0)
        def _():
            d_state[...] = jnp.zeros_like(d_state)
            for ref in dconst_refs:
                ref[...] = jnp.zeros_like(ref)

        row_vals = [r[0] for r in row_refs]
        const_vals = [r[...] for r in const_refs]

        def fn(state, seq_vals, row_d, const_d):
            chunk = make_chunk(list(row_d) + row_vals[n_row_grads:], list(const_d) + const_vals[n_const_grads:])
            return chunk(state, *seq_vals)

        _, vjp = jax.vjp(fn, st_ref[0], tuple(_split_heads(r[...], n_heads) for r in seq_refs),
                         tuple(row_vals[:n_row_grads]), tuple(const_vals[:n_const_grads]))
        ds, dseq, drow, dconst = vjp((d_state[...], _split_heads(do_ref[...], n_heads).astype(F32)))
        d_state[...] = ds
        for j, g in enumerate(dseq):
            _store_heads(dseq_ref, j * hk, g)
        for ref, g in zip(drow_refs, drow):
            ref[0] = g
        for ref, g in zip(dconst_refs, dconst):
            ref[...] += g

    rev = lambda i: n - 1 - i
    row_spec = lambda a: pl.BlockSpec((1,) + a.shape[1:], lambda i, nd=a.ndim: (rev(i),) + (0,) * (nd - 1))
    res = _pallas_call(
        body, name=name, grid=(n,),
        in_specs=[pl.BlockSpec((cs, hk), lambda i, b=b: (rev(i), b)) for _, b in seqs]
        + [row_spec(a) for a in rows] + [_full_spec(a) for a in consts]
        + [pl.BlockSpec((1, n_heads, HEAD_DIM, HEAD_DIM), lambda i: (rev(i), 0, 0, 0)),
           pl.BlockSpec((cs, hk), lambda i, b=d_out[1]: (rev(i), b))],
        out_specs=[pl.BlockSpec((cs, ns * hk), lambda i: (rev(i), 0))]
        + [row_spec(a) for a in rows[:n_row_grads]] + [_full_spec(a) for a in consts[:n_const_grads]],
        out_shape=[jax.ShapeDtypeStruct((t, ns * hk), BF16)]
        + [jax.ShapeDtypeStruct(a.shape, F32) for a in rows[:n_row_grads]]
        + [jax.ShapeDtypeStruct(a.shape, F32) for a in consts[:n_const_grads]],
        scratch_shapes=[pltpu.VMEM((n_heads, HEAD_DIM, HEAD_DIM), F32)],
        compiler_params=_params("arbitrary"),
    )(*[a for a, _ in seqs], *rows, *consts, states, d_out[0])
    return res[0], list(res[1:1 + n_row_grads]), list(res[1 + n_row_grads:])


def _gdn_make_chunk(row_vals, const_vals):
    beta_r, g_r = row_vals
    (onw,) = const_vals
    return lambda state, qc, kc, vc, z: _gdn_chunk(state, qc, kc, vc, z, beta_r, g_r, onw)


def _ret_make_chunk(row_vals, const_vals):
    cos2, sin2 = row_vals
    d_mat, dec_q, dec_k, dec_c = const_vals
    return _make_ret_chunk(cos2, sin2, d_mat, dec_q, dec_k, dec_c)


def _adamw_math(w, g, m, v):
    m = ADAM_B1 * m + (1.0 - ADAM_B1) * g
    v = ADAM_B2 * v + (1.0 - ADAM_B2) * jnp.square(g)
    m_hat = m / (1.0 - ADAM_B1 ** ADAM_STEP)
    v_hat = v / (1.0 - ADAM_B2 ** ADAM_STEP)
    delta = -ADAM_LR * (m_hat / (jnp.sqrt(v_hat) + ADAM_EPS) + ADAM_WD * w)
    return delta, m, v


def _adamw(parts, w, m, v, name):
    rows, cols = w.shape
    tb = _row_tile(rows, max(8, (1 << 17) // cols // 8 * 8))

    def body(p_ref, w_ref, m_ref, v_ref, g_out, d_out, m_out, v_out):
        g = p_ref[0].astype(F32)
        for s in range(1, N_DEV):
            g = g + p_ref[s].astype(F32)
        delta, m_new, v_new = _adamw_math(w_ref[...], g, m_ref[...], v_ref[...])
        g_out[...] = g
        d_out[...] = delta
        m_out[...] = m_new
        v_out[...] = v_new

    spec = pl.BlockSpec((tb, cols), lambda i: (i, 0))
    return _pallas_call(
        body, name=name, grid=(rows // tb,),
        in_specs=[pl.BlockSpec((N_DEV, tb, cols), lambda i: (0, i, 0)), spec, spec, spec],
        out_specs=[spec] * 4, out_shape=[jax.ShapeDtypeStruct((rows, cols), F32)] * 4,
        compiler_params=_params("parallel"),
    )(parts, w, m, v)


def _pack(arrays, rows_multiple=8):
    flat = jnp.concatenate([a.reshape(-1).astype(F32) for a in arrays])
    n = flat.shape[0]
    rows = -(-n // LANES)
    rows = -(-rows // rows_multiple) * rows_multiple
    return jnp.pad(flat, (0, rows * LANES - n)).reshape(rows, LANES)


def _unpack(packed, shapes):
    flat = packed.reshape(-1)
    out, o = [], 0
    for s in shapes:
        n = int(np.prod(s))
        out.append(flat[o:o + n].reshape(s))
        o += n
    return out


def _gather_cols(shard, name):
    g = _all_gather(shard.astype(BF16), name)
    return jnp.transpose(g, (1, 0, 2)).reshape(shard.shape[0], N_DEV * shard.shape[1])


def _gather_rows(shard, name):
    g = _all_gather(shard.astype(BF16), name)
    return g.reshape(N_DEV * shard.shape[0], shard.shape[1])


def _scatter_cols(full):
    k, n = full.shape
    return jnp.transpose(full.astype(BF16).reshape(k, N_DEV, n // N_DEV), (1, 0, 2))


def _scatter_rows(full):
    k, n = full.shape
    return full.astype(BF16).reshape(N_DEV, k // N_DEV, n)


def kernel(x, norm_w, la_w_in, la_conv_w, la_a_log, la_dt_bias, la_out_norm_w, la_w_out, sg_w_in, sg_ln_w, sg_ln_b, sg_w_s, sg_b_s, sg_w_out, ffn_w_up, ffn_w_down, loss_target, m_norm_w, m_la_w_in, m_la_conv_w, m_la_a_log, m_la_dt_bias, m_la_out_norm_w, m_la_w_out, m_sg_w_in, m_sg_ln_w, m_sg_ln_b, m_sg_w_s, m_sg_b_s, m_sg_w_out, m_ffn_w_up, m_ffn_w_down, v_norm_w, v_la_w_in, v_la_conv_w, v_la_a_log, v_la_dt_bias, v_la_out_norm_w, v_la_w_out, v_sg_w_in, v_sg_ln_w, v_sg_ln_b, v_sg_w_s, v_sg_b_s, v_sg_w_out, v_ffn_w_up, v_ffn_w_down):
    weights = dict(norm_w=norm_w, la_w_in=la_w_in, la_conv_w=la_conv_w, la_a_log=la_a_log, la_dt_bias=la_dt_bias,
                   la_out_norm_w=la_out_norm_w, la_w_out=la_w_out, sg_w_in=sg_w_in, sg_ln_w=sg_ln_w, sg_ln_b=sg_ln_b,
                   sg_w_s=sg_w_s, sg_b_s=sg_b_s, sg_w_out=sg_w_out, ffn_w_up=ffn_w_up, ffn_w_down=ffn_w_down)
    mom1 = dict(norm_w=m_norm_w, la_w_in=m_la_w_in, la_conv_w=m_la_conv_w, la_a_log=m_la_a_log, la_dt_bias=m_la_dt_bias,
                la_out_norm_w=m_la_out_norm_w, la_w_out=m_la_w_out, sg_w_in=m_sg_w_in, sg_ln_w=m_sg_ln_w, sg_ln_b=m_sg_ln_b,
                sg_w_s=m_sg_w_s, sg_b_s=m_sg_b_s, sg_w_out=m_sg_w_out, ffn_w_up=m_ffn_w_up, ffn_w_down=m_ffn_w_down)
    mom2 = dict(norm_w=v_norm_w, la_w_in=v_la_w_in, la_conv_w=v_la_conv_w, la_a_log=v_la_a_log, la_dt_bias=v_la_dt_bias,
                la_out_norm_w=v_la_out_norm_w, la_w_out=v_la_w_out, sg_w_in=v_sg_w_in, sg_ln_w=v_sg_ln_w, sg_ln_b=v_sg_ln_b,
                sg_w_s=v_sg_w_s, sg_b_s=v_sg_b_s, sg_w_out=v_sg_w_out, ffn_w_up=v_ffn_w_up, ffn_w_down=v_ffn_w_down)
    names = list(weights)

    t, d = x.shape[1], x.shape[2]
    n_heads = la_a_log.shape[-1]
    hk = n_heads * HEAD_DIM
    cs = LA_CHUNK
    n_chunks = t // cs
    sg_width = sg_w_out.shape[1] * N_DEV
    me = 4 * lax.axis_index("x") + 2 * lax.axis_index("y") + lax.axis_index("c")
    xs = x[0]
    tgt = loss_target[0]

    w_in_full = _gather_cols(la_w_in[0], "ag_la_w_in")
    w_main = jnp.concatenate([w_in_full[:, :4 * hk], w_in_full[:, 4 * hk + 2 * n_heads:]], axis=1)
    w_gate = jnp.pad(w_in_full[:, 4 * hk:4 * hk + 2 * n_heads], ((0, 0), (0, LANES - 2 * n_heads)))
    w_la_out = _gather_rows(la_w_out[0], "ag_la_w_out")
    w_sg_in = _gather_cols(sg_w_in[0], "ag_sg_w_in")
    w_sg_out = _gather_rows(sg_w_out[0], "ag_sg_w_out")
    w_up = [_gather_cols(ffn_w_up[l], f"ag_ffn_w_up{l}") for l in range(2)]
    w_down = [_gather_rows(ffn_w_down[l], f"ag_ffn_w_down{l}") for l in range(2)]
    small_shard = _pack([norm_w, la_conv_w, sg_ln_w, sg_ln_b])
    small_all = _all_gather(small_shard, "ag_small")
    dsh = d // N_DEV
    nw_parts, conv_parts, lnw_parts, lnb_parts = zip(*[
        _unpack(small_all[j], [(2, 4, dsh), la_conv_w.shape[1:], (sg_width // N_DEV,), (sg_width // N_DEV,)])
        for j in range(N_DEV)])
    nw = jnp.concatenate(nw_parts, axis=-1)
    conv_w = jnp.transpose(jnp.concatenate(conv_parts, axis=0))
    ln_w = jnp.concatenate(lnw_parts)[None, :]
    ln_b = jnp.concatenate(lnb_parts)[None, :]
    nrm = lambda l, j: nw[l, j][None, :]

    tb_d = _row_tile(t, 128)
    tb_wide = _row_tile(t, 64)

    (y0,) = _rowwise(_seg_norm, [xs], [nrm(0, 0)], [(d, BF16)], tb_d, "f_norm0")
    proj = _matmul(y0, w_main, "nn", BF16, "f_proj")
    gate_raw = _matmul(y0, w_gate, "nn", F32, "f_gate")
    a_log_v = jnp.pad(la_a_log.reshape(1, n_heads), ((0, 0), (n_heads, LANES - 2 * n_heads)))
    dt_bias_v = jnp.pad(la_dt_bias.reshape(1, n_heads), ((0, 0), (n_heads, LANES - 2 * n_heads)))
    seg_gates = _make_seg_gates(n_heads)
    (bg,) = _rowwise(seg_gates, [gate_raw], [a_log_v, dt_bias_v], [(LANES, F32)], tb_d, "f_gates")
    to_rows = lambda a: jnp.transpose(a.reshape(n_chunks, cs, n_heads), (0, 2, 1))[:, :, None, :]
    beta_r, g_r = to_rows(bg[:, :n_heads]), to_rows(bg[:, n_heads:2 * n_heads])
    qkvc = _conv_silu(proj, conv_w, 3 * hk, "f_conv")
    onw = la_out_norm_w.reshape(1, 1, HEAD_DIM)
    gdn_seqs = [(qkvc, 0), (qkvc, 1), (qkvc, 2), (proj, 3)]
    o_a, gdn_states = _scan_fwd(_gdn_make_chunk, gdn_seqs, [beta_r, g_r], [onw], n_heads, "f_gdn")

    pos = jnp.arange(t, dtype=F32)
    inv_freq = 1.0 / (ROPE_BASE ** jnp.linspace(0.0, 1.0, HEAD_DIM // 2, dtype=F32))
    ang = pos[:, None] * inv_freq[None, :]
    cos2 = jnp.concatenate([jnp.cos(ang), jnp.cos(ang)], axis=-1).reshape(n_chunks, cs, HEAD_DIM)
    sin2 = jnp.concatenate([-jnp.sin(ang), jnp.sin(ang)], axis=-1).reshape(n_chunks, cs, HEAD_DIM)
    log_gamma = jnp.log1p(-jnp.power(2.0, -5.0 - jnp.arange(n_heads, dtype=F32)))
    cpos = jnp.arange(cs, dtype=F32)
    rel = cpos[:, None] - cpos[None, :]
    d_mat = jnp.where(rel >= 0, jnp.exp(jnp.where(rel >= 0, rel, 0.0) * log_gamma[:, None, None]), 0.0)
    dec_q = jnp.exp((cpos + 1.0) * log_gamma[:, None])[..., None]
    dec_k = jnp.exp((cs - 1.0 - cpos) * log_gamma[:, None])[..., None]
    dec_c = jnp.exp(cs * log_gamma)[:, None, None]
    ret_seqs = [(proj, 4), (proj, 5), (proj, 6), (proj, 7)]
    ret_consts = [d_mat, dec_q, dec_k, dec_c]
    o_b, ret_states = _scan_fwd(_ret_make_chunk, ret_seqs, [cos2, sin2], ret_consts, n_heads, "f_ret")
    o_mix = jnp.concatenate([o_a, o_b], axis=1)
    y1 = _matmul(o_mix, w_la_out, "nn", F32, "f_la_out")
    h1, a1 = _rowwise(_seg_residual, [xs, y1], [nrm(0, 1), nrm(0, 2)], [(d, F32), (d, BF16)], tb_d, "f_res0a")

    def ffn_fwd(a, l):
        u = _matmul(a, w_up[l], "nn", BF16, f"f_up{l}")
        (r,) = _rowwise(_seg_relu2, [u], [], [(u.shape[1], BF16)], tb_wide, f"f_relu2_{l}")
        return u, r, _matmul(r, w_down[l], "nn", F32, f"f_down{l}")

    u1, r1, y2 = ffn_fwd(a1, 0)
    h2, a2 = _rowwise(_seg_residual, [h1, y2], [nrm(0, 3), nrm(1, 0)], [(d, F32), (d, BF16)], tb_d, "f_res0b")

    p_sg = _matmul(a2, w_sg_in, "nn", BF16, "f_sg_in")
    ua, vn = _rowwise(_seg_gelu_ln, [(p_sg, 0, sg_width), (p_sg, 1, sg_width)], [ln_w, ln_b],
                      [(sg_width, BF16), (sg_width, BF16)], tb_wide, "f_gelu_ln")
    ws = sg_w_s[0]
    bs = sg_b_s[0][:, :, None]
    gated = _spatial_gate(ua, vn, ws, bs, "f_sgate")
    y3 = _matmul(gated, w_sg_out, "nn", F32, "f_sg_out")
    h3, a3 = _rowwise(_seg_residual, [h2, y3], [nrm(1, 1), nrm(1, 2)], [(d, F32), (d, BF16)], tb_d, "f_res1a")
    u2, r2, y4 = ffn_fwd(a3, 1)

    ones = jnp.ones((t, 1), F32)
    (dh3, dy4), (dnw13,), loss_rows = _rowwise_vjp(
        _seg_loss, [h3, y4, tgt], [nrm(1, 3)], [ones], [F32, BF16, None], tb_d, "b_loss", primal_out=0)
    loss = lax.psum(jnp.sum(loss_rows), ("x", "y", "c"))

    def ffn_bwd(dy, a, u, r, l):
        dr = _matmul(dy, w_down[l], "nt", BF16, f"b_down{l}")
        dw_down = _matmul(r, dy, "tn", BF16, f"b_dw_down{l}")
        (du,), _ = _rowwise_vjp(_seg_relu2, [u], [], [dr], [BF16], tb_wide, f"b_relu2_{l}")
        da = _matmul(du, w_up[l], "nt", BF16, f"b_up{l}")
        dw_up = _matmul(a, du, "tn", BF16, f"b_dw_up{l}")
        return da, dw_up, dw_down

    da3, dw_up1, dw_down1 = ffn_bwd(dy4, a3, u2, r2, 1)
    (dh2, dy3), (dnw11, dnw12) = _rowwise_vjp(
        _seg_residual, [h2, y3], [nrm(1, 1), nrm(1, 2)], [dh3, da3], [F32, BF16], tb_d, "b_res1a")
    dgated = _matmul(dy3, w_sg_out, "nt", BF16, "b_sg_out")
    dw_sg_out = _matmul(gated, dy3, "tn", BF16, "b_dw_sg_out")
    dua, dvn, dws, dbs = _spatial_gate_vjp(ua, vn, ws, bs, dgated, "b_sgate")
    (dpu, dpv), (dln_w, dln_b) = _rowwise_vjp(
        _seg_gelu_ln, [(p_sg, 0, sg_width), (p_sg, 1, sg_width)], [ln_w, ln_b], [dua, dvn], [BF16, BF16],
        tb_wide, "b_gelu_ln")
    dp_sg = jnp.concatenate([dpu, dpv], axis=1)
    da2 = _matmul(dp_sg, w_sg_in, "nt", BF16, "b_sg_in")
    dw_sg_in = _matmul(a2, dp_sg, "tn", BF16, "b_dw_sg_in")
    (dh1, dy2), (dnw03, dnw10) = _rowwise_vjp(
        _seg_residual, [h1, y2], [nrm(0, 3), nrm(1, 0)], [dh2, da2], [F32, BF16], tb_d, "b_res0b")
    da1, dw_up0, dw_down0 = ffn_bwd(dy2, a1, u1, r1, 0)
    (dx_res, dy1), (dnw01, dnw02) = _rowwise_vjp(
        _seg_residual, [xs, y1], [nrm(0, 1), nrm(0, 2)], [dh1, da1], [F32, BF16], tb_d, "b_res0a")
    do_mix = _matmul(dy1, w_la_out, "nt", BF16, "b_la_out")
    dw_la_out = _matmul(o_mix, dy1, "tn", BF16, "b_dw_la_out")

    d_ret, _, _ = _scan_bwd(_ret_make_chunk, ret_seqs, [cos2, sin2], ret_consts, ret_states, (do_mix, 1),
                            n_heads, 0, 0, "b_ret")
    d_gdn, (dbeta_r, dg_r), (donw,) = _scan_bwd(_gdn_make_chunk, gdn_seqs, [beta_r, g_r], [onw], gdn_states,
                                                (do_mix, 0), n_heads, 2, 1, "b_gdn")
    from_rows = lambda a: jnp.transpose(a[:, :, 0, :], (0, 2, 1)).reshape(t, n_heads)
    dbg = jnp.pad(jnp.concatenate([from_rows(dbeta_r), from_rows(dg_r)], axis=1), ((0, 0), (0, LANES - 2 * n_heads)))
    (dgate_raw,), (da_log_v, ddt_bias_v) = _rowwise_vjp(
        seg_gates, [gate_raw], [a_log_v, dt_bias_v], [dbg], [BF16], tb_d, "b_gates")
    dqkv, dconv_w = _conv_silu_vjp(proj, conv_w, d_gdn, 3 * hk, "b_conv")
    dproj = jnp.concatenate([dqkv, d_gdn[:, 3 * hk:], d_ret], axis=1)
    dy0 = _matmul(dproj, w_main, "nt", F32, "b_proj")
    dy0_gate = _matmul(dgate_raw, w_gate, "nt", F32, "b_gate")
    dw_main = _matmul(y0, dproj, "tn", BF16, "b_dw_proj")
    dw_gate = _matmul(y0, dgate_raw, "tn", BF16, "b_dw_gate")
    (grad_x,), (dnw00,) = _rowwise_vjp(
        _seg_norm, [xs], [nrm(0, 0)], [dy0 + dy0_gate], [F32], tb_d, "b_norm0", adds=[dx_res])
    dw_in_full = jnp.concatenate([dw_main[:, :4 * hk], dw_gate[:, :2 * n_heads], dw_main[:, 4 * hk:]], axis=1)

    big = {
        "la_w_in": (_scatter_cols(dw_in_full), la_w_in.shape),
        "la_w_out": (_scatter_rows(dw_la_out), la_w_out.shape),
        "sg_w_in": (_scatter_cols(dw_sg_in), sg_w_in.shape),
        "sg_w_out": (_scatter_rows(dw_sg_out), sg_w_out.shape),
    }
    outs = {}
    for name, (blocks, shape) in big.items():
        parts = _all_to_all(blocks, f"a2a_{name}")
        res = _adamw(parts, weights[name][0], mom1[name][0], mom2[name][0], f"adamw_{name}")
        outs[name] = [r.reshape(shape) for r in res]
    for name, dws_l, scatter in (("ffn_w_up", [dw_up0, dw_up1], _scatter_cols),
                                 ("ffn_w_down", [dw_down0, dw_down1], _scatter_rows)):
        res_l = []
        for l in range(2):
            parts = _all_to_all(scatter(dws_l[l]), f"a2a_{name}{l}")
            res_l.append(_adamw(parts, weights[name][l], mom1[name][l], mom2[name][l], f"adamw_{name}{l}"))
        outs[name] = [jnp.stack([res_l[0][j], res_l[1][j]]) for j in range(4)]

    dnorm = jnp.stack([jnp.concatenate([dnw00, dnw01, dnw02, dnw03], axis=0),
                       jnp.concatenate([dnw10, dnw11, dnw12, dnw13], axis=0)])
    small_grads = [dnorm, jnp.transpose(dconv_w), dln_w[0], dln_b[0],
                   da_log_v[:, n_heads:2 * n_heads], ddt_bias_v[:, n_heads:2 * n_heads],
                   donw.reshape(1, HEAD_DIM), dws[None], dbs[None, :, :, 0]]
    small_shapes = [g.shape for g in small_grads]
    small_parts = _all_gather(_pack(small_grads), "ag_small_grads")
    sharded = lambda full, axis, size: lax.dynamic_slice_in_dim(full, me * size, size, axis)
    take = [lambda g: sharded(g, 2, dsh), lambda g: sharded(g, 0, la_conv_w.shape[1])[None],
            lambda g: sharded(g, 0, sg_width // N_DEV)[None], lambda g: sharded(g, 0, sg_width // N_DEV)[None],
            lambda g: g, lambda g: g, lambda g: g, lambda g: g, lambda g: g]
    small_names = ["norm_w", "la_conv_w", "sg_ln_w", "sg_ln_b", "la_a_log", "la_dt_bias", "la_out_norm_w",
                   "sg_w_s", "sg_b_s"]
    own_parts = jnp.stack([
        _pack([f(g) for f, g in zip(take, _unpack(small_parts[j], small_shapes))]) for j in range(N_DEV)])
    pk = lambda src: _pack([src[n] for n in small_names])
    res = _adamw(own_parts, pk(weights), pk(mom1), pk(mom2), "adamw_small")
    own_shapes = [weights[n].shape for n in small_names]
    for j, r in enumerate(res):
        for n, val in zip(small_names, _unpack(r, own_shapes)):
            outs.setdefault(n, [None] * 4)[j] = val

    grad_x = grad_x[None]
    return (loss, grad_x, *[outs[n][0] for n in names], *[outs[n][1] for n in names],
            *[outs[n][2] for n in names], *[outs[n][3] for n in names])
```

```python
import functools
import math

import jax
import jax.numpy as jnp
import numpy as np
from jax import lax
from jax.experimental import pallas as pl
from jax.experimental.pallas import tpu as pltpu

F32, BF16 = jnp.float32, jnp.bfloat16
_pallas_call = pl.pallas_call

N_DEV = 8
LANES = 128
V7X_VMEM_BYTES = 64 * 1024 * 1024
VMEM_LIMIT = (V7X_VMEM_BYTES * 3) // 4
HEAD_DIM = 128
LA_CHUNK = 64
INV_BLOCK = 8
SG_CHUNK = 128
SG_GROUPS = 8
CONV_WIDTH = 4
HALO = 8
ROPE_BASE = 10000.0
EPS = 1e-6
ADAM_LR, ADAM_B1, ADAM_B2, ADAM_EPS, ADAM_WD, ADAM_STEP = 0.001, 0.9, 0.999, 1e-08, 0.01, 10
MESH_ID = pl.DeviceIdType.MESH


def _params(*sem):
    return pltpu.CompilerParams(dimension_semantics=sem or None, vmem_limit_bytes=VMEM_LIMIT)


def _tile(dim, pref):
    if dim <= pref:
        return dim
    t = (pref // LANES) * LANES
    while dim % t:
        t -= LANES
    return t


def _row_tile(rows, pref):
    t = min(rows, pref)
    while rows % t:
        t -= 8
    return t


def _all_gather(shard, name):
    def body(x_ref, out_ref, send_sems, recv_sems, local_sem):
        x, y, c = lax.axis_index("x"), lax.axis_index("y"), lax.axis_index("c")
        me, sibling = (x, y, c), (x, y, 1 - c)
        chips = [(1 - x, y), (x, 1 - y), (1 - x, 1 - y)]

        def rows(px, py, pc):
            return out_ref.at[4 * px + 2 * py + pc]

        def copy(k, block, to, src=None):
            return pltpu.make_async_remote_copy(
                src_ref=rows(*block) if src is None else src, dst_ref=rows(*block),
                send_sem=send_sems.at[k], recv_sem=recv_sems.at[k], device_id=to, device_id_type=MESH_ID)

        mine = pltpu.make_async_copy(x_ref, rows(*me), local_sem)
        mine.start()
        first = [copy(0, me, sibling, src=x_ref)]
        first += [copy(1 + j, me, (*chip, c), src=x_ref) for j, chip in enumerate(chips)]
        for cp in first:
            cp.start()
        passed = [copy(4 + j, (*chip, c), sibling) for j, chip in enumerate(chips)]
        for j, chip in enumerate(chips):
            copy(1 + j, (*chip, c), me).wait_recv()
            passed[j].start()
        copy(0, sibling, me).wait_recv()
        for j, chip in enumerate(chips):
            copy(4 + j, (*chip, 1 - c), me).wait_recv()
        for cp in first + passed:
            cp.wait_send()
        mine.wait()

    return _pallas_call(
        body, name=name,
        out_shape=jax.ShapeDtypeStruct((N_DEV,) + shard.shape, shard.dtype),
        in_specs=[pl.BlockSpec(memory_space=pl.ANY)],
        out_specs=pl.BlockSpec(memory_space=pl.ANY),
        scratch_shapes=[pltpu.SemaphoreType.DMA((7,)), pltpu.SemaphoreType.DMA((7,)), pltpu.SemaphoreType.DMA],
    )(shard)


class _Exchange:
    def __init__(self, kind, src):
        self.kind, self.src = kind, src
        shape = (N_DEV,) + src.shape if kind == "gather" else src.shape
        self.dst = jax.ShapeDtypeStruct(shape, src.dtype)

    def copies(self, src_ref, dst_ref, send_sems, recv_sems, local_sem):
        x, y, c = lax.axis_index("x"), lax.axis_index("y"), lax.axis_index("c")
        me = 4 * x + 2 * y + c
        pick = (lambda p: src_ref) if self.kind == "gather" else (lambda p: src_ref.at[p])
        local = pltpu.make_async_copy(pick(me), dst_ref.at[me], local_sem)
        remote = []
        for k in range(1, N_DEV):
            px, py, pc = x ^ ((k >> 2) & 1), y ^ ((k >> 1) & 1), c ^ (k & 1)
            remote.append(pltpu.make_async_remote_copy(
                src_ref=pick(4 * px + 2 * py + pc), dst_ref=dst_ref.at[me],
                send_sem=send_sems.at[k - 1], recv_sem=recv_sems.at[k - 1],
                device_id=(px, py, pc), device_id_type=MESH_ID))
        return local, remote


_ANY = pl.BlockSpec(memory_space=pl.ANY)
_EXCHANGE_SEMS = [pltpu.SemaphoreType.DMA((N_DEV - 1,)), pltpu.SemaphoreType.DMA((N_DEV - 1,)), pltpu.SemaphoreType.DMA]


def _carry(exchanges, src_refs, dst_refs, sem_refs, first, last, compute):
    def each():
        for e, (ex, s, d) in enumerate(zip(exchanges, src_refs, dst_refs)):
            yield ex.copies(s, d, *sem_refs[3 * e:3 * e + 3])

    if exchanges:
        @pl.when(first)
        def _():
            for local, remote in each():
                local.start()
                for cp in remote:
                    cp.start()

    compute()

    if exchanges:
        @pl.when(last)
        def _():
            for local, remote in each():
                for cp in remote:
                    cp.wait_recv()
                for cp in remote:
                    cp.wait_send()
                local.wait()


def _matmul(a, b, mode, out_dtype, name, b_blocked=False, out_blocked=False, extra=(), epilogue=None,
            out_dtypes=None, comm=()):
    if mode == "nn":
        (m, k), n = a.shape, b.shape[-1] * (N_DEV if b_blocked else 1)
    elif mode == "nt":
        (m, k), n = a.shape, b.shape[-2]
    else:
        (k, m), n = a.shape, b.shape[1]
    tm, tn, tk = _tile(m, 1024), _tile(n, 1024), _tile(k, 2048)
    if (b_blocked and mode == "nn") or out_blocked:
        tn = n // N_DEV
    if b_blocked and mode == "nt":
        tk = k // N_DEV
    nj, ni, nk = n // tn, m // tm, k // tk
    dims = {"nn": (((1,), (0,)), ((), ())), "nt": (((1,), (1,)), ((), ())), "tn": (((0,), (0,)), ((), ()))}[mode]
    out_dtypes = out_dtypes or [out_dtype]
    ne, nc, no = len(extra), len(comm), len(out_dtypes)

    def body(*refs):
        a_ref, b_ref = refs[:2]
        extra_refs, src_refs = refs[2:2 + ne], refs[2 + ne:2 + ne + nc]
        o = 2 + ne + nc
        out_refs, dst_refs = refs[o:o + no], refs[o + no:o + no + nc]
        rest = refs[o + no + nc:]
        acc, sems = (rest[0], rest[1:]) if nk > 1 else (None, rest)
        j, i, kk = pl.program_id(0), pl.program_id(1), pl.program_id(2)

        def finish(res):
            outs = epilogue(res, *[r[...] for r in extra_refs]) if epilogue else (res,)
            for ref, val in zip(out_refs, outs):
                ref[...] = val.astype(ref.dtype)

        def compute():
            prod = lax.dot_general(a_ref[...].astype(BF16), b_ref[...].astype(BF16), dims, preferred_element_type=F32)
            if nk == 1:
                finish(prod)
                return

            @pl.when(kk == 0)
            def _():
                acc[...] = prod

            @pl.when(kk > 0)
            def _():
                acc[...] += prod

            @pl.when(kk == nk - 1)
            def _():
                finish(acc[...])

        _carry(comm, src_refs, dst_refs, sems, (j == 0) & (i == 0) & (kk == 0),
               (j == nj - 1) & (i == ni - 1) & (kk == nk - 1), compute)

    a_spec = (pl.BlockSpec((tk, tm), lambda j, i, kk: (kk, i)) if mode == "tn"
              else pl.BlockSpec((tm, tk), lambda j, i, kk: (i, kk)))
    if b_blocked:
        b_spec = (pl.BlockSpec((None, tk, tn), lambda j, i, kk: (j, kk, 0)) if mode == "nn"
                  else pl.BlockSpec((None, tn, tk), lambda j, i, kk: (kk, j, 0)))
    else:
        b_spec = (pl.BlockSpec((tn, tk), lambda j, i, kk: (j, kk)) if mode == "nt"
                  else pl.BlockSpec((tk, tn), lambda j, i, kk: (kk, j)))
    tile_spec = pl.BlockSpec((tm, tn), lambda j, i, kk: (i, j))
    if out_blocked:
        o_spec = pl.BlockSpec((None, tm, tn), lambda j, i, kk: (j, i, 0))
        o_shape = (N_DEV, m, tn)
    else:
        o_spec, o_shape = tile_spec, (m, n)
    res = _pallas_call(
        body, name=name, grid=(nj, ni, nk),
        in_specs=[a_spec, b_spec] + [tile_spec] * ne + [_ANY] * nc,
        out_specs=[o_spec] * no + [_ANY] * nc,
        out_shape=[jax.ShapeDtypeStruct(o_shape, d) for d in out_dtypes] + [ex.dst for ex in comm],
        scratch_shapes=([pltpu.VMEM((tm, tn), F32)] if nk > 1 else []) + _EXCHANGE_SEMS * nc,
        compiler_params=_params("arbitrary", "arbitrary", "arbitrary"),
    )(a, b, *extra, *[ex.src for ex in comm])
    return res[0] if len(res) == 1 else res


def _col_spec(tb, spec):
    if isinstance(spec, tuple):
        arr, blk, width = spec
        return arr, pl.BlockSpec((tb, width), lambda i, blk=blk: (i, blk))
    return spec, pl.BlockSpec((tb, spec.shape[1]), lambda i: (i, 0))


def _full_spec(p):
    return pl.BlockSpec(p.shape, lambda i, nd=p.ndim: (0,) * nd)


def _rowwise(fn, xs, params, outs, tb, name):
    arrs, specs = zip(*[_col_spec(tb, s) for s in xs])
    t = arrs[0].shape[0]
    nx, npar = len(xs), len(params)

    def body(*refs):
        res = fn(*[r[...] for r in refs[:nx + npar]])
        for o_ref, r in zip(refs[nx + npar:], res):
            o_ref[...] = r.astype(o_ref.dtype)

    return _pallas_call(
        body, name=name, grid=(t // tb,),
        in_specs=list(specs) + [_full_spec(p) for p in params],
        out_specs=[pl.BlockSpec((tb, w), lambda i: (i, 0)) for w, _ in outs],
        out_shape=[jax.ShapeDtypeStruct((t, w), d) for w, d in outs],
        compiler_params=_params("parallel"),
    )(*arrs, *params)


def _rowwise_vjp(fn, xs, params, cts, dx_dtypes, tb, name, adds=None, primal_out=None, primal_width=1):
    arrs, specs = zip(*[_col_spec(tb, s) for s in xs])
    t = arrs[0].shape[0]
    nx, npar, nct = len(xs), len(params), len(cts)
    adds = adds or [None] * nx
    add_ix = [i for i in range(nx) if adds[i] is not None]
    dx_ix = [i for i in range(nx) if dx_dtypes[i] is not None]
    widths = [s.block_shape[1] for s in specs]

    def body(*refs):
        x_refs, p_refs = refs[:nx], refs[nx:nx + npar]
        ct_refs = refs[nx + npar:nx + npar + nct]
        add_refs = refs[nx + npar + nct:nx + npar + nct + len(add_ix)]
        o = nx + npar + nct + len(add_ix)
        dx_refs, dp_refs = refs[o:o + len(dx_ix)], refs[o + len(dx_ix):o + len(dx_ix) + npar]
        prim, vjp = jax.vjp(fn, *[r[...] for r in x_refs], *[r[...] for r in p_refs])
        grads = vjp(tuple(c[...].astype(p.dtype) for c, p in zip(ct_refs, prim)))
        for ref, i in zip(dx_refs, dx_ix):
            g = grads[i].astype(F32)
            if adds[i] is not None:
                g = g + add_refs[add_ix.index(i)][...].astype(F32)
            ref[...] = g.astype(ref.dtype)

        @pl.when(pl.program_id(0) == 0)
        def _():
            for ref in dp_refs:
                ref[...] = jnp.zeros_like(ref)

        for ref, g in zip(dp_refs, grads[nx:]):
            ref[...] += g.astype(F32)
        if primal_out is not None:
            refs[-1][...] = prim[primal_out].astype(refs[-1].dtype)

    ct_specs = [pl.BlockSpec((tb, c.shape[1]), lambda i: (i, 0)) for c in cts]
    add_specs = [pl.BlockSpec((tb, widths[i]), lambda i_: (i_, 0)) for i in add_ix]
    out_specs = [pl.BlockSpec((tb, widths[i]), lambda i_: (i_, 0)) for i in dx_ix]
    out_shape = [jax.ShapeDtypeStruct((t, widths[i]), dx_dtypes[i]) for i in dx_ix]
    out_specs += [_full_spec(p) for p in params]
    out_shape += [jax.ShapeDtypeStruct(p.shape, F32) for p in params]
    if primal_out is not None:
        out_specs.append(pl.BlockSpec((tb, primal_width), lambda i: (i, 0)))
        out_shape.append(jax.ShapeDtypeStruct((t, primal_width), F32))
    res = _pallas_call(
        body, name=name, grid=(t // tb,),
        in_specs=list(specs) + [_full_spec(p) for p in params] + ct_specs + add_specs,
        out_specs=out_specs, out_shape=out_shape,
        compiler_params=_params("arbitrary"),
    )(*arrs, *params, *cts, *[adds[i] for i in add_ix])
    ndx = len(dx_ix)
    out = (list(res[:ndx]), list(res[ndx:ndx + npar]))
    return out + (res[-1],) if primal_out is not None else out


def _rms(x, w):
    xf = x.astype(F32)
    return xf * lax.rsqrt(jnp.mean(xf * xf, axis=-1, keepdims=True) + EPS) * w


def _seg_norm(x, w):
    return (_rms(x, w),)


def _seg_residual(h, y, w_post, w_pre):
    h2 = h.astype(F32) + _rms(y, w_post)
    return h2, _rms(h2, w_pre)


def _seg_loss(h, y, tgt, w_post):
    err = h.astype(F32) + _rms(y, w_post) - tgt
    return (0.5 * jnp.mean(err * err, axis=-1, keepdims=True),)


def _seg_gelu_ln(pu, pv, ln_w, ln_b):
    u = jax.nn.gelu(pu.astype(F32))
    v = jax.nn.gelu(pv.astype(F32))
    mu = jnp.mean(v, axis=-1, keepdims=True)
    vc = v - mu
    var = jnp.mean(vc * vc, axis=-1, keepdims=True)
    return u, vc * lax.rsqrt(var + EPS) * ln_w + ln_b


def _make_seg_gates(n_heads):
    def seg(raw, a_log, dt_bias):
        lane = lax.broadcasted_iota(jnp.int32, raw.shape, 1)
        beta = jax.nn.sigmoid(raw)
        g = -jnp.exp(a_log) * jax.nn.softplus(raw + dt_bias)
        return (jnp.where(lane < n_heads, beta, jnp.where(lane < 2 * n_heads, g, 0.0)),)
    return seg


def _gate_fn(u, v, ws, bs):
    r = lax.broadcasted_iota(jnp.int32, (SG_CHUNK, SG_CHUNK), 0)
    c = lax.broadcasted_iota(jnp.int32, (SG_CHUNK, SG_CHUNK), 1)
    w = jnp.where(r >= c, ws[0], 0.0).astype(BF16)
    s = jnp.dot(w, v.astype(BF16), preferred_element_type=F32) + bs[0]
    return u.astype(F32) * s


def _sg_specs(gd):
    x_spec = pl.BlockSpec((SG_CHUNK, gd), lambda g, i: (i, g))
    ws_spec = pl.BlockSpec((1, SG_CHUNK, SG_CHUNK), lambda g, i: (g, 0, 0))
    bs_spec = pl.BlockSpec((1, SG_CHUNK, 1), lambda g, i: (g, 0, 0))
    return x_spec, ws_spec, bs_spec


def _spatial_gate(u, v, ws, bs, name):
    t, w = u.shape
    gd = w // SG_GROUPS
    x_spec, ws_spec, bs_spec = _sg_specs(gd)

    def body(u_ref, v_ref, ws_ref, bs_ref, o_ref):
        o_ref[...] = _gate_fn(u_ref[...], v_ref[...], ws_ref[...], bs_ref[...]).astype(o_ref.dtype)

    return _pallas_call(
        body, name=name, grid=(SG_GROUPS, t // SG_CHUNK),
        in_specs=[x_spec, x_spec, ws_spec, bs_spec], out_specs=x_spec,
        out_shape=jax.ShapeDtypeStruct((t, w), BF16),
        compiler_params=_params("parallel", "parallel"),
    )(u, v, ws, bs)


def _spatial_gate_vjp(u, v, ws, bs, ct, name, comm=()):
    t, w = u.shape
    gd = w // SG_GROUPS
    nchunks = t // SG_CHUNK
    nx = len(comm)
    x_spec, ws_spec, bs_spec = _sg_specs(gd)

    def body(*refs):
        u_ref, v_ref, ws_ref, bs_ref, ct_ref = refs[:5]
        src_refs = refs[5:5 + nx]
        du_ref, dv_ref, dws_ref, dbs_ref = refs[5 + nx:9 + nx]
        dst_refs, sems = refs[9 + nx:9 + 2 * nx], refs[9 + 2 * nx:]
        g, i = pl.program_id(0), pl.program_id(1)

        def compute():
            _, vjp = jax.vjp(_gate_fn, u_ref[...], v_ref[...].astype(F32), ws_ref[...], bs_ref[...])
            du, dv, dws, dbs = vjp(ct_ref[...])
            du_ref[...] = du
            dv_ref[...] = dv

            @pl.when(i == 0)
            def _():
                dws_ref[...] = jnp.zeros_like(dws_ref)
                dbs_ref[...] = jnp.zeros_like(dbs_ref)

            dws_ref[...] += dws
            dbs_ref[...] += dbs

        _carry(comm, src_refs, dst_refs, sems, (g == 0) & (i == 0), (g == SG_GROUPS - 1) & (i == nchunks - 1), compute)

    return _pallas_call(
        body, name=name, grid=(SG_GROUPS, nchunks),
        in_specs=[x_spec, x_spec, ws_spec, bs_spec, x_spec] + [_ANY] * nx,
        out_specs=[x_spec, x_spec, ws_spec, bs_spec] + [_ANY] * nx,
        out_shape=[jax.ShapeDtypeStruct((t, w), F32), jax.ShapeDtypeStruct((t, w), F32),
                   jax.ShapeDtypeStruct(ws.shape, F32), jax.ShapeDtypeStruct(bs.shape, F32)] + [ex.dst for ex in comm],
        scratch_shapes=_EXCHANGE_SEMS * nx,
        compiler_params=_params("arbitrary", "arbitrary"),
    )(u, v, ws, bs, ct, *[ex.src for ex in comm])


def _conv_silu(proj, w, width, name):
    t = proj.shape[0]
    tb = _row_tile(t, 128)
    hb = tb // HALO

    def body(prev_ref, x_ref, w_ref, o_ref, ext):
        keep = (pl.program_id(0) > 0).astype(F32)
        ext[0:HALO, :] = prev_ref[...].astype(F32) * keep
        ext[HALO:HALO + tb, :] = x_ref[...].astype(F32)
        acc = jnp.zeros((tb, width), F32)
        for j in range(CONV_WIDTH):
            o = HALO - (CONV_WIDTH - 1) + j
            acc = acc + w_ref[j:j + 1, :] * ext[o:o + tb, :]
        o_ref[...] = jax.nn.silu(acc).astype(o_ref.dtype)

    return _pallas_call(
        body, name=name, grid=(t // tb,),
        in_specs=[pl.BlockSpec((HALO, width), lambda i: (jnp.maximum(i * hb - 1, 0), 0)),
                  pl.BlockSpec((tb, width), lambda i: (i, 0)),
                  pl.BlockSpec((CONV_WIDTH, width), lambda i: (0, 0))],
        out_specs=pl.BlockSpec((tb, width), lambda i: (i, 0)),
        out_shape=jax.ShapeDtypeStruct((t, width), F32),
        scratch_shapes=[pltpu.VMEM((tb + HALO, width), F32)],
        compiler_params=_params("parallel"),
    )(proj, proj, w)


def _conv_silu_vjp(proj, w, dy, width, name):
    t = proj.shape[0]
    tb = _row_tile(t, 128)
    hb = tb // HALO
    nb = t // tb
    te = tb + HALO

    def body(prev_ref, x_ref, next_ref, w_ref, dy_ref, dyn_ref, dx_ref, dw_ref, ext, dpre):
        i = pl.program_id(0)
        first = (i > 0).astype(F32)
        last = (i < nb - 1).astype(F32)
        ext[0:HALO, :] = prev_ref[...].astype(F32) * first
        ext[HALO:HALO + tb, :] = x_ref[...].astype(F32)
        ext[HALO + tb:, :] = next_ref[...].astype(F32) * last
        pre = jnp.zeros((te, width), F32)
        for j in range(CONV_WIDTH):
            o = HALO - (CONV_WIDTH - 1) + j
            pre = pre + w_ref[j:j + 1, :] * ext[o:o + te, :]
        sig = jax.nn.sigmoid(pre)
        dsilu = sig * (1.0 + pre * (1.0 - sig))
        dpre[0:tb, :] = dy_ref[...].astype(F32) * dsilu[0:tb, :]
        dpre[tb:, :] = dyn_ref[...].astype(F32) * last * dsilu[tb:, :]
        dx = jnp.zeros((tb, width), F32)
        for j in range(CONV_WIDTH):
            o = CONV_WIDTH - 1 - j
            dx = dx + w_ref[j:j + 1, :] * dpre[o:o + tb, :]
        dx_ref[...] = dx.astype(dx_ref.dtype)

        @pl.when(i == 0)
        def _():
            dw_ref[...] = jnp.zeros_like(dw_ref)

        own = dpre[0:tb, :]
        for j in range(CONV_WIDTH):
            o = HALO - (CONV_WIDTH - 1) + j
            dw_ref[j:j + 1, :] += jnp.sum(own * ext[o:o + tb, :], axis=0, keepdims=True)

    halo_prev = lambda i: (jnp.maximum(i * hb - 1, 0), 0)
    halo_next = lambda i: (jnp.minimum((i + 1) * hb, t // HALO - 1), 0)
    return _pallas_call(
        body, name=name, grid=(nb,),
        in_specs=[pl.BlockSpec((HALO, width), halo_prev),
                  pl.BlockSpec((tb, width), lambda i: (i, 0)),
                  pl.BlockSpec((HALO, width), halo_next),
                  pl.BlockSpec((CONV_WIDTH, width), lambda i: (0, 0)),
                  pl.BlockSpec((tb, width), lambda i: (i, 0)),
                  pl.BlockSpec((HALO, width), halo_next)],
        out_specs=[pl.BlockSpec((tb, width), lambda i: (i, 0)),
                   pl.BlockSpec((CONV_WIDTH, width), lambda i: (0, 0))],
        out_shape=[jax.ShapeDtypeStruct((t, width), BF16), jax.ShapeDtypeStruct((CONV_WIDTH, width), F32)],
        scratch_shapes=[pltpu.VMEM((tb + 2 * HALO, width), F32), pltpu.VMEM((te, width), F32)],
        compiler_params=_params("arbitrary"),
    )(proj, proj, proj, w, dy, dy)


def _bdot(a, b, ca, cb):
    return lax.dot_general(a.astype(BF16), b.astype(BF16), (((ca,), (cb,)), ((0,), (0,))),
                           preferred_element_type=F32)


def _bdot3(a, b, ca, cb):
    a_hi, b_hi = a.astype(BF16), b.astype(BF16)
    a_lo, b_lo = (a - a_hi.astype(F32)).astype(BF16), (b - b_hi.astype(F32)).astype(BF16)
    dot = lambda p, q: lax.dot_general(p, q, (((ca,), (cb,)), ((0,), (0,))), preferred_element_type=F32)
    return dot(a_hi, b_hi) + (dot(a_hi, b_lo) + dot(a_lo, b_hi))


@jax.custom_vjp
def _bmm3(a, b):
    return _bdot3(a, b, 2, 1)


_bmm3.defvjp(lambda a, b: (_bmm3(a, b), (a, b)),
             lambda res, g: (_bdot3(g, res[1], 2, 2), _bdot3(res[0], g, 1, 1)))


def _bmm(a, b):
    return _bdot(a, b, 2, 1)


def _bmm_nt(a, b):
    return _bdot(a, b, 2, 2)


def _bmm_tn(a, b):
    return _bdot(a, b, 1, 1)


def _head_rms(o):
    return o * lax.rsqrt(jnp.mean(o * o, axis=-1, keepdims=True) + EPS)


def _l2norm(x):
    return x * lax.rsqrt(jnp.sum(x * x, axis=-1, keepdims=True) + 1e-6)


def _gdn_chunk(state, qc, kc, vc, z, beta_r, g_r, onw):
    cs = LA_CHUNK
    r = lax.broadcasted_iota(jnp.int32, (1, cs, cs), 1)
    c = lax.broadcasted_iota(jnp.int32, (1, cs, cs), 2)
    eye = (r == c).astype(F32)
    causal, strict = r >= c, r > c
    q = _l2norm(qc.astype(F32)) * (HEAD_DIM ** -0.5)
    k = _l2norm(kc.astype(F32))
    v = vc.astype(F32)
    g_col = jnp.sum(eye * g_r, axis=-1, keepdims=True)
    beta_col = jnp.sum(eye * beta_r, axis=-1, keepdims=True)
    gc_col = jnp.sum(causal.astype(F32) * g_r, axis=-1, keepdims=True)
    gc_row = jnp.sum((r <= c).astype(F32) * g_col, axis=-2, keepdims=True)
    gc_last = jnp.sum(g_r, axis=-1, keepdims=True)
    decay = jnp.where(causal, jnp.exp(jnp.where(causal, gc_col - gc_row, 0.0)), 0.0)
    kb = k * beta_col
    a = jnp.where(strict, _bmm_nt(kb, k) * decay, 0.0)
    same_block = (r // INV_BLOCK) == (c // INV_BLOCK)
    diag = jnp.where(same_block, a, 0.0)

    def nilpotent_inverse(x, order):
        inv, p = eye - x, x
        for _ in range(int(math.log2(order)) - 1):
            p = _bmm3(p, p)
            inv = inv + _bmm3(inv, p)
        return inv

    t_diag = nilpotent_inverse(diag, INV_BLOCK)
    t_inv = _bmm3(nilpotent_inverse(_bmm3(t_diag, a - diag), cs // INV_BLOCK), t_diag)
    eg = jnp.exp(gc_col)
    u = _bmm(t_inv, v * beta_col)
    w = _bmm(t_inv, kb * eg)
    qk = jnp.where(causal, _bmm_nt(q, k) * decay, 0.0)
    v_new = u - _bmm(w, state)
    o = _bmm(q * eg, state) + _bmm(qk, v_new)
    new_state = state * jnp.exp(gc_last) + _bmm_tn(k * jnp.exp(gc_last - gc_col), v_new)
    o = _head_rms(o) * onw * jax.nn.silu(z.astype(F32))
    return new_state, o


@jax.custom_vjp
def _swap_halves(x):
    return pltpu.roll(x, HEAD_DIM // 2, x.ndim - 1)


_swap_halves.defvjp(lambda x: (_swap_halves(x), None), lambda _, g: (_swap_halves(g),))


def _make_ret_chunk(cos2, sin2, d_mat, dec_q, dec_k, dec_c):
    def chunk(state, rq, rk, rv, rg):
        qf, kf = rq.astype(F32), rk.astype(F32)
        q = qf * cos2 + _swap_halves(qf) * sin2
        k = (kf * cos2 + _swap_halves(kf) * sin2) * (HEAD_DIM ** -0.5)
        v = rv.astype(F32)
        inner = _bmm(_bmm_nt(q, k) * d_mat, v)
        cross = _bmm(q * dec_q, state)
        new_state = state * dec_c + _bmm_tn(k * dec_k, v)
        return new_state, jax.nn.silu(rg.astype(F32)) * _head_rms(inner + cross)
    return chunk


def _split_heads(x, n_heads):
    return jnp.stack([x[:, h * HEAD_DIM:(h + 1) * HEAD_DIM] for h in range(n_heads)], axis=0)


def _store_heads(ref, col0, val):
    for h in range(val.shape[0]):
        ref[:, col0 + h * HEAD_DIM:col0 + (h + 1) * HEAD_DIM] = val[h].astype(ref.dtype)


def _scan_fwd(make_chunk, seqs, rows, consts, n_heads, name, comm=()):
    cs, hk = LA_CHUNK, n_heads * HEAD_DIM
    t = seqs[0][0].shape[0]
    n = t // cs
    ns, nr, nc, nx = len(seqs), len(rows), len(consts), len(comm)

    def body(*refs):
        seq_refs, row_refs, const_refs = refs[:ns], refs[ns:ns + nr], refs[ns + nr:ns + nr + nc]
        o = ns + nr + nc
        src_refs = refs[o:o + nx]
        o_ref, st_ref = refs[o + nx:o + nx + 2]
        dst_refs = refs[o + nx + 2:o + 2 * nx + 2]
        state, sems = refs[o + 2 * nx + 2], refs[o + 2 * nx + 3:]
        i = pl.program_id(0)

        def compute():
            @pl.when(i == 0)
            def _():
                state[...] = jnp.zeros_like(state)

            st_ref[0] = state[...]
            chunk = make_chunk([r[0] for r in row_refs], [r[...] for r in const_refs])
            new_state, out = chunk(state[...], *[_split_heads(r[...], n_heads) for r in seq_refs])
            state[...] = new_state
            _store_heads(o_ref, 0, out)

        _carry(comm, src_refs, dst_refs, sems, i == 0, i == n - 1, compute)

    return _pallas_call(
        body, name=name, grid=(n,),
        in_specs=[pl.BlockSpec((cs, hk), lambda i, b=b: (i, b)) for _, b in seqs]
        + [pl.BlockSpec((1,) + a.shape[1:], lambda i, nd=a.ndim: (i,) + (0,) * (nd - 1)) for a in rows]
        + [_full_spec(a) for a in consts] + [_ANY] * nx,
        out_specs=[pl.BlockSpec((cs, hk), lambda i: (i, 0)),
                   pl.BlockSpec((1, n_heads, HEAD_DIM, HEAD_DIM), lambda i: (i, 0, 0, 0))] + [_ANY] * nx,
        out_shape=[jax.ShapeDtypeStruct((t, hk), BF16),
                   jax.ShapeDtypeStruct((n, n_heads, HEAD_DIM, HEAD_DIM), F32)] + [ex.dst for ex in comm],
        scratch_shapes=[pltpu.VMEM((n_heads, HEAD_DIM, HEAD_DIM), F32)] + _EXCHANGE_SEMS * nx,
        compiler_params=_params("arbitrary"),
    )(*[a for a, _ in seqs], *rows, *consts, *[ex.src for ex in comm])


def _scan_bwd(make_chunk, seqs, rows, consts, states, d_out, n_heads, n_row_grads, n_const_grads, dseq_dtype,
              name, comm=()):
    cs, hk = LA_CHUNK, n_heads * HEAD_DIM
    t = seqs[0][0].shape[0]
    n = t // cs
    ns, nr, nc, nx = len(seqs), len(rows), len(consts), len(comm)
    n_grads = 1 + n_row_grads + n_const_grads

    def body(*refs):
        seq_refs, row_refs, const_refs = refs[:ns], refs[ns:ns + nr], refs[ns + nr:ns + nr + nc]
        st_ref, do_ref = refs[ns + nr + nc:ns + nr + nc + 2]
        o = ns + nr + nc + 2
        src_refs = refs[o:o + nx]
        o += nx
        dseq_ref = refs[o]
        drow_refs = refs[o + 1:o + 1 + n_row_grads]
        dconst_refs = refs[o + 1 + n_row_grads:o + n_grads]
        dst_refs = refs[o + n_grads:o + n_grads + nx]
        d_state, sems = refs[o + n_grads + nx], refs[o + n_grads + nx + 1:]
        i = pl.program_id(0)

        def compute():
            @pl.when(i == 0)
            def _():
                d_state[...] = jnp.zeros_like(d_state)
                for ref in dconst_refs:
                    ref[...] = jnp.zeros_like(ref)

            row_vals = [r[0] for r in row_refs]
            const_vals = [r[...] for r in const_refs]

            def fn(state, seq_vals, row_d, const_d):
                chunk = make_chunk(list(row_d) + row_vals[n_row_grads:], list(const_d) + const_vals[n_const_grads:])
                return chunk(state, *seq_vals)

            _, vjp = jax.vjp(fn, st_ref[0], tuple(_split_heads(r[...], n_heads) for r in seq_refs),
                             tuple(row_vals[:n_row_grads]), tuple(const_vals[:n_const_grads]))
            ds, dseq, drow, dconst = vjp((d_state[...], _split_heads(do_ref[...], n_heads).astype(F32)))
            d_state[...] = ds
            for j, g in enumerate(dseq):
                _store_heads(dseq_ref, j * hk, g)
            for ref, g in zip(drow_refs, drow):
                ref[0] = g
            for ref, g in zip(dconst_refs, dconst):
                ref[...] += g

        _carry(comm, src_refs, dst_refs, sems, i == 0, i == n - 1, compute)

    rev = lambda i: n - 1 - i
    row_spec = lambda a: pl.BlockSpec((1,) + a.shape[1:], lambda i, nd=a.ndim: (rev(i),) + (0,) * (nd - 1))
    res = _pallas_call(
        body, name=name, grid=(n,),
        in_specs=[pl.BlockSpec((cs, hk), lambda i, b=b: (rev(i), b)) for _, b in seqs]
        + [row_spec(a) for a in rows] + [_full_spec(a) for a in consts]
        + [pl.BlockSpec((1, n_heads, HEAD_DIM, HEAD_DIM), lambda i: (rev(i), 0, 0, 0)),
           pl.BlockSpec((cs, hk), lambda i, b=d_out[1]: (rev(i), b))] + [_ANY] * nx,
        out_specs=[pl.BlockSpec((cs, ns * hk), lambda i: (rev(i), 0))]
        + [row_spec(a) for a in rows[:n_row_grads]] + [_full_spec(a) for a in consts[:n_const_grads]] + [_ANY] * nx,
        out_shape=[jax.ShapeDtypeStruct((t, ns * hk), dseq_dtype)]
        + [jax.ShapeDtypeStruct(a.shape, F32) for a in rows[:n_row_grads]]
        + [jax.ShapeDtypeStruct(a.shape, F32) for a in consts[:n_const_grads]] + [ex.dst for ex in comm],
        scratch_shapes=[pltpu.VMEM((n_heads, HEAD_DIM, HEAD_DIM), F32)] + _EXCHANGE_SEMS * nx,
        compiler_params=_params("arbitrary"),
    )(*[a for a, _ in seqs], *rows, *consts, states, d_out[0], *[ex.src for ex in comm])
    return (res[0], list(res[1:1 + n_row_grads]), list(res[1 + n_row_grads:n_grads]), list(res[n_grads:]))


def _gdn_make_chunk(row_vals, const_vals):
    beta_r, g_r = row_vals
    (onw,) = const_vals
    return lambda state, qc, kc, vc, z: _gdn_chunk(state, qc, kc, vc, z, beta_r, g_r, onw)


def _ret_make_chunk(row_vals, const_vals):
    cos2, sin2 = row_vals
    d_mat, dec_q, dec_k, dec_c = const_vals
    return _make_ret_chunk(cos2, sin2, d_mat, dec_q, dec_k, dec_c)


def _adamw_math(w, g, m, v):
    m = ADAM_B1 * m + (1.0 - ADAM_B1) * g
    v = ADAM_B2 * v + (1.0 - ADAM_B2) * jnp.square(g)
    m_hat = m / (1.0 - ADAM_B1 ** ADAM_STEP)
    v_hat = v / (1.0 - ADAM_B2 ** ADAM_STEP)
    delta = -ADAM_LR * (m_hat / (jnp.sqrt(v_hat) + ADAM_EPS) + ADAM_WD * w)
    return delta, m, v


def _sum_parts(parts, name):
    n_parts, rows, cols = parts.shape
    tb = _row_tile(rows, 512)

    def body(p_ref, o_ref):
        g = p_ref[0].astype(F32)
        for s in range(1, n_parts):
            g = g + p_ref[s].astype(F32)
        o_ref[...] = g

    return _pallas_call(
        body, name=name, grid=(rows // tb,),
        in_specs=[pl.BlockSpec((n_parts, tb, cols), lambda i: (0, i, 0))],
        out_specs=pl.BlockSpec((tb, cols), lambda i: (i, 0)),
        out_shape=jax.ShapeDtypeStruct((rows, cols), F32),
        compiler_params=_params("parallel"),
    )(parts)


def _adamw(parts, w, m, v, name):
    rows, cols = w.shape
    n_parts = parts.shape[0]
    tb = _row_tile(rows, max(8, (1 << 17) // cols // 8 * 8))

    def body(p_ref, w_ref, m_ref, v_ref, g_out, d_out, m_out, v_out):
        g = p_ref[0].astype(F32)
        for s in range(1, n_parts):
            g = g + p_ref[s].astype(F32)
        delta, m_new, v_new = _adamw_math(w_ref[...], g, m_ref[...], v_ref[...])
        g_out[...] = g
        d_out[...] = delta
        m_out[...] = m_new
        v_out[...] = v_new

    spec = pl.BlockSpec((tb, cols), lambda i: (i, 0))
    return _pallas_call(
        body, name=name, grid=(rows // tb,),
        in_specs=[pl.BlockSpec((n_parts, tb, cols), lambda i: (0, i, 0)), spec, spec, spec],
        out_specs=[spec] * 4, out_shape=[jax.ShapeDtypeStruct((rows, cols), F32)] * 4,
        compiler_params=_params("parallel"),
    )(parts, w, m, v)


def _pack(arrays, rows_multiple=8):
    flat = jnp.concatenate([a.reshape(-1).astype(F32) for a in arrays])
    n = flat.shape[0]
    rows = -(-n // LANES)
    rows = -(-rows // rows_multiple) * rows_multiple
    return jnp.pad(flat, (0, rows * LANES - n)).reshape(rows, LANES)


def _unpack(packed, shapes):
    flat = packed.reshape(-1)
    out, o = [], 0
    for s in shapes:
        n = int(np.prod(s))
        out.append(flat[o:o + n].reshape(s))
        o += n
    return out


def _gather(shard):
    return _Exchange("gather", shard.astype(BF16))


def _rows_of(gathered):
    return gathered.reshape(gathered.shape[0] * gathered.shape[1], gathered.shape[2])


def _scatter_rows(full):
    k, n = full.shape
    return _Exchange("scatter", full.reshape(N_DEV, k // N_DEV, n))


def _relu2_epilogue(u):
    return u, jnp.square(jax.nn.relu(u))


def _relu2_vjp_epilogue(dr, u):
    return (dr * (2.0 * jax.nn.relu(u.astype(F32))),)


def kernel(x, norm_w, la_w_in, la_conv_w, la_a_log, la_dt_bias, la_out_norm_w, la_w_out, sg_w_in, sg_ln_w, sg_ln_b, sg_w_s, sg_b_s, sg_w_out, ffn_w_up, ffn_w_down, loss_target, m_norm_w, m_la_w_in, m_la_conv_w, m_la_a_log, m_la_dt_bias, m_la_out_norm_w, m_la_w_out, m_sg_w_in, m_sg_ln_w, m_sg_ln_b, m_sg_w_s, m_sg_b_s, m_sg_w_out, m_ffn_w_up, m_ffn_w_down, v_norm_w, v_la_w_in, v_la_conv_w, v_la_a_log, v_la_dt_bias, v_la_out_norm_w, v_la_w_out, v_sg_w_in, v_sg_ln_w, v_sg_ln_b, v_sg_w_s, v_sg_b_s, v_sg_w_out, v_ffn_w_up, v_ffn_w_down):
    weights = dict(norm_w=norm_w, la_w_in=la_w_in, la_conv_w=la_conv_w, la_a_log=la_a_log, la_dt_bias=la_dt_bias,
                   la_out_norm_w=la_out_norm_w, la_w_out=la_w_out, sg_w_in=sg_w_in, sg_ln_w=sg_ln_w, sg_ln_b=sg_ln_b,
                   sg_w_s=sg_w_s, sg_b_s=sg_b_s, sg_w_out=sg_w_out, ffn_w_up=ffn_w_up, ffn_w_down=ffn_w_down)
    mom1 = dict(norm_w=m_norm_w, la_w_in=m_la_w_in, la_conv_w=m_la_conv_w, la_a_log=m_la_a_log, la_dt_bias=m_la_dt_bias,
                la_out_norm_w=m_la_out_norm_w, la_w_out=m_la_w_out, sg_w_in=m_sg_w_in, sg_ln_w=m_sg_ln_w, sg_ln_b=m_sg_ln_b,
                sg_w_s=m_sg_w_s, sg_b_s=m_sg_b_s, sg_w_out=m_sg_w_out, ffn_w_up=m_ffn_w_up, ffn_w_down=m_ffn_w_down)
    mom2 = dict(norm_w=v_norm_w, la_w_in=v_la_w_in, la_conv_w=v_la_conv_w, la_a_log=v_la_a_log, la_dt_bias=v_la_dt_bias,
                la_out_norm_w=v_la_out_norm_w, la_w_out=v_la_w_out, sg_w_in=v_sg_w_in, sg_ln_w=v_sg_ln_w, sg_ln_b=v_sg_ln_b,
                sg_w_s=v_sg_w_s, sg_b_s=v_sg_b_s, sg_w_out=v_sg_w_out, ffn_w_up=v_ffn_w_up, ffn_w_down=v_ffn_w_down)
    names = list(weights)

    t, d = x.shape[1], x.shape[2]
    n_heads = la_a_log.shape[-1]
    hk = n_heads * HEAD_DIM
    cs = LA_CHUNK
    n_chunks = t // cs
    sg_width = sg_w_out.shape[1] * N_DEV
    me = 4 * lax.axis_index("x") + 2 * lax.axis_index("y") + lax.axis_index("c")
    xs = x[0]
    tgt = loss_target[0]

    w_in_g = _all_gather(la_w_in[0].astype(BF16), "ag_la_w_in")
    w_in_full = jnp.concatenate([w_in_g[j] for j in range(N_DEV)], axis=1)
    w_main = jnp.concatenate([w_in_full[:, :4 * hk], w_in_full[:, 4 * hk + 2 * n_heads:]], axis=1)
    w_gate = jnp.pad(w_in_full[:, 4 * hk:4 * hk + 2 * n_heads], ((0, 0), (0, LANES - 2 * n_heads)))
    conv_t = lambda a: jnp.transpose(a[0])
    small_local = lambda src: [src["norm_w"], conv_t(src["la_conv_w"]), src["sg_ln_w"], src["sg_ln_b"]]
    small_all = _all_gather(_pack(small_local(weights)), "ag_small")
    dsh = d // N_DEV
    csh = la_conv_w.shape[1]
    wsh = sg_width // N_DEV
    nw_parts, conv_parts, lnw_parts, lnb_parts = zip(*[
        _unpack(small_all[j], [(2, 4, dsh), (CONV_WIDTH, csh), (wsh,), (wsh,)]) for j in range(N_DEV)])
    nw = jnp.concatenate(nw_parts, axis=-1)
    conv_w = jnp.concatenate(conv_parts, axis=1)
    ln_w = jnp.concatenate(lnw_parts)[None, :]
    ln_b = jnp.concatenate(lnb_parts)[None, :]
    nrm = lambda l, j: nw[l, j][None, :]

    tb_d = _row_tile(t, 128)
    tb_wide = _row_tile(t, 64)

    (y0,) = _rowwise(_seg_norm, [xs], [nrm(0, 0)], [(d, BF16)], tb_d, "f_norm0")
    proj, w_la_out_g, w_up0_g = _matmul(y0, w_main, "nn", F32, "f_proj",
                                        comm=[_gather(la_w_out[0]), _gather(ffn_w_up[0])])
    gate_raw = _matmul(y0, w_gate, "nn", F32, "f_gate")
    a_log_v = jnp.pad(la_a_log.reshape(1, n_heads), ((0, 0), (n_heads, LANES - 2 * n_heads)))
    dt_bias_v = jnp.pad(la_dt_bias.reshape(1, n_heads), ((0, 0), (n_heads, LANES - 2 * n_heads)))
    seg_gates = _make_seg_gates(n_heads)
    (bg,) = _rowwise(seg_gates, [gate_raw], [a_log_v, dt_bias_v], [(LANES, F32)], tb_d, "f_gates")
    to_rows = lambda a: jnp.transpose(a.reshape(n_chunks, cs, n_heads), (0, 2, 1))[:, :, None, :]
    beta_r, g_r = to_rows(bg[:, :n_heads]), to_rows(bg[:, n_heads:2 * n_heads])
    qkvc = _conv_silu(proj, conv_w, 3 * hk, "f_conv")
    onw = la_out_norm_w.reshape(1, 1, HEAD_DIM)
    gdn_seqs = [(qkvc, 0), (qkvc, 1), (qkvc, 2), (proj, 3)]
    o_a, gdn_states, w_down0_g = _scan_fwd(_gdn_make_chunk, gdn_seqs, [beta_r, g_r], [onw], n_heads, "f_gdn",
                                           comm=[_gather(ffn_w_down[0])])

    pos = jnp.arange(t, dtype=F32)
    inv_freq = 1.0 / (ROPE_BASE ** jnp.linspace(0.0, 1.0, HEAD_DIM // 2, dtype=F32))
    ang = pos[:, None] * inv_freq[None, :]
    cos2 = jnp.concatenate([jnp.cos(ang), jnp.cos(ang)], axis=-1).reshape(n_chunks, cs, HEAD_DIM)
    sin2 = jnp.concatenate([-jnp.sin(ang), jnp.sin(ang)], axis=-1).reshape(n_chunks, cs, HEAD_DIM)
    log_gamma = jnp.log1p(-jnp.power(2.0, -5.0 - jnp.arange(n_heads, dtype=F32)))
    cpos = jnp.arange(cs, dtype=F32)
    rel = cpos[:, None] - cpos[None, :]
    d_mat = jnp.where(rel >= 0, jnp.exp(jnp.where(rel >= 0, rel, 0.0) * log_gamma[:, None, None]), 0.0)
    dec_q = jnp.exp((cpos + 1.0) * log_gamma[:, None])[..., None]
    dec_k = jnp.exp((cs - 1.0 - cpos) * log_gamma[:, None])[..., None]
    dec_c = jnp.exp(cs * log_gamma)[:, None, None]
    ret_seqs = [(proj, 4), (proj, 5), (proj, 6), (proj, 7)]
    ret_consts = [d_mat, dec_q, dec_k, dec_c]
    o_b, ret_states = _scan_fwd(_ret_make_chunk, ret_seqs, [cos2, sin2], ret_consts, n_heads, "f_ret")
    o_mix = jnp.concatenate([o_a, o_b], axis=1)
    y1 = _matmul(o_mix, _rows_of(w_la_out_g), "nn", F32, "f_la_out")
    h1, a1 = _rowwise(_seg_residual, [xs, y1], [nrm(0, 1), nrm(0, 2)], [(d, F32), (d, BF16)], tb_d, "f_res0a")
    u1, r1, w_sg_in_g = _matmul(a1, w_up0_g, "nn", None, "f_up0", b_blocked=True, epilogue=_relu2_epilogue,
                                out_dtypes=[F32, BF16], comm=[_gather(sg_w_in[0])])
    y2, w_sg_out_g = _matmul(r1, _rows_of(w_down0_g), "nn", F32, "f_down0", comm=[_gather(sg_w_out[0])])
    h2, a2 = _rowwise(_seg_residual, [h1, y2], [nrm(0, 3), nrm(1, 0)], [(d, F32), (d, BF16)], tb_d, "f_res0b")

    p_sg, w_up1_g = _matmul(a2, w_sg_in_g, "nn", F32, "f_sg_in", b_blocked=True,
                            comm=[_gather(ffn_w_up[1])])
    ua, vn = _rowwise(_seg_gelu_ln, [(p_sg, 0, sg_width), (p_sg, 1, sg_width)], [ln_w, ln_b],
                      [(sg_width, F32), (sg_width, BF16)], tb_wide, "f_gelu_ln")
    ws = sg_w_s[0]
    bs = sg_b_s[0][:, :, None]
    gated = _spatial_gate(ua, vn, ws, bs, "f_sgate")
    y3 = _matmul(gated, _rows_of(w_sg_out_g), "nn", F32, "f_sg_out")
    h3, a3 = _rowwise(_seg_residual, [h2, y3], [nrm(1, 1), nrm(1, 2)], [(d, F32), (d, BF16)], tb_d, "f_res1a")
    u2, r2, w_down1_g = _matmul(a3, w_up1_g, "nn", None, "f_up1", b_blocked=True, epilogue=_relu2_epilogue,
                                out_dtypes=[F32, BF16], comm=[_gather(ffn_w_down[1])])
    y4 = _matmul(r2, _rows_of(w_down1_g), "nn", F32, "f_down1")

    ones = jnp.ones((t, 1), F32)
    (dh3, dy4), (dnw13,), loss_rows = _rowwise_vjp(
        _seg_loss, [h3, y4, tgt], [nrm(1, 3)], [ones], [F32, BF16, None], tb_d, "b_loss", primal_out=0)
    loss = lax.psum(jnp.sum(loss_rows), ("x", "y", "c"))

    du2 = _matmul(dy4, _rows_of(w_down1_g), "nt", BF16, "b_down1", extra=[u2], epilogue=_relu2_vjp_epilogue)
    dw_down1 = _matmul(r2, dy4, "tn", BF16, "b_dw_down1")
    da3, parts_down1 = _matmul(du2, w_up1_g, "nt", F32, "b_up1", b_blocked=True, comm=[_scatter_rows(dw_down1)])
    dw_up1 = _matmul(a3, du2, "tn", BF16, "b_dw_up1", out_blocked=True)
    (dh2, dy3), (dnw11, dnw12) = _rowwise_vjp(
        _seg_residual, [h2, y3], [nrm(1, 1), nrm(1, 2)], [dh3, da3], [F32, BF16], tb_d, "b_res1a")
    dgated = _matmul(dy3, _rows_of(w_sg_out_g), "nt", F32, "b_sg_out")
    dw_sg_out = _matmul(gated, dy3, "tn", BF16, "b_dw_sg_out")
    dua, dvn, dws, dbs, parts_up1 = _spatial_gate_vjp(ua, vn, ws, bs, dgated, "b_sgate",
                                                      comm=[_Exchange("scatter", dw_up1)])
    (dpu, dpv), (dln_w, dln_b) = _rowwise_vjp(
        _seg_gelu_ln, [(p_sg, 0, sg_width), (p_sg, 1, sg_width)], [ln_w, ln_b], [dua, dvn], [BF16, BF16],
        tb_wide, "b_gelu_ln")
    dp_sg = jnp.concatenate([dpu, dpv], axis=1)
    da2, parts_sg_out = _matmul(dp_sg, w_sg_in_g, "nt", F32, "b_sg_in", b_blocked=True,
                                comm=[_scatter_rows(dw_sg_out)])
    dw_sg_in = _matmul(a2, dp_sg, "tn", BF16, "b_dw_sg_in", out_blocked=True)
    (dh1, dy2), (dnw03, dnw10) = _rowwise_vjp(
        _seg_residual, [h1, y2], [nrm(0, 3), nrm(1, 0)], [dh2, da2], [F32, BF16], tb_d, "b_res0b")
    du1, parts_sg_in = _matmul(dy2, _rows_of(w_down0_g), "nt", BF16, "b_down0", extra=[u1],
                               epilogue=_relu2_vjp_epilogue, comm=[_Exchange("scatter", dw_sg_in)])
    dw_down0 = _matmul(r1, dy2, "tn", BF16, "b_dw_down0")
    da1, parts_down0 = _matmul(du1, w_up0_g, "nt", F32, "b_up0", b_blocked=True, comm=[_scatter_rows(dw_down0)])
    dw_up0 = _matmul(a1, du1, "tn", BF16, "b_dw_up0", out_blocked=True)
    (dx_res, dy1), (dnw01, dnw02) = _rowwise_vjp(
        _seg_residual, [xs, y1], [nrm(0, 1), nrm(0, 2)], [dh1, da1], [F32, BF16], tb_d, "b_res0a")
    do_mix = _matmul(dy1, _rows_of(w_la_out_g), "nt", F32, "b_la_out")
    dw_la_out = _matmul(o_mix, dy1, "tn", BF16, "b_dw_la_out")

    d_ret, _, _, _ = _scan_bwd(_ret_make_chunk, ret_seqs, [cos2, sin2], ret_consts, ret_states, (do_mix, 1),
                               n_heads, 0, 0, BF16, "b_ret")
    d_gdn, (dbeta_r, dg_r), (donw,), (parts_up0, parts_la_out) = _scan_bwd(
        _gdn_make_chunk, gdn_seqs, [beta_r, g_r], [onw], gdn_states, (do_mix, 0), n_heads, 2, 1, F32, "b_gdn",
        comm=[_Exchange("scatter", dw_up0), _scatter_rows(dw_la_out)])
    from_rows = lambda a: jnp.transpose(a[:, :, 0, :], (0, 2, 1)).reshape(t, n_heads)
    dbg = jnp.pad(jnp.concatenate([from_rows(dbeta_r), from_rows(dg_r)], axis=1), ((0, 0), (0, LANES - 2 * n_heads)))
    (dgate_raw,), (da_log_v, ddt_bias_v) = _rowwise_vjp(
        seg_gates, [gate_raw], [a_log_v, dt_bias_v], [dbg], [BF16], tb_d, "b_gates")
    dqkv, dconv_w = _conv_silu_vjp(proj, conv_w, d_gdn, 3 * hk, "b_conv")
    dproj = jnp.concatenate([dqkv, d_gdn[:, 3 * hk:].astype(BF16), d_ret], axis=1)
    dw_main = _matmul(y0, dproj, "tn", BF16, "b_dw_proj")
    dw_gate = _matmul(y0, dgate_raw, "tn", BF16, "b_dw_gate")
    dw_in_full = jnp.concatenate([dw_main[:, :4 * hk], dw_gate[:, :2 * n_heads], dw_main[:, 4 * hk:]], axis=1)
    in_cols = dw_in_full.shape[1] // N_DEV
    dw_in_blocks = jnp.stack([dw_in_full[:, j * in_cols:(j + 1) * in_cols] for j in range(N_DEV)])
    dy0, parts_in = _matmul(dproj, w_main, "nt", F32, "b_proj", comm=[_Exchange("scatter", dw_in_blocks)])
    dy0_gate = _matmul(dgate_raw, w_gate, "nt", F32, "b_gate")
    (grad_x,), (dnw00,) = _rowwise_vjp(
        _seg_norm, [xs], [nrm(0, 0)], [dy0 + dy0_gate], [F32], tb_d, "b_norm0", adds=[dx_res])

    outs = {}
    for name, parts in (("la_w_in", parts_in), ("la_w_out", parts_la_out), ("sg_w_in", parts_sg_in),
                        ("sg_w_out", parts_sg_out)):
        res = _adamw(parts, weights[name][0], mom1[name][0], mom2[name][0], f"adamw_{name}")
        outs[name] = [r[None] for r in res]
    for name, parts_l in (("ffn_w_up", [parts_up0, parts_up1]), ("ffn_w_down", [parts_down0, parts_down1])):
        res_l = [_adamw(parts_l[l], weights[name][l], mom1[name][l], mom2[name][l], f"adamw_{name}{l}")
                 for l in range(2)]
        outs[name] = [jnp.stack([res_l[0][j], res_l[1][j]]) for j in range(4)]

    dnorm = jnp.stack([jnp.concatenate([dnw00, dnw01, dnw02, dnw03], axis=0),
                       jnp.concatenate([dnw10, dnw11, dnw12, dnw13], axis=0)])
    small_grads = [dnorm, dconv_w, dln_w[0], dln_b[0],
                   da_log_v[:, n_heads:2 * n_heads], ddt_bias_v[:, n_heads:2 * n_heads],
                   donw.reshape(1, HEAD_DIM), dws[None], dbs[None, :, :, 0]]
    small_parts = _all_gather(_pack(small_grads), "ag_small_grads")
    gs = _unpack(_sum_parts(small_parts, "sum_small_grads"), [g.shape for g in small_grads])
    sharded = lambda full, axis, size: lax.dynamic_slice_in_dim(full, me * size, size, axis)
    own = [sharded(gs[0], 2, dsh), sharded(gs[1], 1, csh), sharded(gs[2], 0, wsh)[None], sharded(gs[3], 0, wsh)[None]]
    own += gs[4:]
    replicated = ["la_a_log", "la_dt_bias", "la_out_norm_w", "sg_w_s", "sg_b_s"]
    small_names = ["norm_w", "la_conv_w", "sg_ln_w", "sg_ln_b"] + replicated
    pk = lambda src: _pack(small_local(src) + [src[n] for n in replicated])
    res = _adamw(_pack(own)[None], pk(weights), pk(mom1), pk(mom2), "adamw_small")
    for j, r in enumerate(res):
        vals = _unpack(r, [g.shape for g in own])
        vals[1] = jnp.transpose(vals[1])[None]
        for n, val in zip(small_names, vals):
            outs.setdefault(n, [None] * 4)[j] = val

    grad_x = grad_x[None]
    return (loss, grad_x, *[outs[n][0] for n in names], *[outs[n][1] for n in names],
            *[outs[n][2] for n in names], *[outs[n][3] for n in names])
```

```python
import functools
import math

import jax
import jax.numpy as jnp
import numpy as np
from jax import lax
from jax.experimental import pallas as pl
from jax.experimental.pallas import tpu as pltpu

F32, BF16 = jnp.float32, jnp.bfloat16
_pallas_call = pl.pallas_call

N_DEV = 8
LANES = 128
V7X_VMEM_BYTES = 64 * 1024 * 1024
VMEM_LIMIT = (V7X_VMEM_BYTES * 3) // 4
HEAD_DIM = 128
LA_CHUNK = 64
INV_BLOCK = 8
SG_CHUNK = 128
SG_STEP_CHUNKS = 8
SG_GROUPS = 8
CONV_WIDTH = 4
HALO = 8
ROPE_BASE = 10000.0
EPS = 1e-6
ADAM_LR, ADAM_B1, ADAM_B2, ADAM_EPS, ADAM_WD, ADAM_STEP = 0.001, 0.9, 0.999, 1e-08, 0.01, 10
MESH_ID = pl.DeviceIdType.MESH


def _params(*sem):
    return pltpu.CompilerParams(dimension_semantics=sem or None, vmem_limit_bytes=VMEM_LIMIT)


def _tile(dim, pref):
    if dim <= pref:
        return dim
    t = (pref // LANES) * LANES
    while dim % t:
        t -= LANES
    return t


def _row_tile(rows, pref):
    t = min(rows, pref)
    while rows % t:
        t -= 8
    return t


def _all_gather(shard, name):
    def body(x_ref, out_ref, send_sems, recv_sems, local_sem):
        x, y, c = lax.axis_index("x"), lax.axis_index("y"), lax.axis_index("c")
        me, sibling = (x, y, c), (x, y, 1 - c)
        chips = [(1 - x, y), (x, 1 - y), (1 - x, 1 - y)]

        def rows(px, py, pc):
            return out_ref.at[4 * px + 2 * py + pc]

        def copy(k, block, to, src=None):
            return pltpu.make_async_remote_copy(
                src_ref=rows(*block) if src is None else src, dst_ref=rows(*block),
                send_sem=send_sems.at[k], recv_sem=recv_sems.at[k], device_id=to, device_id_type=MESH_ID)

        mine = pltpu.make_async_copy(x_ref, rows(*me), local_sem)
        mine.start()
        first = [copy(0, me, sibling, src=x_ref)]
        first += [copy(1 + j, me, (*chip, c), src=x_ref) for j, chip in enumerate(chips)]
        for cp in first:
            cp.start()
        passed = [copy(4 + j, (*chip, c), sibling) for j, chip in enumerate(chips)]
        for j, chip in enumerate(chips):
            copy(1 + j, (*chip, c), me).wait_recv()
            passed[j].start()
        copy(0, sibling, me).wait_recv()
        for j, chip in enumerate(chips):
            copy(4 + j, (*chip, 1 - c), me).wait_recv()
        for cp in first + passed:
            cp.wait_send()
        mine.wait()

    return _pallas_call(
        body, name=name,
        out_shape=jax.ShapeDtypeStruct((N_DEV,) + shard.shape, shard.dtype),
        in_specs=[pl.BlockSpec(memory_space=pl.ANY)],
        out_specs=pl.BlockSpec(memory_space=pl.ANY),
        scratch_shapes=[pltpu.SemaphoreType.DMA((7,)), pltpu.SemaphoreType.DMA((7,)), pltpu.SemaphoreType.DMA],
    )(shard)


class _Exchange:
    def __init__(self, kind, src):
        self.kind, self.src = kind, src
        shape = (N_DEV,) + src.shape if kind == "gather" else src.shape
        self.dst = jax.ShapeDtypeStruct(shape, src.dtype)

    def copies(self, src_ref, dst_ref, send_sems, recv_sems, local_sem):
        x, y, c = lax.axis_index("x"), lax.axis_index("y"), lax.axis_index("c")
        me = 4 * x + 2 * y + c
        pick = (lambda p: src_ref) if self.kind == "gather" else (lambda p: src_ref.at[p])
        local = pltpu.make_async_copy(pick(me), dst_ref.at[me], local_sem)
        remote = []
        for k in range(1, N_DEV):
            px, py, pc = x ^ ((k >> 2) & 1), y ^ ((k >> 1) & 1), c ^ (k & 1)
            remote.append(pltpu.make_async_remote_copy(
                src_ref=pick(4 * px + 2 * py + pc), dst_ref=dst_ref.at[me],
                send_sem=send_sems.at[k - 1], recv_sem=recv_sems.at[k - 1],
                device_id=(px, py, pc), device_id_type=MESH_ID))
        return local, remote


_ANY = pl.BlockSpec(memory_space=pl.ANY)
_EXCHANGE_SEMS = [pltpu.SemaphoreType.DMA((N_DEV - 1,)), pltpu.SemaphoreType.DMA((N_DEV - 1,)), pltpu.SemaphoreType.DMA]


def _carry(exchanges, src_refs, dst_refs, sem_refs, first, last, compute):
    def each():
        for e, (ex, s, d) in enumerate(zip(exchanges, src_refs, dst_refs)):
            yield ex.copies(s, d, *sem_refs[3 * e:3 * e + 3])

    if exchanges:
        @pl.when(first)
        def _():
            for local, remote in each():
                local.start()
                for cp in remote:
                    cp.start()

    compute()

    if exchanges:
        @pl.when(last)
        def _():
            for local, remote in each():
                for cp in remote:
                    cp.wait_recv()
                for cp in remote:
                    cp.wait_send()
                local.wait()


def _matmul(a, b, mode, out_dtype, name, b_blocked=False, out_blocked=False, extra=(), epilogue=None,
            out_dtypes=None, comm=()):
    if mode == "nn":
        (m, k), n = a.shape, b.shape[-1] * (N_DEV if b_blocked else 1)
    elif mode == "nt":
        (m, k), n = a.shape, b.shape[-2]
    else:
        (k, m), n = a.shape, b.shape[1]
    tm, tn, tk = _tile(m, 1024), _tile(n, 1024), _tile(k, 2048)
    if (b_blocked and mode == "nn") or out_blocked:
        tn = n // N_DEV
    span = 1
    if b_blocked and mode == "nt":
        kb = k // N_DEV
        span = max(1, min(N_DEV, 2048 // kb))
        tk = span * kb
    nj, ni, nk = n // tn, m // tm, k // tk
    dims = {"nn": (((1,), (0,)), ((), ())), "nt": (((1,), (1,)), ((), ())), "tn": (((0,), (0,)), ((), ()))}[mode]
    out_dtypes = out_dtypes or [out_dtype]
    ne, nc, no = len(extra), len(comm), len(out_dtypes)

    def body(*refs):
        a_ref, b_refs = refs[0], refs[1:1 + span]
        refs = refs[span - 1:]
        extra_refs, src_refs = refs[2:2 + ne], refs[2 + ne:2 + ne + nc]
        o = 2 + ne + nc
        out_refs, dst_refs = refs[o:o + no], refs[o + no:o + no + nc]
        rest = refs[o + no + nc:]
        acc, sems = (rest[0], rest[1:]) if nk > 1 else (None, rest)
        j, i, kk = pl.program_id(0), pl.program_id(1), pl.program_id(2)

        def finish(res):
            outs = epilogue(res, *[r[...] for r in extra_refs]) if epilogue else (res,)
            for ref, val in zip(out_refs, outs):
                ref[...] = val.astype(ref.dtype)

        def compute():
            a_tile = a_ref[...].astype(BF16)
            kw = a_tile.shape[1] // span if mode == "nt" else None
            prod = None
            for s, b_ref in enumerate(b_refs):
                a_part = a_tile if span == 1 else a_tile[:, s * kw:(s + 1) * kw]
                part = lax.dot_general(a_part, b_ref[...].astype(BF16), dims, preferred_element_type=F32)
                prod = part if prod is None else prod + part
            if nk == 1:
                finish(prod)
                return

            @pl.when(kk == 0)
            def _():
                acc[...] = prod

            @pl.when(kk > 0)
            def _():
                acc[...] += prod

            @pl.when(kk == nk - 1)
            def _():
                finish(acc[...])

        _carry(comm, src_refs, dst_refs, sems, (j == 0) & (i == 0) & (kk == 0),
               (j == nj - 1) & (i == ni - 1) & (kk == nk - 1), compute)

    a_spec = (pl.BlockSpec((tk, tm), lambda j, i, kk: (kk, i)) if mode == "tn"
              else pl.BlockSpec((tm, tk), lambda j, i, kk: (i, kk)))
    if b_blocked and mode == "nn":
        b_specs = [pl.BlockSpec((None, tk, tn), lambda j, i, kk: (j, kk, 0))]
    elif b_blocked:
        b_specs = [pl.BlockSpec((None, tn, tk // span), lambda j, i, kk, s=s: (kk * span + s, j, 0)) for s in range(span)]
    else:
        b_specs = [pl.BlockSpec((tn, tk), lambda j, i, kk: (j, kk)) if mode == "nt"
                   else pl.BlockSpec((tk, tn), lambda j, i, kk: (kk, j))]
    tile_spec = pl.BlockSpec((tm, tn), lambda j, i, kk: (i, j))
    if out_blocked:
        o_spec = pl.BlockSpec((None, tm, tn), lambda j, i, kk: (j, i, 0))
        o_shape = (N_DEV, m, tn)
    else:
        o_spec, o_shape = tile_spec, (m, n)
    res = _pallas_call(
        body, name=name, grid=(nj, ni, nk),
        in_specs=[a_spec] + b_specs + [tile_spec] * ne + [_ANY] * nc,
        out_specs=[o_spec] * no + [_ANY] * nc,
        out_shape=[jax.ShapeDtypeStruct(o_shape, d) for d in out_dtypes] + [ex.dst for ex in comm],
        scratch_shapes=([pltpu.VMEM((tm, tn), F32)] if nk > 1 else []) + _EXCHANGE_SEMS * nc,
        compiler_params=_params("arbitrary", "arbitrary", "arbitrary"),
    )(a, *[b] * span, *extra, *[ex.src for ex in comm])
    return res[0] if len(res) == 1 else res


def _col_spec(tb, spec):
    if isinstance(spec, tuple):
        arr, blk, width = spec
        return arr, pl.BlockSpec((tb, width), lambda i, blk=blk: (i, blk))
    return spec, pl.BlockSpec((tb, spec.shape[1]), lambda i: (i, 0))


def _full_spec(p):
    return pl.BlockSpec(p.shape, lambda i, nd=p.ndim: (0,) * nd)


def _rowwise(fn, xs, params, outs, tb, name):
    arrs, specs = zip(*[_col_spec(tb, s) for s in xs])
    t = arrs[0].shape[0]
    nx, npar = len(xs), len(params)

    def body(*refs):
        res = fn(*[r[...] for r in refs[:nx + npar]])
        for o_ref, r in zip(refs[nx + npar:], res):
            o_ref[...] = r.astype(o_ref.dtype)

    return _pallas_call(
        body, name=name, grid=(t // tb,),
        in_specs=list(specs) + [_full_spec(p) for p in params],
        out_specs=[pl.BlockSpec((tb, w), lambda i: (i, 0)) for w, _ in outs],
        out_shape=[jax.ShapeDtypeStruct((t, w), d) for w, d in outs],
        compiler_params=_params("parallel"),
    )(*arrs, *params)


def _rowwise_vjp(fn, xs, params, cts, dx_dtypes, tb, name, adds=None, primal_out=None, primal_width=1,
                 merge_dx=False):
    arrs, specs = zip(*[_col_spec(tb, s) for s in xs])
    t = arrs[0].shape[0]
    ct_groups = [c if isinstance(c, tuple) else (c,) for c in cts]
    ct_flat = [a for grp in ct_groups for a in grp]
    nx, npar, nct = len(xs), len(params), len(ct_flat)
    adds = adds or [None] * nx
    add_ix = [i for i in range(nx) if adds[i] is not None]
    dx_ix = [i for i in range(nx) if dx_dtypes[i] is not None]
    widths = [s.block_shape[1] for s in specs]
    n_dx_out = 1 if merge_dx else len(dx_ix)

    def body(*refs):
        x_refs, p_refs = refs[:nx], refs[nx:nx + npar]
        ct_refs = list(refs[nx + npar:nx + npar + nct])
        add_refs = refs[nx + npar + nct:nx + npar + nct + len(add_ix)]
        o = nx + npar + nct + len(add_ix)
        dx_refs, dp_refs = refs[o:o + n_dx_out], refs[o + n_dx_out:o + n_dx_out + npar]
        prim, vjp = jax.vjp(fn, *[r[...] for r in x_refs], *[r[...] for r in p_refs])
        ct_vals = []
        for grp, p in zip(ct_groups, prim):
            val = ct_refs.pop(0)[...].astype(p.dtype)
            for _ in grp[1:]:
                val = val + ct_refs.pop(0)[...].astype(p.dtype)
            ct_vals.append(val)
        grads = vjp(tuple(ct_vals))
        col = 0
        for n, i in enumerate(dx_ix):
            g = grads[i].astype(F32)
            if adds[i] is not None:
                g = g + add_refs[add_ix.index(i)][...].astype(F32)
            if merge_dx:
                dx_refs[0][:, col:col + widths[i]] = g.astype(dx_refs[0].dtype)
                col += widths[i]
            else:
                dx_refs[n][...] = g.astype(dx_refs[n].dtype)

        @pl.when(pl.program_id(0) == 0)
        def _():
            for ref in dp_refs:
                ref[...] = jnp.zeros_like(ref)

        for ref, g in zip(dp_refs, grads[nx:]):
            ref[...] += g.astype(F32)
        if primal_out is not None:
            refs[-1][...] = prim[primal_out].astype(refs[-1].dtype)

    ct_specs = [pl.BlockSpec((tb, c.shape[1]), lambda i: (i, 0)) for c in ct_flat]
    add_specs = [pl.BlockSpec((tb, widths[i]), lambda i_: (i_, 0)) for i in add_ix]
    if merge_dx:
        total = sum(widths[i] for i in dx_ix)
        out_specs = [pl.BlockSpec((tb, total), lambda i_: (i_, 0))]
        out_shape = [jax.ShapeDtypeStruct((t, total), dx_dtypes[dx_ix[0]])]
    else:
        out_specs = [pl.BlockSpec((tb, widths[i]), lambda i_: (i_, 0)) for i in dx_ix]
        out_shape = [jax.ShapeDtypeStruct((t, widths[i]), dx_dtypes[i]) for i in dx_ix]
    out_specs += [_full_spec(p) for p in params]
    out_shape += [jax.ShapeDtypeStruct(p.shape, F32) for p in params]
    if primal_out is not None:
        out_specs.append(pl.BlockSpec((tb, primal_width), lambda i: (i, 0)))
        out_shape.append(jax.ShapeDtypeStruct((t, primal_width), F32))
    res = _pallas_call(
        body, name=name, grid=(t // tb,),
        in_specs=list(specs) + [_full_spec(p) for p in params] + ct_specs + add_specs,
        out_specs=out_specs, out_shape=out_shape,
        compiler_params=_params("arbitrary"),
    )(*arrs, *params, *ct_flat, *[adds[i] for i in add_ix])
    ndx = n_dx_out
    out = (list(res[:ndx]), list(res[ndx:ndx + npar]))
    return out + (res[-1],) if primal_out is not None else out


def _rms(x, w):
    xf = x.astype(F32)
    return xf * lax.rsqrt(jnp.mean(xf * xf, axis=-1, keepdims=True) + EPS) * w


def _seg_norm(x, w):
    return (_rms(x, w),)


def _seg_residual(h, y, w_post, w_pre):
    h2 = h.astype(F32) + _rms(y, w_post)
    return h2, _rms(h2, w_pre)


def _seg_loss(h, y, tgt, w_post):
    err = h.astype(F32) + _rms(y, w_post) - tgt
    return (0.5 * jnp.mean(err * err, axis=-1, keepdims=True),)


def _seg_gelu_ln(pu, pv, ln_w, ln_b):
    u = jax.nn.gelu(pu.astype(F32))
    v = jax.nn.gelu(pv.astype(F32))
    mu = jnp.mean(v, axis=-1, keepdims=True)
    vc = v - mu
    var = jnp.mean(vc * vc, axis=-1, keepdims=True)
    return u, vc * lax.rsqrt(var + EPS) * ln_w + ln_b


def _make_seg_gates(n_heads):
    def seg(raw, a_log, dt_bias):
        lane = lax.broadcasted_iota(jnp.int32, raw.shape, 1)
        beta = jax.nn.sigmoid(raw)
        g = -jnp.exp(a_log) * jax.nn.softplus(raw + dt_bias)
        return (jnp.where(lane < n_heads, beta, jnp.where(lane < 2 * n_heads, g, 0.0)),)
    return seg


def _gate_fn(u, v, ws, bs):
    rows, gd = u.shape
    nb = rows // SG_CHUNK
    r = lax.broadcasted_iota(jnp.int32, (SG_CHUNK, SG_CHUNK), 0)
    c = lax.broadcasted_iota(jnp.int32, (SG_CHUNK, SG_CHUNK), 1)
    w = jnp.broadcast_to(jnp.where(r >= c, ws[0], 0.0).astype(BF16), (nb, SG_CHUNK, SG_CHUNK))
    v3 = v.astype(BF16).reshape(nb, SG_CHUNK, gd)
    s = lax.dot_general(w, v3, (((2,), (1,)), ((0,), (0,))), preferred_element_type=F32) + bs[0]
    return u.astype(F32) * s.reshape(rows, gd)


def _sg_rows(t):
    rows = SG_CHUNK * SG_STEP_CHUNKS
    while t % rows:
        rows -= SG_CHUNK
    return rows


def _sg_specs(rows, gd):
    x_spec = pl.BlockSpec((rows, gd), lambda g, i: (i, g))
    ws_spec = pl.BlockSpec((1, SG_CHUNK, SG_CHUNK), lambda g, i: (g, 0, 0))
    bs_spec = pl.BlockSpec((1, SG_CHUNK, 1), lambda g, i: (g, 0, 0))
    return x_spec, ws_spec, bs_spec


def _spatial_gate(u, v, ws, bs, name):
    t, w = u.shape
    gd = w // SG_GROUPS
    rows = _sg_rows(t)
    x_spec, ws_spec, bs_spec = _sg_specs(rows, gd)

    def body(u_ref, v_ref, ws_ref, bs_ref, o_ref):
        o_ref[...] = _gate_fn(u_ref[...], v_ref[...], ws_ref[...], bs_ref[...]).astype(o_ref.dtype)

    return _pallas_call(
        body, name=name, grid=(SG_GROUPS, t // rows),
        in_specs=[x_spec, x_spec, ws_spec, bs_spec], out_specs=x_spec,
        out_shape=jax.ShapeDtypeStruct((t, w), BF16),
        compiler_params=_params("parallel", "parallel"),
    )(u, v, ws, bs)


def _spatial_gate_vjp(u, v, ws, bs, ct, name, comm=()):
    t, w = u.shape
    gd = w // SG_GROUPS
    rows = _sg_rows(t)
    nchunks = t // rows
    nx = len(comm)
    x_spec, ws_spec, bs_spec = _sg_specs(rows, gd)

    def body(*refs):
        u_ref, v_ref, ws_ref, bs_ref, ct_ref = refs[:5]
        src_refs = refs[5:5 + nx]
        du_ref, dv_ref, dws_ref, dbs_ref = refs[5 + nx:9 + nx]
        dst_refs, sems = refs[9 + nx:9 + 2 * nx], refs[9 + 2 * nx:]
        g, i = pl.program_id(0), pl.program_id(1)

        def compute():
            _, vjp = jax.vjp(_gate_fn, u_ref[...], v_ref[...].astype(F32), ws_ref[...], bs_ref[...])
            du, dv, dws, dbs = vjp(ct_ref[...])
            du_ref[...] = du
            dv_ref[...] = dv

            @pl.when(i == 0)
            def _():
                dws_ref[...] = jnp.zeros_like(dws_ref)
                dbs_ref[...] = jnp.zeros_like(dbs_ref)

            dws_ref[...] += dws
            dbs_ref[...] += dbs

        _carry(comm, src_refs, dst_refs, sems, (g == 0) & (i == 0), (g == SG_GROUPS - 1) & (i == nchunks - 1), compute)

    return _pallas_call(
        body, name=name, grid=(SG_GROUPS, nchunks),
        in_specs=[x_spec, x_spec, ws_spec, bs_spec, x_spec] + [_ANY] * nx,
        out_specs=[x_spec, x_spec, ws_spec, bs_spec] + [_ANY] * nx,
        out_shape=[jax.ShapeDtypeStruct((t, w), F32), jax.ShapeDtypeStruct((t, w), F32),
                   jax.ShapeDtypeStruct(ws.shape, F32), jax.ShapeDtypeStruct(bs.shape, F32)] + [ex.dst for ex in comm],
        scratch_shapes=_EXCHANGE_SEMS * nx,
        compiler_params=_params("arbitrary", "arbitrary"),
    )(u, v, ws, bs, ct, *[ex.src for ex in comm])


def _conv_silu(proj, w, width, name):
    t = proj.shape[0]
    tb = _row_tile(t, 128)
    hb = tb // HALO

    def body(prev_ref, x_ref, w_ref, o_ref, ext):
        keep = (pl.program_id(0) > 0).astype(F32)
        ext[0:HALO, :] = prev_ref[...].astype(F32) * keep
        ext[HALO:HALO + tb, :] = x_ref[...].astype(F32)
        acc = jnp.zeros((tb, width), F32)
        for j in range(CONV_WIDTH):
            o = HALO - (CONV_WIDTH - 1) + j
            acc = acc + w_ref[j:j + 1, :] * ext[o:o + tb, :]
        o_ref[...] = jax.nn.silu(acc).astype(o_ref.dtype)

    return _pallas_call(
        body, name=name, grid=(t // tb,),
        in_specs=[pl.BlockSpec((HALO, width), lambda i: (jnp.maximum(i * hb - 1, 0), 0)),
                  pl.BlockSpec((tb, width), lambda i: (i, 0)),
                  pl.BlockSpec((CONV_WIDTH, width), lambda i: (0, 0))],
        out_specs=pl.BlockSpec((tb, width), lambda i: (i, 0)),
        out_shape=jax.ShapeDtypeStruct((t, width), F32),
        scratch_shapes=[pltpu.VMEM((tb + HALO, width), F32)],
        compiler_params=_params("parallel"),
    )(proj, proj, w)


def _conv_silu_vjp(proj, w, dy, width, name):
    t = proj.shape[0]
    tb = _row_tile(t, 128)
    hb = tb // HALO
    nb = t // tb
    te = tb + HALO

    def body(prev_ref, x_ref, next_ref, w_ref, dy_ref, dyn_ref, dx_ref, dw_ref, ext, dpre):
        i = pl.program_id(0)
        first = (i > 0).astype(F32)
        last = (i < nb - 1).astype(F32)
        ext[0:HALO, :] = prev_ref[...].astype(F32) * first
        ext[HALO:HALO + tb, :] = x_ref[...].astype(F32)
        ext[HALO + tb:, :] = next_ref[...].astype(F32) * last
        pre = jnp.zeros((te, width), F32)
        for j in range(CONV_WIDTH):
            o = HALO - (CONV_WIDTH - 1) + j
            pre = pre + w_ref[j:j + 1, :] * ext[o:o + te, :]
        sig = jax.nn.sigmoid(pre)
        dsilu = sig * (1.0 + pre * (1.0 - sig))
        dpre[0:tb, :] = dy_ref[...].astype(F32) * dsilu[0:tb, :]
        dpre[tb:, :] = dyn_ref[...].astype(F32) * last * dsilu[tb:, :]
        dx = jnp.zeros((tb, width), F32)
        for j in range(CONV_WIDTH):
            o = CONV_WIDTH - 1 - j
            dx = dx + w_ref[j:j + 1, :] * dpre[o:o + tb, :]
        dx_ref[...] = dx.astype(dx_ref.dtype)

        @pl.when(i == 0)
        def _():
            dw_ref[...] = jnp.zeros_like(dw_ref)

        own = dpre[0:tb, :]
        for j in range(CONV_WIDTH):
            o = HALO - (CONV_WIDTH - 1) + j
            dw_ref[j:j + 1, :] += jnp.sum(own * ext[o:o + tb, :], axis=0, keepdims=True)

    halo_prev = lambda i: (jnp.maximum(i * hb - 1, 0), 0)
    halo_next = lambda i: (jnp.minimum((i + 1) * hb, t // HALO - 1), 0)
    return _pallas_call(
        body, name=name, grid=(nb,),
        in_specs=[pl.BlockSpec((HALO, width), halo_prev),
                  pl.BlockSpec((tb, width), lambda i: (i, 0)),
                  pl.BlockSpec((HALO, width), halo_next),
                  pl.BlockSpec((CONV_WIDTH, width), lambda i: (0, 0)),
                  pl.BlockSpec((tb, width), lambda i: (i, 0)),
                  pl.BlockSpec((HALO, width), halo_next)],
        out_specs=[pl.BlockSpec((tb, width), lambda i: (i, 0)),
                   pl.BlockSpec((CONV_WIDTH, width), lambda i: (0, 0))],
        out_shape=[jax.ShapeDtypeStruct((t, width), BF16), jax.ShapeDtypeStruct((CONV_WIDTH, width), F32)],
        scratch_shapes=[pltpu.VMEM((tb + 2 * HALO, width), F32), pltpu.VMEM((te, width), F32)],
        compiler_params=_params("arbitrary"),
    )(proj, proj, proj, w, dy, dy)


def _bdot(a, b, ca, cb):
    return lax.dot_general(a.astype(BF16), b.astype(BF16), (((ca,), (cb,)), ((0,), (0,))),
                           preferred_element_type=F32)


def _bdot3(a, b, ca, cb):
    a_hi, b_hi = a.astype(BF16), b.astype(BF16)
    a_lo, b_lo = (a - a_hi.astype(F32)).astype(BF16), (b - b_hi.astype(F32)).astype(BF16)
    dot = lambda p, q: lax.dot_general(p, q, (((ca,), (cb,)), ((0,), (0,))), preferred_element_type=F32)
    return dot(a_hi, b_hi) + (dot(a_hi, b_lo) + dot(a_lo, b_hi))


@jax.custom_vjp
def _bmm3(a, b):
    return _bdot3(a, b, 2, 1)


_bmm3.defvjp(lambda a, b: (_bmm3(a, b), (a, b)),
             lambda res, g: (_bdot3(g, res[1], 2, 2), _bdot3(res[0], g, 1, 1)))


def _bmm(a, b):
    return _bdot(a, b, 2, 1)


def _bmm_nt(a, b):
    return _bdot(a, b, 2, 2)


def _bmm_tn(a, b):
    return _bdot(a, b, 1, 1)


def _head_rms(o):
    return o * lax.rsqrt(jnp.mean(o * o, axis=-1, keepdims=True) + EPS)


def _l2norm(x):
    return x * lax.rsqrt(jnp.sum(x * x, axis=-1, keepdims=True) + 1e-6)


def _gdn_chunk(state, qc, kc, vc, z, beta_r, g_r, onw):
    cs = LA_CHUNK
    r = lax.broadcasted_iota(jnp.int32, (1, cs, cs), 1)
    c = lax.broadcasted_iota(jnp.int32, (1, cs, cs), 2)
    eye = (r == c).astype(F32)
    causal, strict = r >= c, r > c
    q = _l2norm(qc.astype(F32)) * (HEAD_DIM ** -0.5)
    k = _l2norm(kc.astype(F32))
    v = vc.astype(F32)
    g_col = jnp.sum(eye * g_r, axis=-1, keepdims=True)
    beta_col = jnp.sum(eye * beta_r, axis=-1, keepdims=True)
    gc_col = jnp.sum(causal.astype(F32) * g_r, axis=-1, keepdims=True)
    gc_row = jnp.sum((r <= c).astype(F32) * g_col, axis=-2, keepdims=True)
    gc_last = jnp.sum(g_r, axis=-1, keepdims=True)
    decay = jnp.where(causal, jnp.exp(jnp.where(causal, gc_col - gc_row, 0.0)), 0.0)
    kb = k * beta_col
    a = jnp.where(strict, _bmm_nt(kb, k) * decay, 0.0)
    same_block = (r // INV_BLOCK) == (c // INV_BLOCK)
    diag = jnp.where(same_block, a, 0.0)

    def nilpotent_inverse(x, order):
        inv, p = eye - x, x
        for _ in range(int(math.log2(order)) - 1):
            p = _bmm3(p, p)
            inv = inv + _bmm3(inv, p)
        return inv

    t_diag = nilpotent_inverse(diag, INV_BLOCK)
    t_inv = _bmm3(nilpotent_inverse(_bmm3(t_diag, a - diag), cs // INV_BLOCK), t_diag)
    eg = jnp.exp(gc_col)
    u = _bmm(t_inv, v * beta_col)
    w = _bmm(t_inv, kb * eg)
    qk = jnp.where(causal, _bmm_nt(q, k) * decay, 0.0)
    v_new = u - _bmm(w, state)
    o = _bmm(q * eg, state) + _bmm(qk, v_new)
    new_state = state * jnp.exp(gc_last) + _bmm_tn(k * jnp.exp(gc_last - gc_col), v_new)
    o = _head_rms(o) * onw * jax.nn.silu(z.astype(F32))
    return new_state, o


@jax.custom_vjp
def _swap_halves(x):
    return pltpu.roll(x, HEAD_DIM // 2, x.ndim - 1)


_swap_halves.defvjp(lambda x: (_swap_halves(x), None), lambda _, g: (_swap_halves(g),))


def _make_ret_chunk(cos2, sin2, d_mat, dec_q, dec_k, dec_c):
    def chunk(state, rq, rk, rv, rg):
        qf, kf = rq.astype(F32), rk.astype(F32)
        q = qf * cos2 + _swap_halves(qf) * sin2
        k = (kf * cos2 + _swap_halves(kf) * sin2) * (HEAD_DIM ** -0.5)
        v = rv.astype(F32)
        inner = _bmm(_bmm_nt(q, k) * d_mat, v)
        cross = _bmm(q * dec_q, state)
        new_state = state * dec_c + _bmm_tn(k * dec_k, v)
        return new_state, jax.nn.silu(rg.astype(F32)) * _head_rms(inner + cross)
    return chunk


def _split_heads(x, n_heads):
    return jnp.stack([x[:, h * HEAD_DIM:(h + 1) * HEAD_DIM] for h in range(n_heads)], axis=0)


def _store_heads(ref, col0, val):
    for h in range(val.shape[0]):
        ref[:, col0 + h * HEAD_DIM:col0 + (h + 1) * HEAD_DIM] = val[h].astype(ref.dtype)


def _scan_fwd(make_chunk, seqs, rows, consts, n_heads, name, comm=()):
    cs, hk = LA_CHUNK, n_heads * HEAD_DIM
    t = seqs[0][0].shape[0]
    n = t // cs
    ns, nr, nc, nx = len(seqs), len(rows), len(consts), len(comm)

    def body(*refs):
        seq_refs, row_refs, const_refs = refs[:ns], refs[ns:ns + nr], refs[ns + nr:ns + nr + nc]
        o = ns + nr + nc
        src_refs = refs[o:o + nx]
        o_ref, st_ref = refs[o + nx:o + nx + 2]
        dst_refs = refs[o + nx + 2:o + 2 * nx + 2]
        state, sems = refs[o + 2 * nx + 2], refs[o + 2 * nx + 3:]
        i = pl.program_id(0)

        def compute():
            @pl.when(i == 0)
            def _():
                state[...] = jnp.zeros_like(state)

            st_ref[0] = state[...]
            chunk = make_chunk([r[0] for r in row_refs], [r[...] for r in const_refs])
            new_state, out = chunk(state[...], *[_split_heads(r[...], n_heads) for r in seq_refs])
            state[...] = new_state
            _store_heads(o_ref, 0, out)

        _carry(comm, src_refs, dst_refs, sems, i == 0, i == n - 1, compute)

    return _pallas_call(
        body, name=name, grid=(n,),
        in_specs=[pl.BlockSpec((cs, hk), lambda i, b=b: (i, b)) for _, b in seqs]
        + [pl.BlockSpec((1,) + a.shape[1:], lambda i, nd=a.ndim: (i,) + (0,) * (nd - 1)) for a in rows]
        + [_full_spec(a) for a in consts] + [_ANY] * nx,
        out_specs=[pl.BlockSpec((cs, hk), lambda i: (i, 0)),
                   pl.BlockSpec((1, n_heads, HEAD_DIM, HEAD_DIM), lambda i: (i, 0, 0, 0))] + [_ANY] * nx,
        out_shape=[jax.ShapeDtypeStruct((t, hk), BF16),
                   jax.ShapeDtypeStruct((n, n_heads, HEAD_DIM, HEAD_DIM), F32)] + [ex.dst for ex in comm],
        scratch_shapes=[pltpu.VMEM((n_heads, HEAD_DIM, HEAD_DIM), F32)] + _EXCHANGE_SEMS * nx,
        compiler_params=_params("arbitrary"),
    )(*[a for a, _ in seqs], *rows, *consts, *[ex.src for ex in comm])


def _scan_bwd(make_chunk, seqs, rows, consts, states, d_out, n_heads, n_row_grads, n_const_grads, dseq_dtype,
              name, comm=()):
    cs, hk = LA_CHUNK, n_heads * HEAD_DIM
    t = seqs[0][0].shape[0]
    n = t // cs
    ns, nr, nc, nx = len(seqs), len(rows), len(consts), len(comm)
    n_grads = 1 + n_row_grads + n_const_grads

    def body(*refs):
        seq_refs, row_refs, const_refs = refs[:ns], refs[ns:ns + nr], refs[ns + nr:ns + nr + nc]
        st_ref, do_ref = refs[ns + nr + nc:ns + nr + nc + 2]
        o = ns + nr + nc + 2
        src_refs = refs[o:o + nx]
        o += nx
        dseq_ref = refs[o]
        drow_refs = refs[o + 1:o + 1 + n_row_grads]
        dconst_refs = refs[o + 1 + n_row_grads:o + n_grads]
        dst_refs = refs[o + n_grads:o + n_grads + nx]
        d_state, sems = refs[o + n_grads + nx], refs[o + n_grads + nx + 1:]
        i = pl.program_id(0)

        def compute():
            @pl.when(i == 0)
            def _():
                d_state[...] = jnp.zeros_like(d_state)
                for ref in dconst_refs:
                    ref[...] = jnp.zeros_like(ref)

            row_vals = [r[0] for r in row_refs]
            const_vals = [r[...] for r in const_refs]

            def fn(state, seq_vals, row_d, const_d):
                chunk = make_chunk(list(row_d) + row_vals[n_row_grads:], list(const_d) + const_vals[n_const_grads:])
                return chunk(state, *seq_vals)

            _, vjp = jax.vjp(fn, st_ref[0], tuple(_split_heads(r[...], n_heads) for r in seq_refs),
                             tuple(row_vals[:n_row_grads]), tuple(const_vals[:n_const_grads]))
            ds, dseq, drow, dconst = vjp((d_state[...], _split_heads(do_ref[...], n_heads).astype(F32)))
            d_state[...] = ds
            for j, g in enumerate(dseq):
                _store_heads(dseq_ref, j * hk, g)
            for ref, g in zip(drow_refs, drow):
                ref[0] = g
            for ref, g in zip(dconst_refs, dconst):
                ref[...] += g

        _carry(comm, src_refs, dst_refs, sems, i == 0, i == n - 1, compute)

    rev = lambda i: n - 1 - i
    row_spec = lambda a: pl.BlockSpec((1,) + a.shape[1:], lambda i, nd=a.ndim: (rev(i),) + (0,) * (nd - 1))
    res = _pallas_call(
        body, name=name, grid=(n,),
        in_specs=[pl.BlockSpec((cs, hk), lambda i, b=b: (rev(i), b)) for _, b in seqs]
        + [row_spec(a) for a in rows] + [_full_spec(a) for a in consts]
        + [pl.BlockSpec((1, n_heads, HEAD_DIM, HEAD_DIM), lambda i: (rev(i), 0, 0, 0)),
           pl.BlockSpec((cs, hk), lambda i, b=d_out[1]: (rev(i), b))] + [_ANY] * nx,
        out_specs=[pl.BlockSpec((cs, ns * hk), lambda i: (rev(i), 0))]
        + [row_spec(a) for a in rows[:n_row_grads]] + [_full_spec(a) for a in consts[:n_const_grads]] + [_ANY] * nx,
        out_shape=[jax.ShapeDtypeStruct((t, ns * hk), dseq_dtype)]
        + [jax.ShapeDtypeStruct(a.shape, F32) for a in rows[:n_row_grads]]
        + [jax.ShapeDtypeStruct(a.shape, F32) for a in consts[:n_const_grads]] + [ex.dst for ex in comm],
        scratch_shapes=[pltpu.VMEM((n_heads, HEAD_DIM, HEAD_DIM), F32)] + _EXCHANGE_SEMS * nx,
        compiler_params=_params("arbitrary"),
    )(*[a for a, _ in seqs], *rows, *consts, states, d_out[0], *[ex.src for ex in comm])
    return (res[0], list(res[1:1 + n_row_grads]), list(res[1 + n_row_grads:n_grads]), list(res[n_grads:]))


def _gdn_make_chunk(row_vals, const_vals):
    beta_r, g_r = row_vals
    (onw,) = const_vals
    return lambda state, qc, kc, vc, z: _gdn_chunk(state, qc, kc, vc, z, beta_r, g_r, onw)


def _ret_make_chunk(row_vals, const_vals):
    cos2, sin2 = row_vals
    d_mat, dec_q, dec_k, dec_c = const_vals
    return _make_ret_chunk(cos2, sin2, d_mat, dec_q, dec_k, dec_c)


def _adamw_math(w, g, m, v):
    m = ADAM_B1 * m + (1.0 - ADAM_B1) * g
    v = ADAM_B2 * v + (1.0 - ADAM_B2) * jnp.square(g)
    m_hat = m / (1.0 - ADAM_B1 ** ADAM_STEP)
    v_hat = v / (1.0 - ADAM_B2 ** ADAM_STEP)
    delta = -ADAM_LR * (m_hat / (jnp.sqrt(v_hat) + ADAM_EPS) + ADAM_WD * w)
    return delta, m, v


def _sum_parts(parts, name):
    n_parts, rows, cols = parts.shape
    tb = _row_tile(rows, 512)

    def body(p_ref, o_ref):
        g = p_ref[0].astype(F32)
        for s in range(1, n_parts):
            g = g + p_ref[s].astype(F32)
        o_ref[...] = g

    return _pallas_call(
        body, name=name, grid=(rows // tb,),
        in_specs=[pl.BlockSpec((n_parts, tb, cols), lambda i: (0, i, 0))],
        out_specs=pl.BlockSpec((tb, cols), lambda i: (i, 0)),
        out_shape=jax.ShapeDtypeStruct((rows, cols), F32),
        compiler_params=_params("parallel"),
    )(parts)


def _adamw(parts, w, m, v, name):
    rows, cols = w.shape
    n_parts = parts.shape[0]
    tb = _row_tile(rows, max(8, (1 << 17) // cols // 8 * 8))

    def body(p_ref, w_ref, m_ref, v_ref, g_out, d_out, m_out, v_out):
        g = p_ref[0].astype(F32)
        for s in range(1, n_parts):
            g = g + p_ref[s].astype(F32)
        delta, m_new, v_new = _adamw_math(w_ref[...], g, m_ref[...], v_ref[...])
        g_out[...] = g
        d_out[...] = delta
        m_out[...] = m_new
        v_out[...] = v_new

    spec = pl.BlockSpec((tb, cols), lambda i: (i, 0))
    return _pallas_call(
        body, name=name, grid=(rows // tb,),
        in_specs=[pl.BlockSpec((n_parts, tb, cols), lambda i: (0, i, 0)), spec, spec, spec],
        out_specs=[spec] * 4, out_shape=[jax.ShapeDtypeStruct((rows, cols), F32)] * 4,
        compiler_params=_params("parallel"),
    )(parts, w, m, v)


def _pack(arrays, rows_multiple=8):
    flat = jnp.concatenate([a.reshape(-1).astype(F32) for a in arrays])
    n = flat.shape[0]
    rows = -(-n // LANES)
    rows = -(-rows // rows_multiple) * rows_multiple
    return jnp.pad(flat, (0, rows * LANES - n)).reshape(rows, LANES)


def _unpack(packed, shapes):
    flat = packed.reshape(-1)
    out, o = [], 0
    for s in shapes:
        n = int(np.prod(s))
        out.append(flat[o:o + n].reshape(s))
        o += n
    return out


def _gather(shard):
    return _Exchange("gather", shard.astype(BF16))


def _rows_of(gathered):
    return gathered.reshape(gathered.shape[0] * gathered.shape[1], gathered.shape[2])


def _scatter_rows(full):
    k, n = full.shape
    return _Exchange("scatter", full.reshape(N_DEV, k // N_DEV, n))


def _relu2_epilogue(u):
    return u, jnp.square(jax.nn.relu(u))


def _relu2_vjp_epilogue(dr, u):
    return (dr * (2.0 * jax.nn.relu(u.astype(F32))),)


def kernel(x, norm_w, la_w_in, la_conv_w, la_a_log, la_dt_bias, la_out_norm_w, la_w_out, sg_w_in, sg_ln_w, sg_ln_b, sg_w_s, sg_b_s, sg_w_out, ffn_w_up, ffn_w_down, loss_target, m_norm_w, m_la_w_in, m_la_conv_w, m_la_a_log, m_la_dt_bias, m_la_out_norm_w, m_la_w_out, m_sg_w_in, m_sg_ln_w, m_sg_ln_b, m_sg_w_s, m_sg_b_s, m_sg_w_out, m_ffn_w_up, m_ffn_w_down, v_norm_w, v_la_w_in, v_la_conv_w, v_la_a_log, v_la_dt_bias, v_la_out_norm_w, v_la_w_out, v_sg_w_in, v_sg_ln_w, v_sg_ln_b, v_sg_w_s, v_sg_b_s, v_sg_w_out, v_ffn_w_up, v_ffn_w_down):
    weights = dict(norm_w=norm_w, la_w_in=la_w_in, la_conv_w=la_conv_w, la_a_log=la_a_log, la_dt_bias=la_dt_bias,
                   la_out_norm_w=la_out_norm_w, la_w_out=la_w_out, sg_w_in=sg_w_in, sg_ln_w=sg_ln_w, sg_ln_b=sg_ln_b,
                   sg_w_s=sg_w_s, sg_b_s=sg_b_s, sg_w_out=sg_w_out, ffn_w_up=ffn_w_up, ffn_w_down=ffn_w_down)
    mom1 = dict(norm_w=m_norm_w, la_w_in=m_la_w_in, la_conv_w=m_la_conv_w, la_a_log=m_la_a_log, la_dt_bias=m_la_dt_bias,
                la_out_norm_w=m_la_out_norm_w, la_w_out=m_la_w_out, sg_w_in=m_sg_w_in, sg_ln_w=m_sg_ln_w, sg_ln_b=m_sg_ln_b,
                sg_w_s=m_sg_w_s, sg_b_s=m_sg_b_s, sg_w_out=m_sg_w_out, ffn_w_up=m_ffn_w_up, ffn_w_down=m_ffn_w_down)
    mom2 = dict(norm_w=v_norm_w, la_w_in=v_la_w_in, la_conv_w=v_la_conv_w, la_a_log=v_la_a_log, la_dt_bias=v_la_dt_bias,
                la_out_norm_w=v_la_out_norm_w, la_w_out=v_la_w_out, sg_w_in=v_sg_w_in, sg_ln_w=v_sg_ln_w, sg_ln_b=v_sg_ln_b,
                sg_w_s=v_sg_w_s, sg_b_s=v_sg_b_s, sg_w_out=v_sg_w_out, ffn_w_up=v_ffn_w_up, ffn_w_down=v_ffn_w_down)
    names = list(weights)

    t, d = x.shape[1], x.shape[2]
    n_heads = la_a_log.shape[-1]
    hk = n_heads * HEAD_DIM
    cs = LA_CHUNK
    n_chunks = t // cs
    sg_width = sg_w_out.shape[1] * N_DEV
    me = 4 * lax.axis_index("x") + 2 * lax.axis_index("y") + lax.axis_index("c")
    xs = x[0]
    tgt = loss_target[0]

    w_in_g = _all_gather(la_w_in[0].astype(BF16), "ag_la_w_in")
    w_in_full = jnp.concatenate([w_in_g[j] for j in range(N_DEV)], axis=1)
    w_main = jnp.concatenate([w_in_full[:, :4 * hk], w_in_full[:, 4 * hk + 2 * n_heads:]], axis=1)
    w_gate = jnp.pad(w_in_full[:, 4 * hk:4 * hk + 2 * n_heads], ((0, 0), (0, LANES - 2 * n_heads)))
    conv_t = lambda a: jnp.transpose(a[0])
    small_local = lambda src: [src["norm_w"], conv_t(src["la_conv_w"]), src["sg_ln_w"], src["sg_ln_b"]]
    small_all = _all_gather(_pack(small_local(weights)), "ag_small")
    dsh = d // N_DEV
    csh = la_conv_w.shape[1]
    wsh = sg_width // N_DEV
    nw_parts, conv_parts, lnw_parts, lnb_parts = zip(*[
        _unpack(small_all[j], [(2, 4, dsh), (CONV_WIDTH, csh), (wsh,), (wsh,)]) for j in range(N_DEV)])
    nw = jnp.concatenate(nw_parts, axis=-1)
    conv_w = jnp.concatenate(conv_parts, axis=1)
    ln_w = jnp.concatenate(lnw_parts)[None, :]
    ln_b = jnp.concatenate(lnb_parts)[None, :]
    nrm = lambda l, j: nw[l, j][None, :]

    tb_d = _row_tile(t, 128)
    tb_wide = _row_tile(t, 64)

    (y0,) = _rowwise(_seg_norm, [xs], [nrm(0, 0)], [(d, BF16)], tb_d, "f_norm0")
    proj, w_la_out_g, w_up0_g = _matmul(y0, w_main, "nn", F32, "f_proj",
                                        comm=[_gather(la_w_out[0]), _gather(ffn_w_up[0])])
    gate_raw = _matmul(y0, w_gate, "nn", F32, "f_gate")
    a_log_v = jnp.pad(la_a_log.reshape(1, n_heads), ((0, 0), (n_heads, LANES - 2 * n_heads)))
    dt_bias_v = jnp.pad(la_dt_bias.reshape(1, n_heads), ((0, 0), (n_heads, LANES - 2 * n_heads)))
    seg_gates = _make_seg_gates(n_heads)
    (bg,) = _rowwise(seg_gates, [gate_raw], [a_log_v, dt_bias_v], [(LANES, F32)], tb_d, "f_gates")
    to_rows = lambda a: jnp.transpose(a.reshape(n_chunks, cs, n_heads), (0, 2, 1))[:, :, None, :]
    beta_r, g_r = to_rows(bg[:, :n_heads]), to_rows(bg[:, n_heads:2 * n_heads])
    qkvc = _conv_silu(proj, conv_w, 3 * hk, "f_conv")
    onw = la_out_norm_w.reshape(1, 1, HEAD_DIM)
    gdn_seqs = [(qkvc, 0), (qkvc, 1), (qkvc, 2), (proj, 3)]
    o_a, gdn_states, w_down0_g = _scan_fwd(_gdn_make_chunk, gdn_seqs, [beta_r, g_r], [onw], n_heads, "f_gdn",
                                           comm=[_gather(ffn_w_down[0])])

    pos = jnp.arange(t, dtype=F32)
    inv_freq = 1.0 / (ROPE_BASE ** jnp.linspace(0.0, 1.0, HEAD_DIM // 2, dtype=F32))
    ang = pos[:, None] * inv_freq[None, :]
    cos2 = jnp.concatenate([jnp.cos(ang), jnp.cos(ang)], axis=-1).reshape(n_chunks, cs, HEAD_DIM)
    sin2 = jnp.concatenate([-jnp.sin(ang), jnp.sin(ang)], axis=-1).reshape(n_chunks, cs, HEAD_DIM)
    log_gamma = jnp.log1p(-jnp.power(2.0, -5.0 - jnp.arange(n_heads, dtype=F32)))
    cpos = jnp.arange(cs, dtype=F32)
    rel = cpos[:, None] - cpos[None, :]
    d_mat = jnp.where(rel >= 0, jnp.exp(jnp.where(rel >= 0, rel, 0.0) * log_gamma[:, None, None]), 0.0)
    dec_q = jnp.exp((cpos + 1.0) * log_gamma[:, None])[..., None]
    dec_k = jnp.exp((cs - 1.0 - cpos) * log_gamma[:, None])[..., None]
    dec_c = jnp.exp(cs * log_gamma)[:, None, None]
    ret_seqs = [(proj, 4), (proj, 5), (proj, 6), (proj, 7)]
    ret_consts = [d_mat, dec_q, dec_k, dec_c]
    o_b, ret_states = _scan_fwd(_ret_make_chunk, ret_seqs, [cos2, sin2], ret_consts, n_heads, "f_ret")
    o_mix = jnp.concatenate([o_a, o_b], axis=1)
    y1 = _matmul(o_mix, _rows_of(w_la_out_g), "nn", F32, "f_la_out")
    h1, a1 = _rowwise(_seg_residual, [xs, y1], [nrm(0, 1), nrm(0, 2)], [(d, F32), (d, BF16)], tb_d, "f_res0a")
    u1, r1, w_sg_in_g = _matmul(a1, w_up0_g, "nn", None, "f_up0", b_blocked=True, epilogue=_relu2_epilogue,
                                out_dtypes=[F32, BF16], comm=[_gather(sg_w_in[0])])
    y2, w_sg_out_g = _matmul(r1, _rows_of(w_down0_g), "nn", F32, "f_down0", comm=[_gather(sg_w_out[0])])
    h2, a2 = _rowwise(_seg_residual, [h1, y2], [nrm(0, 3), nrm(1, 0)], [(d, F32), (d, BF16)], tb_d, "f_res0b")

    p_sg, w_up1_g = _matmul(a2, w_sg_in_g, "nn", F32, "f_sg_in", b_blocked=True,
                            comm=[_gather(ffn_w_up[1])])
    ua, vn = _rowwise(_seg_gelu_ln, [(p_sg, 0, sg_width), (p_sg, 1, sg_width)], [ln_w, ln_b],
                      [(sg_width, F32), (sg_width, BF16)], tb_wide, "f_gelu_ln")
    ws = sg_w_s[0]
    bs = sg_b_s[0][:, :, None]
    gated = _spatial_gate(ua, vn, ws, bs, "f_sgate")
    y3 = _matmul(gated, _rows_of(w_sg_out_g), "nn", F32, "f_sg_out")
    h3, a3 = _rowwise(_seg_residual, [h2, y3], [nrm(1, 1), nrm(1, 2)], [(d, F32), (d, BF16)], tb_d, "f_res1a")
    u2, r2, w_down1_g = _matmul(a3, w_up1_g, "nn", None, "f_up1", b_blocked=True, epilogue=_relu2_epilogue,
                                out_dtypes=[F32, BF16], comm=[_gather(ffn_w_down[1])])
    y4 = _matmul(r2, _rows_of(w_down1_g), "nn", F32, "f_down1")

    ones = jnp.ones((t, 1), F32)
    (dh3, dy4), (dnw13,), loss_rows = _rowwise_vjp(
        _seg_loss, [h3, y4, tgt], [nrm(1, 3)], [ones], [F32, BF16, None], tb_d, "b_loss", primal_out=0)
    loss = lax.psum(jnp.sum(loss_rows), ("x", "y", "c"))

    du2 = _matmul(dy4, _rows_of(w_down1_g), "nt", BF16, "b_down1", extra=[u2], epilogue=_relu2_vjp_epilogue)
    dw_down1 = _matmul(r2, dy4, "tn", BF16, "b_dw_down1")
    da3, parts_down1 = _matmul(du2, w_up1_g, "nt", F32, "b_up1", b_blocked=True, comm=[_scatter_rows(dw_down1)])
    dw_up1 = _matmul(a3, du2, "tn", BF16, "b_dw_up1", out_blocked=True)
    (dh2, dy3), (dnw11, dnw12) = _rowwise_vjp(
        _seg_residual, [h2, y3], [nrm(1, 1), nrm(1, 2)], [dh3, da3], [F32, BF16], tb_d, "b_res1a")
    dgated = _matmul(dy3, _rows_of(w_sg_out_g), "nt", F32, "b_sg_out")
    dw_sg_out = _matmul(gated, dy3, "tn", BF16, "b_dw_sg_out")
    dua, dvn, dws, dbs = _spatial_gate_vjp(ua, vn, ws, bs, dgated, "b_sgate")
    (dp_sg,), (dln_w, dln_b) = _rowwise_vjp(
        _seg_gelu_ln, [(p_sg, 0, sg_width), (p_sg, 1, sg_width)], [ln_w, ln_b], [dua, dvn], [BF16, BF16],
        tb_wide, "b_gelu_ln", merge_dx=True)
    da2, parts_sg_out = _matmul(dp_sg, w_sg_in_g, "nt", F32, "b_sg_in", b_blocked=True,
                                comm=[_scatter_rows(dw_sg_out)])
    dw_sg_in, parts_up1 = _matmul(a2, dp_sg, "tn", BF16, "b_dw_sg_in", out_blocked=True,
                                  comm=[_Exchange("scatter", dw_up1)])
    (dh1, dy2), (dnw03, dnw10) = _rowwise_vjp(
        _seg_residual, [h1, y2], [nrm(0, 3), nrm(1, 0)], [dh2, da2], [F32, BF16], tb_d, "b_res0b")
    du1, parts_sg_in = _matmul(dy2, _rows_of(w_down0_g), "nt", BF16, "b_down0", extra=[u1],
                               epilogue=_relu2_vjp_epilogue, comm=[_Exchange("scatter", dw_sg_in)])
    dw_down0 = _matmul(r1, dy2, "tn", BF16, "b_dw_down0")
    da1, parts_down0 = _matmul(du1, w_up0_g, "nt", F32, "b_up0", b_blocked=True, comm=[_scatter_rows(dw_down0)])
    dw_up0 = _matmul(a1, du1, "tn", BF16, "b_dw_up0", out_blocked=True)
    (dx_res, dy1), (dnw01, dnw02) = _rowwise_vjp(
        _seg_residual, [xs, y1], [nrm(0, 1), nrm(0, 2)], [dh1, da1], [F32, BF16], tb_d, "b_res0a")
    do_mix = _matmul(dy1, _rows_of(w_la_out_g), "nt", F32, "b_la_out")
    dw_la_out = _matmul(o_mix, dy1, "tn", BF16, "b_dw_la_out")

    d_ret, _, _, _ = _scan_bwd(_ret_make_chunk, ret_seqs, [cos2, sin2], ret_consts, ret_states, (do_mix, 1),
                               n_heads, 0, 0, BF16, "b_ret")
    d_gdn, (dbeta_r, dg_r), (donw,), (parts_up0, parts_la_out) = _scan_bwd(
        _gdn_make_chunk, gdn_seqs, [beta_r, g_r], [onw], gdn_states, (do_mix, 0), n_heads, 2, 1, F32, "b_gdn",
        comm=[_Exchange("scatter", dw_up0), _scatter_rows(dw_la_out)])
    from_rows = lambda a: jnp.transpose(a[:, :, 0, :], (0, 2, 1)).reshape(t, n_heads)
    dbg = jnp.pad(jnp.concatenate([from_rows(dbeta_r), from_rows(dg_r)], axis=1), ((0, 0), (0, LANES - 2 * n_heads)))
    (dgate_raw,), (da_log_v, ddt_bias_v) = _rowwise_vjp(
        seg_gates, [gate_raw], [a_log_v, dt_bias_v], [dbg], [BF16], tb_d, "b_gates")
    dqkv, dconv_w = _conv_silu_vjp(proj, conv_w, d_gdn, 3 * hk, "b_conv")
    dproj = jnp.concatenate([dqkv, d_gdn[:, 3 * hk:].astype(BF16), d_ret], axis=1)
    dw_main = _matmul(y0, dproj, "tn", BF16, "b_dw_proj")
    dw_gate = _matmul(y0, dgate_raw, "tn", BF16, "b_dw_gate")
    dw_in_full = jnp.concatenate([dw_main[:, :4 * hk], dw_gate[:, :2 * n_heads], dw_main[:, 4 * hk:]], axis=1)
    in_cols = dw_in_full.shape[1] // N_DEV
    dw_in_blocks = jnp.stack([dw_in_full[:, j * in_cols:(j + 1) * in_cols] for j in range(N_DEV)])
    dy0, parts_in = _matmul(dproj, w_main, "nt", F32, "b_proj", comm=[_Exchange("scatter", dw_in_blocks)])
    dy0_gate = _matmul(dgate_raw, w_gate, "nt", F32, "b_gate")
    (grad_x,), (dnw00,) = _rowwise_vjp(
        _seg_norm, [xs], [nrm(0, 0)], [(dy0, dy0_gate)], [F32], tb_d, "b_norm0", adds=[dx_res])

    outs = {}
    for name, parts in (("la_w_in", parts_in), ("la_w_out", parts_la_out), ("sg_w_in", parts_sg_in),
                        ("sg_w_out", parts_sg_out)):
        res = _adamw(parts, weights[name][0], mom1[name][0], mom2[name][0], f"adamw_{name}")
        outs[name] = [r[None] for r in res]
    for name, parts_l in (("ffn_w_up", [parts_up0, parts_up1]), ("ffn_w_down", [parts_down0, parts_down1])):
        res_l = [_adamw(parts_l[l], weights[name][l], mom1[name][l], mom2[name][l], f"adamw_{name}{l}")
                 for l in range(2)]
        outs[name] = [jnp.stack([res_l[0][j], res_l[1][j]]) for j in range(4)]

    dnorm = jnp.stack([jnp.concatenate([dnw00, dnw01, dnw02, dnw03], axis=0),
                       jnp.concatenate([dnw10, dnw11, dnw12, dnw13], axis=0)])
    small_grads = [dnorm, dconv_w, dln_w[0], dln_b[0],
                   da_log_v[:, n_heads:2 * n_heads], ddt_bias_v[:, n_heads:2 * n_heads],
                   donw.reshape(1, HEAD_DIM), dws[None], dbs[None, :, :, 0]]
    small_parts = _all_gather(_pack(small_grads), "ag_small_grads")
    gs = _unpack(_sum_parts(small_parts, "sum_small_grads"), [g.shape for g in small_grads])
    sharded = lambda full, axis, size: lax.dynamic_slice_in_dim(full, me * size, size, axis)
    own = [sharded(gs[0], 2, dsh), sharded(gs[1], 1, csh), sharded(gs[2], 0, wsh)[None], sharded(gs[3], 0, wsh)[None]]
    own += gs[4:]
    replicated = ["la_a_log", "la_dt_bias", "la_out_norm_w", "sg_w_s", "sg_b_s"]
    small_names = ["norm_w", "la_conv_w", "sg_ln_w", "sg_ln_b"] + replicated
    pk = lambda src: _pack(small_local(src) + [src[n] for n in replicated])
    res = _adamw(_pack(own)[None], pk(weights), pk(mom1), pk(mom2), "adamw_small")
    for j, r in enumerate(res):
        vals = _unpack(r, [g.shape for g in own])
        vals[1] = jnp.transpose(vals[1])[None]
        for n, val in zip(small_names, vals):
            outs.setdefault(n, [None] * 4)[j] = val

    grad_x = grad_x[None]
    return (loss, grad_x, *[outs[n][0] for n in names], *[outs[n][1] for n in names],
            *[outs[n][2] for n in names], *[outs[n][3] for n in names])
```

```python
import functools
import math

import jax
import jax.numpy as jnp
import numpy as np
from jax import lax
from jax.experimental import pallas as pl
from jax.experimental.pallas import tpu as pltpu

F32, BF16 = jnp.float32, jnp.bfloat16
_pallas_call = pl.pallas_call

N_DEV = 8
LANES = 128
V7X_VMEM_BYTES = 64 * 1024 * 1024
VMEM_LIMIT = (V7X_VMEM_BYTES * 3) // 4
HEAD_DIM = 128
LA_CHUNK = 64
INV_BLOCK = 8
SG_CHUNK = 128
SG_STEP_CHUNKS = 8
SG_GROUPS = 8
CONV_WIDTH = 4
HALO = 8
ROPE_BASE = 10000.0
EPS = 1e-6
ADAM_LR, ADAM_B1, ADAM_B2, ADAM_EPS, ADAM_WD, ADAM_STEP = 0.001, 0.9, 0.999, 1e-08, 0.01, 10
MESH_ID = pl.DeviceIdType.MESH


def _params(*sem):
    return pltpu.CompilerParams(dimension_semantics=sem or None, vmem_limit_bytes=VMEM_LIMIT)


def _tile(dim, pref):
    if dim <= pref:
        return dim
    t = (pref // LANES) * LANES
    while dim % t:
        t -= LANES
    return t


def _row_tile(rows, pref):
    t = min(rows, pref)
    while rows % t:
        t -= 8
    return t


def _all_gather(shard, name):
    def body(x_ref, out_ref, send_sems, recv_sems, local_sem):
        x, y, c = lax.axis_index("x"), lax.axis_index("y"), lax.axis_index("c")
        me, sibling = (x, y, c), (x, y, 1 - c)
        chips = [(1 - x, y), (x, 1 - y), (1 - x, 1 - y)]

        def rows(px, py, pc):
            return out_ref.at[4 * px + 2 * py + pc]

        def copy(k, block, to, src=None):
            return pltpu.make_async_remote_copy(
                src_ref=rows(*block) if src is None else src, dst_ref=rows(*block),
                send_sem=send_sems.at[k], recv_sem=recv_sems.at[k], device_id=to, device_id_type=MESH_ID)

        mine = pltpu.make_async_copy(x_ref, rows(*me), local_sem)
        mine.start()
        first = [copy(0, me, sibling, src=x_ref)]
        first += [copy(1 + j, me, (*chip, c), src=x_ref) for j, chip in enumerate(chips)]
        for cp in first:
            cp.start()
        passed = [copy(4 + j, (*chip, c), sibling) for j, chip in enumerate(chips)]
        for j, chip in enumerate(chips):
            copy(1 + j, (*chip, c), me).wait_recv()
            passed[j].start()
        copy(0, sibling, me).wait_recv()
        for j, chip in enumerate(chips):
            copy(4 + j, (*chip, 1 - c), me).wait_recv()
        for cp in first + passed:
            cp.wait_send()
        mine.wait()

    return _pallas_call(
        body, name=name,
        out_shape=jax.ShapeDtypeStruct((N_DEV,) + shard.shape, shard.dtype),
        in_specs=[pl.BlockSpec(memory_space=pl.ANY)],
        out_specs=pl.BlockSpec(memory_space=pl.ANY),
        scratch_shapes=[pltpu.SemaphoreType.DMA((7,)), pltpu.SemaphoreType.DMA((7,)), pltpu.SemaphoreType.DMA],
    )(shard)


class _Exchange:
    def __init__(self, kind, src):
        self.kind, self.src = kind, src
        shape = (N_DEV,) + src.shape if kind == "gather" else src.shape
        self.dst = jax.ShapeDtypeStruct(shape, src.dtype)

    def _scatter_copies(self, src_ref, dst_ref, send_sems, recv_sems, local_sem):
        x, y, c = lax.axis_index("x"), lax.axis_index("y"), lax.axis_index("c")
        me = 4 * x + 2 * y + c
        local = pltpu.make_async_copy(src_ref.at[me], dst_ref.at[me], local_sem)
        remote = []
        for k in range(1, N_DEV):
            px, py, pc = x ^ ((k >> 2) & 1), y ^ ((k >> 1) & 1), c ^ (k & 1)
            remote.append(pltpu.make_async_remote_copy(
                src_ref=src_ref.at[4 * px + 2 * py + pc], dst_ref=dst_ref.at[me],
                send_sem=send_sems.at[k - 1], recv_sem=recv_sems.at[k - 1],
                device_id=(px, py, pc), device_id_type=MESH_ID))
        return local, remote

    def _gather_copies(self, src_ref, dst_ref, send_sems, recv_sems, local_sem):
        x, y, c = lax.axis_index("x"), lax.axis_index("y"), lax.axis_index("c")
        me, sibling = (x, y, c), (x, y, 1 - c)
        chips = [(1 - x, y), (x, 1 - y), (1 - x, 1 - y)]
        rows = lambda px, py, pc: dst_ref.at[4 * px + 2 * py + pc]

        def copy(k, block, to, src=None):
            return pltpu.make_async_remote_copy(
                src_ref=rows(*block) if src is None else src, dst_ref=rows(*block),
                send_sem=send_sems.at[k], recv_sem=recv_sems.at[k], device_id=to, device_id_type=MESH_ID)

        local = pltpu.make_async_copy(src_ref, rows(*me), local_sem)
        own = [copy(0, me, sibling, src=src_ref)] + [copy(1 + j, me, (*chip, c), src=src_ref)
                                                     for j, chip in enumerate(chips)]
        passed = [copy(4 + j, (*chip, c), sibling) for j, chip in enumerate(chips)]
        landed = [copy(1 + j, (*chip, c), me) for j, chip in enumerate(chips)]
        from_sibling = [copy(0, sibling, me)] + [copy(4 + j, (*chip, 1 - c), me) for j, chip in enumerate(chips)]
        return local, own, landed, passed, from_sibling

    def start(self, *refs):
        if self.kind == "gather":
            local, own, _, _, _ = self._gather_copies(*refs)
        else:
            local, own = self._scatter_copies(*refs)
        local.start()
        for cp in own:
            cp.start()

    def middle(self, *refs):
        if self.kind == "gather":
            _, _, landed, passed, _ = self._gather_copies(*refs)
            for arrived, forward in zip(landed, passed):
                arrived.wait_recv()
                forward.start()

    def finish(self, *refs):
        if self.kind == "gather":
            local, own, _, passed, from_sibling = self._gather_copies(*refs)
            for cp in from_sibling:
                cp.wait_recv()
            for cp in own + passed:
                cp.wait_send()
        else:
            local, remote = self._scatter_copies(*refs)
            for cp in remote:
                cp.wait_recv()
            for cp in remote:
                cp.wait_send()
        local.wait()


_ANY = pl.BlockSpec(memory_space=pl.ANY)
_EXCHANGE_SEMS = [pltpu.SemaphoreType.DMA((N_DEV - 1,)), pltpu.SemaphoreType.DMA((N_DEV - 1,)), pltpu.SemaphoreType.DMA]


def _carry(exchanges, src_refs, dst_refs, sem_refs, first, middle, last, compute):
    def each():
        for e, (ex, s, d) in enumerate(zip(exchanges, src_refs, dst_refs)):
            yield ex, (s, d, *sem_refs[3 * e:3 * e + 3])

    if exchanges:
        @pl.when(first)
        def _():
            for ex, refs in each():
                ex.start(*refs)

    compute()

    if any(ex.kind == "gather" for ex in exchanges):
        @pl.when(middle)
        def _():
            for ex, refs in each():
                ex.middle(*refs)

    if exchanges:
        @pl.when(last)
        def _():
            for ex, refs in each():
                ex.finish(*refs)


def _matmul(a, b, mode, out_dtype, name, b_blocked=False, out_blocked=False, extra=(), epilogue=None,
            out_dtypes=None, comm=()):
    if mode == "nn":
        (m, k), n = a.shape, b.shape[-1] * (N_DEV if b_blocked else 1)
    elif mode == "nt":
        (m, k), n = a.shape, b.shape[-2]
    else:
        (k, m), n = a.shape, b.shape[1]
    tm, tn, tk = _tile(m, 1024), _tile(n, 1024), _tile(k, 2048)
    if (b_blocked and mode == "nn") or out_blocked:
        tn = n // N_DEV
    span = 1
    if b_blocked and mode == "nt":
        kb = k // N_DEV
        span = max(1, min(N_DEV, 2048 // kb))
        tk = span * kb
    nj, ni, nk = n // tn, m // tm, k // tk
    dims = {"nn": (((1,), (0,)), ((), ())), "nt": (((1,), (1,)), ((), ())), "tn": (((0,), (0,)), ((), ()))}[mode]
    out_dtypes = out_dtypes or [out_dtype]
    ne, nc, no = len(extra), len(comm), len(out_dtypes)

    def body(*refs):
        a_ref, b_refs = refs[0], refs[1:1 + span]
        refs = refs[span - 1:]
        extra_refs, src_refs = refs[2:2 + ne], refs[2 + ne:2 + ne + nc]
        o = 2 + ne + nc
        out_refs, dst_refs = refs[o:o + no], refs[o + no:o + no + nc]
        rest = refs[o + no + nc:]
        acc, sems = (rest[0], rest[1:]) if nk > 1 else (None, rest)
        j, i, kk = pl.program_id(0), pl.program_id(1), pl.program_id(2)

        def finish(res):
            outs = epilogue(res, *[r[...] for r in extra_refs]) if epilogue else (res,)
            for ref, val in zip(out_refs, outs):
                ref[...] = val.astype(ref.dtype)

        def compute():
            a_tile = a_ref[...].astype(BF16)
            kw = a_tile.shape[1] // span if mode == "nt" else None
            prod = None
            for s, b_ref in enumerate(b_refs):
                a_part = a_tile if span == 1 else a_tile[:, s * kw:(s + 1) * kw]
                part = lax.dot_general(a_part, b_ref[...].astype(BF16), dims, preferred_element_type=F32)
                prod = part if prod is None else prod + part
            if nk == 1:
                finish(prod)
                return

            @pl.when(kk == 0)
            def _():
                acc[...] = prod

            @pl.when(kk > 0)
            def _():
                acc[...] += prod

            @pl.when(kk == nk - 1)
            def _():
                finish(acc[...])

        step = (j * ni + i) * nk + kk
        _carry(comm, src_refs, dst_refs, sems, step == 0, step == (nj * ni * nk) // 2, step == nj * ni * nk - 1, compute)

    a_spec = (pl.BlockSpec((tk, tm), lambda j, i, kk: (kk, i)) if mode == "tn"
              else pl.BlockSpec((tm, tk), lambda j, i, kk: (i, kk)))
    if b_blocked and mode == "nn":
        b_specs = [pl.BlockSpec((None, tk, tn), lambda j, i, kk: (j, kk, 0))]
    elif b_blocked:
        b_specs = [pl.BlockSpec((None, tn, tk // span), lambda j, i, kk, s=s: (kk * span + s, j, 0)) for s in range(span)]
    else:
        b_specs = [pl.BlockSpec((tn, tk), lambda j, i, kk: (j, kk)) if mode == "nt"
                   else pl.BlockSpec((tk, tn), lambda j, i, kk: (kk, j))]
    tile_spec = pl.BlockSpec((tm, tn), lambda j, i, kk: (i, j))
    if out_blocked:
        o_spec = pl.BlockSpec((None, tm, tn), lambda j, i, kk: (j, i, 0))
        o_shape = (N_DEV, m, tn)
    else:
        o_spec, o_shape = tile_spec, (m, n)
    res = _pallas_call(
        body, name=name, grid=(nj, ni, nk),
        in_specs=[a_spec] + b_specs + [tile_spec] * ne + [_ANY] * nc,
        out_specs=[o_spec] * no + [_ANY] * nc,
        out_shape=[jax.ShapeDtypeStruct(o_shape, d) for d in out_dtypes] + [ex.dst for ex in comm],
        scratch_shapes=([pltpu.VMEM((tm, tn), F32)] if nk > 1 else []) + _EXCHANGE_SEMS * nc,
        compiler_params=_params("arbitrary", "arbitrary", "arbitrary"),
    )(a, *[b] * span, *extra, *[ex.src for ex in comm])
    return res[0] if len(res) == 1 else res


def _col_spec(tb, spec):
    if isinstance(spec, tuple):
        arr, blk, width = spec
        return arr, pl.BlockSpec((tb, width), lambda i, blk=blk: (i, blk))
    return spec, pl.BlockSpec((tb, spec.shape[1]), lambda i: (i, 0))


def _full_spec(p):
    return pl.BlockSpec(p.shape, lambda i, nd=p.ndim: (0,) * nd)


def _rowwise(fn, xs, params, outs, tb, name):
    arrs, specs = zip(*[_col_spec(tb, s) for s in xs])
    t = arrs[0].shape[0]
    nx, npar = len(xs), len(params)

    def body(*refs):
        res = fn(*[r[...] for r in refs[:nx + npar]])
        for o_ref, r in zip(refs[nx + npar:], res):
            o_ref[...] = r.astype(o_ref.dtype)

    return _pallas_call(
        body, name=name, grid=(t // tb,),
        in_specs=list(specs) + [_full_spec(p) for p in params],
        out_specs=[pl.BlockSpec((tb, w), lambda i: (i, 0)) for w, _ in outs],
        out_shape=[jax.ShapeDtypeStruct((t, w), d) for w, d in outs],
        compiler_params=_params("parallel"),
    )(*arrs, *params)


def _rowwise_vjp(fn, xs, params, cts, dx_dtypes, tb, name, adds=None, primal_out=None, primal_width=1,
                 merge_dx=False):
    arrs, specs = zip(*[_col_spec(tb, s) for s in xs])
    t = arrs[0].shape[0]
    ct_groups = [c if isinstance(c, tuple) else (c,) for c in cts]
    ct_flat = [a for grp in ct_groups for a in grp]
    nx, npar, nct = len(xs), len(params), len(ct_flat)
    adds = adds or [None] * nx
    add_ix = [i for i in range(nx) if adds[i] is not None]
    dx_ix = [i for i in range(nx) if dx_dtypes[i] is not None]
    widths = [s.block_shape[1] for s in specs]
    n_dx_out = 1 if merge_dx else len(dx_ix)

    def body(*refs):
        x_refs, p_refs = refs[:nx], refs[nx:nx + npar]
        ct_refs = list(refs[nx + npar:nx + npar + nct])
        add_refs = refs[nx + npar + nct:nx + npar + nct + len(add_ix)]
        o = nx + npar + nct + len(add_ix)
        dx_refs, dp_refs = refs[o:o + n_dx_out], refs[o + n_dx_out:o + n_dx_out + npar]
        prim, vjp = jax.vjp(fn, *[r[...] for r in x_refs], *[r[...] for r in p_refs])
        ct_vals = []
        for grp, p in zip(ct_groups, prim):
            val = ct_refs.pop(0)[...].astype(p.dtype)
            for _ in grp[1:]:
                val = val + ct_refs.pop(0)[...].astype(p.dtype)
            ct_vals.append(val)
        grads = vjp(tuple(ct_vals))
        col = 0
        for n, i in enumerate(dx_ix):
            g = grads[i].astype(F32)
            if adds[i] is not None:
                g = g + add_refs[add_ix.index(i)][...].astype(F32)
            if merge_dx:
                dx_refs[0][:, col:col + widths[i]] = g.astype(dx_refs[0].dtype)
                col += widths[i]
            else:
                dx_refs[n][...] = g.astype(dx_refs[n].dtype)

        @pl.when(pl.program_id(0) == 0)
        def _():
            for ref in dp_refs:
                ref[...] = jnp.zeros_like(ref)

        for ref, g in zip(dp_refs, grads[nx:]):
            ref[...] += g.astype(F32)
        if primal_out is not None:
            refs[-1][...] = prim[primal_out].astype(refs[-1].dtype)

    ct_specs = [pl.BlockSpec((tb, c.shape[1]), lambda i: (i, 0)) for c in ct_flat]
    add_specs = [pl.BlockSpec((tb, widths[i]), lambda i_: (i_, 0)) for i in add_ix]
    if merge_dx:
        total = sum(widths[i] for i in dx_ix)
        out_specs = [pl.BlockSpec((tb, total), lambda i_: (i_, 0))]
        out_shape = [jax.ShapeDtypeStruct((t, total), dx_dtypes[dx_ix[0]])]
    else:
        out_specs = [pl.BlockSpec((tb, widths[i]), lambda i_: (i_, 0)) for i in dx_ix]
        out_shape = [jax.ShapeDtypeStruct((t, widths[i]), dx_dtypes[i]) for i in dx_ix]
    out_specs += [_full_spec(p) for p in params]
    out_shape += [jax.ShapeDtypeStruct(p.shape, F32) for p in params]
    if primal_out is not None:
        out_specs.append(pl.BlockSpec((tb, primal_width), lambda i: (i, 0)))
        out_shape.append(jax.ShapeDtypeStruct((t, primal_width), F32))
    res = _pallas_call(
        body, name=name, grid=(t // tb,),
        in_specs=list(specs) + [_full_spec(p) for p in params] + ct_specs + add_specs,
        out_specs=out_specs, out_shape=out_shape,
        compiler_params=_params("arbitrary"),
    )(*arrs, *params, *ct_flat, *[adds[i] for i in add_ix])
    ndx = n_dx_out
    out = (list(res[:ndx]), list(res[ndx:ndx + npar]))
    return out + (res[-1],) if primal_out is not None else out


def _rms(x, w):
    xf = x.astype(F32)
    return xf * lax.rsqrt(jnp.mean(xf * xf, axis=-1, keepdims=True) + EPS) * w


def _seg_norm(x, w):
    return (_rms(x, w),)


def _seg_residual(h, y, w_post, w_pre):
    h2 = h.astype(F32) + _rms(y, w_post)
    return h2, _rms(h2, w_pre)


def _seg_loss(h, y, tgt, w_post):
    err = h.astype(F32) + _rms(y, w_post) - tgt
    return (0.5 * jnp.mean(err * err, axis=-1, keepdims=True),)


GELU_C, GELU_A = math.sqrt(2.0 / math.pi), 0.044715


@jax.custom_vjp
def _gelu(x):
    return jax.nn.gelu(x, approximate=True)


def _gelu_bwd(x, g):
    x2 = x * x
    th = jnp.tanh(GELU_C * x * (1.0 + GELU_A * x2))
    return (g * (0.5 * (1.0 + th) + 0.5 * x * (1.0 - th * th) * (GELU_C * (1.0 + 3.0 * GELU_A * x2))),)


_gelu.defvjp(lambda x: (_gelu(x), x), _gelu_bwd)


def _seg_gelu_ln(pu, pv, ln_w, ln_b):
    u = _gelu(pu.astype(F32))
    v = _gelu(pv.astype(F32))
    mu = jnp.mean(v, axis=-1, keepdims=True)
    vc = v - mu
    var = jnp.mean(vc * vc, axis=-1, keepdims=True)
    return u, vc * lax.rsqrt(var + EPS) * ln_w + ln_b


def _make_seg_gates(n_heads):
    def seg(raw, a_log, dt_bias):
        lane = lax.broadcasted_iota(jnp.int32, raw.shape, 1)
        beta = jax.nn.sigmoid(raw)
        g = -jnp.exp(a_log) * jax.nn.softplus(raw + dt_bias)
        return (jnp.where(lane < n_heads, beta, jnp.where(lane < 2 * n_heads, g, 0.0)),)
    return seg


def _gate_fn(u, v, ws, bs):
    rows, gd = u.shape
    nb = rows // SG_CHUNK
    r = lax.broadcasted_iota(jnp.int32, (SG_CHUNK, SG_CHUNK), 0)
    c = lax.broadcasted_iota(jnp.int32, (SG_CHUNK, SG_CHUNK), 1)
    w = jnp.broadcast_to(jnp.where(r >= c, ws[0], 0.0).astype(BF16), (nb, SG_CHUNK, SG_CHUNK))
    v3 = v.astype(BF16).reshape(nb, SG_CHUNK, gd)
    s = lax.dot_general(w, v3, (((2,), (1,)), ((0,), (0,))), preferred_element_type=F32) + bs[0]
    return u.astype(F32) * s.reshape(rows, gd)


def _sg_rows(t):
    rows = SG_CHUNK * SG_STEP_CHUNKS
    while t % rows:
        rows -= SG_CHUNK
    return rows


def _sg_specs(rows, gd):
    x_spec = pl.BlockSpec((rows, gd), lambda g, i: (i, g))
    ws_spec = pl.BlockSpec((1, SG_CHUNK, SG_CHUNK), lambda g, i: (g, 0, 0))
    bs_spec = pl.BlockSpec((1, SG_CHUNK, 1), lambda g, i: (g, 0, 0))
    return x_spec, ws_spec, bs_spec


def _spatial_gate(u, v, ws, bs, name):
    t, w = u.shape
    gd = w // SG_GROUPS
    rows = _sg_rows(t)
    x_spec, ws_spec, bs_spec = _sg_specs(rows, gd)

    def body(u_ref, v_ref, ws_ref, bs_ref, o_ref):
        o_ref[...] = _gate_fn(u_ref[...], v_ref[...], ws_ref[...], bs_ref[...]).astype(o_ref.dtype)

    return _pallas_call(
        body, name=name, grid=(SG_GROUPS, t // rows),
        in_specs=[x_spec, x_spec, ws_spec, bs_spec], out_specs=x_spec,
        out_shape=jax.ShapeDtypeStruct((t, w), BF16),
        compiler_params=_params("parallel", "parallel"),
    )(u, v, ws, bs)


def _spatial_gate_vjp(u, v, ws, bs, ct, name, comm=()):
    t, w = u.shape
    gd = w // SG_GROUPS
    rows = _sg_rows(t)
    nchunks = t // rows
    nx = len(comm)
    x_spec, ws_spec, bs_spec = _sg_specs(rows, gd)

    def body(*refs):
        u_ref, v_ref, ws_ref, bs_ref, ct_ref = refs[:5]
        src_refs = refs[5:5 + nx]
        du_ref, dv_ref, dws_ref, dbs_ref = refs[5 + nx:9 + nx]
        dst_refs, sems = refs[9 + nx:9 + 2 * nx], refs[9 + 2 * nx:]
        g, i = pl.program_id(0), pl.program_id(1)

        def compute():
            _, vjp = jax.vjp(_gate_fn, u_ref[...], v_ref[...].astype(F32), ws_ref[...], bs_ref[...])
            du, dv, dws, dbs = vjp(ct_ref[...])
            du_ref[...] = du
            dv_ref[...] = dv

            @pl.when(i == 0)
            def _():
                dws_ref[...] = jnp.zeros_like(dws_ref)
                dbs_ref[...] = jnp.zeros_like(dbs_ref)

            dws_ref[...] += dws
            dbs_ref[...] += dbs

        step = g * nchunks + i
        _carry(comm, src_refs, dst_refs, sems, step == 0, step == (SG_GROUPS * nchunks) // 2,
               step == SG_GROUPS * nchunks - 1, compute)

    return _pallas_call(
        body, name=name, grid=(SG_GROUPS, nchunks),
        in_specs=[x_spec, x_spec, ws_spec, bs_spec, x_spec] + [_ANY] * nx,
        out_specs=[x_spec, x_spec, ws_spec, bs_spec] + [_ANY] * nx,
        out_shape=[jax.ShapeDtypeStruct((t, w), F32), jax.ShapeDtypeStruct((t, w), F32),
                   jax.ShapeDtypeStruct(ws.shape, F32), jax.ShapeDtypeStruct(bs.shape, F32)] + [ex.dst for ex in comm],
        scratch_shapes=_EXCHANGE_SEMS * nx,
        compiler_params=_params("arbitrary", "arbitrary"),
    )(u, v, ws, bs, ct, *[ex.src for ex in comm])


def _conv_silu(proj, w, width, name):
    t = proj.shape[0]
    tb = _row_tile(t, 128)
    hb = tb // HALO

    def body(prev_ref, x_ref, w_ref, o_ref, ext):
        keep = (pl.program_id(0) > 0).astype(F32)
        ext[0:HALO, :] = prev_ref[...].astype(F32) * keep
        ext[HALO:HALO + tb, :] = x_ref[...].astype(F32)
        acc = jnp.zeros((tb, width), F32)
        for j in range(CONV_WIDTH):
            o = HALO - (CONV_WIDTH - 1) + j
            acc = acc + w_ref[j:j + 1, :] * ext[o:o + tb, :]
        o_ref[...] = jax.nn.silu(acc).astype(o_ref.dtype)

    return _pallas_call(
        body, name=name, grid=(t // tb,),
        in_specs=[pl.BlockSpec((HALO, width), lambda i: (jnp.maximum(i * hb - 1, 0), 0)),
                  pl.BlockSpec((tb, width), lambda i: (i, 0)),
                  pl.BlockSpec((CONV_WIDTH, width), lambda i: (0, 0))],
        out_specs=pl.BlockSpec((tb, width), lambda i: (i, 0)),
        out_shape=jax.ShapeDtypeStruct((t, width), F32),
        scratch_shapes=[pltpu.VMEM((tb + HALO, width), F32)],
        compiler_params=_params("parallel"),
    )(proj, proj, w)


def _conv_silu_vjp(proj, w, dy, width, name):
    t = proj.shape[0]
    tb = _row_tile(t, 128)
    hb = tb // HALO
    nb = t // tb
    te = tb + HALO

    def body(prev_ref, x_ref, next_ref, w_ref, dy_ref, dyn_ref, dx_ref, dw_ref, ext, dpre):
        i = pl.program_id(0)
        first = (i > 0).astype(F32)
        last = (i < nb - 1).astype(F32)
        ext[0:HALO, :] = prev_ref[...].astype(F32) * first
        ext[HALO:HALO + tb, :] = x_ref[...].astype(F32)
        ext[HALO + tb:, :] = next_ref[...].astype(F32) * last
        pre = jnp.zeros((te, width), F32)
        for j in range(CONV_WIDTH):
            o = HALO - (CONV_WIDTH - 1) + j
            pre = pre + w_ref[j:j + 1, :] * ext[o:o + te, :]
        sig = jax.nn.sigmoid(pre)
        dsilu = sig * (1.0 + pre * (1.0 - sig))
        dpre[0:tb, :] = dy_ref[...].astype(F32) * dsilu[0:tb, :]
        dpre[tb:, :] = dyn_ref[...].astype(F32) * last * dsilu[tb:, :]
        dx = jnp.zeros((tb, width), F32)
        for j in range(CONV_WIDTH):
            o = CONV_WIDTH - 1 - j
            dx = dx + w_ref[j:j + 1, :] * dpre[o:o + tb, :]
        dx_ref[...] = dx.astype(dx_ref.dtype)

        @pl.when(i == 0)
        def _():
            dw_ref[...] = jnp.zeros_like(dw_ref)

        own = dpre[0:tb, :]
        for j in range(CONV_WIDTH):
            o = HALO - (CONV_WIDTH - 1) + j
            dw_ref[j:j + 1, :] += jnp.sum(own * ext[o:o + tb, :], axis=0, keepdims=True)

    halo_prev = lambda i: (jnp.maximum(i * hb - 1, 0), 0)
    halo_next = lambda i: (jnp.minimum((i + 1) * hb, t // HALO - 1), 0)
    return _pallas_call(
        body, name=name, grid=(nb,),
        in_specs=[pl.BlockSpec((HALO, width), halo_prev),
                  pl.BlockSpec((tb, width), lambda i: (i, 0)),
                  pl.BlockSpec((HALO, width), halo_next),
                  pl.BlockSpec((CONV_WIDTH, width), lambda i: (0, 0)),
                  pl.BlockSpec((tb, width), lambda i: (i, 0)),
                  pl.BlockSpec((HALO, width), halo_next)],
        out_specs=[pl.BlockSpec((tb, width), lambda i: (i, 0)),
                   pl.BlockSpec((CONV_WIDTH, width), lambda i: (0, 0))],
        out_shape=[jax.ShapeDtypeStruct((t, width), BF16), jax.ShapeDtypeStruct((CONV_WIDTH, width), F32)],
        scratch_shapes=[pltpu.VMEM((tb + 2 * HALO, width), F32), pltpu.VMEM((te, width), F32)],
        compiler_params=_params("arbitrary"),
    )(proj, proj, proj, w, dy, dy)


def _bdot(a, b, ca, cb):
    return lax.dot_general(a.astype(BF16), b.astype(BF16), (((ca,), (cb,)), ((0,), (0,))),
                           preferred_element_type=F32)


def _bdot3(a, b, ca, cb):
    a_hi, b_hi = a.astype(BF16), b.astype(BF16)
    a_lo, b_lo = (a - a_hi.astype(F32)).astype(BF16), (b - b_hi.astype(F32)).astype(BF16)
    dot = lambda p, q: lax.dot_general(p, q, (((ca,), (cb,)), ((0,), (0,))), preferred_element_type=F32)
    return dot(a_hi, b_hi) + (dot(a_hi, b_lo) + dot(a_lo, b_hi))


def _unit_lower_inverse(a):
    cs = a.shape[-1]
    r = lax.broadcasted_iota(jnp.int32, (1, cs, cs), 1)
    c = lax.broadcasted_iota(jnp.int32, (1, cs, cs), 2)
    eye = (r == c).astype(F32)
    diag = jnp.where((r // INV_BLOCK) == (c // INV_BLOCK), a, 0.0)

    def nilpotent_inverse(x, order):
        inv, p = eye - x, x
        for _ in range(int(math.log2(order)) - 1):
            p = _bdot3(p, p, 2, 1)
            inv = inv + _bdot3(inv, p, 2, 1)
        return inv

    t_diag = nilpotent_inverse(diag, INV_BLOCK)
    return _bdot3(nilpotent_inverse(_bdot3(t_diag, a - diag, 2, 1), cs // INV_BLOCK), t_diag, 2, 1)


@jax.custom_vjp
def _inverse_given(a, t):
    return t


_inverse_given.defvjp(lambda a, t: (t, t),
                      lambda t, g: (-_bdot3(_bdot3(t, g, 1, 1), t, 2, 2), jnp.zeros_like(t)))


def _bmm(a, b):
    return _bdot(a, b, 2, 1)


def _bmm_nt(a, b):
    return _bdot(a, b, 2, 2)


def _bmm_tn(a, b):
    return _bdot(a, b, 1, 1)


def _head_rms(o):
    return o * lax.rsqrt(jnp.mean(o * o, axis=-1, keepdims=True) + EPS)


def _l2norm(x):
    return x * lax.rsqrt(jnp.sum(x * x, axis=-1, keepdims=True) + 1e-6)


def _gdn_chunk(state, qc, kc, vc, z, beta_r, g_r, onw, t_saved=None):
    cs = LA_CHUNK
    r = lax.broadcasted_iota(jnp.int32, (1, cs, cs), 1)
    c = lax.broadcasted_iota(jnp.int32, (1, cs, cs), 2)
    eye = (r == c).astype(F32)
    causal, strict = r >= c, r > c
    q = _l2norm(qc.astype(F32)) * (HEAD_DIM ** -0.5)
    k = _l2norm(kc.astype(F32))
    v = vc.astype(F32)
    g_col = jnp.sum(eye * g_r, axis=-1, keepdims=True)
    beta_col = jnp.sum(eye * beta_r, axis=-1, keepdims=True)
    gc_col = jnp.sum(causal.astype(F32) * g_r, axis=-1, keepdims=True)
    gc_row = jnp.sum((r <= c).astype(F32) * g_col, axis=-2, keepdims=True)
    gc_last = jnp.sum(g_r, axis=-1, keepdims=True)
    decay = jnp.where(causal, jnp.exp(jnp.where(causal, gc_col - gc_row, 0.0)), 0.0)
    kb = k * beta_col
    a = jnp.where(strict, _bmm_nt(kb, k) * decay, 0.0)
    t_inv = _unit_lower_inverse(a) if t_saved is None else _inverse_given(a, t_saved)
    eg = jnp.exp(gc_col)
    u = _bmm(t_inv, v * beta_col)
    w = _bmm(t_inv, kb * eg)
    qk = jnp.where(causal, _bmm_nt(q, k) * decay, 0.0)
    v_new = u - _bmm(w, state)
    o = _bmm(q * eg, state) + _bmm(qk, v_new)
    new_state = state * jnp.exp(gc_last) + _bmm_tn(k * jnp.exp(gc_last - gc_col), v_new)
    o = _head_rms(o) * onw * jax.nn.silu(z.astype(F32))
    return (new_state, o, t_inv) if t_saved is None else (new_state, o)


@jax.custom_vjp
def _swap_halves(x):
    return pltpu.roll(x, HEAD_DIM // 2, x.ndim - 1)


_swap_halves.defvjp(lambda x: (_swap_halves(x), None), lambda _, g: (_swap_halves(g),))


def _make_ret_chunk(cos2, sin2, d_mat, dec_q, dec_k, dec_c):
    def chunk(state, rq, rk, rv, rg):
        qf, kf = rq.astype(F32), rk.astype(F32)
        q = qf * cos2 + _swap_halves(qf) * sin2
        k = (kf * cos2 + _swap_halves(kf) * sin2) * (HEAD_DIM ** -0.5)
        v = rv.astype(F32)
        inner = _bmm(_bmm_nt(q, k) * d_mat, v)
        cross = _bmm(q * dec_q, state)
        new_state = state * dec_c + _bmm_tn(k * dec_k, v)
        return new_state, jax.nn.silu(rg.astype(F32)) * _head_rms(inner + cross)
    return chunk


def _split_heads(x, n_heads):
    return jnp.stack([x[:, h * HEAD_DIM:(h + 1) * HEAD_DIM] for h in range(n_heads)], axis=0)


def _store_heads(ref, col0, val):
    for h in range(val.shape[0]):
        ref[:, col0 + h * HEAD_DIM:col0 + (h + 1) * HEAD_DIM] = val[h].astype(ref.dtype)


def _scan_fwd(make_chunk, seqs, rows, consts, n_heads, name, comm=(), saved_shape=None):
    cs, hk = LA_CHUNK, n_heads * HEAD_DIM
    t = seqs[0][0].shape[0]
    n = t // cs
    ns, nr, nc, nx = len(seqs), len(rows), len(consts), len(comm)
    n_out = 2 if saved_shape is None else 3

    def body(*refs):
        seq_refs, row_refs, const_refs = refs[:ns], refs[ns:ns + nr], refs[ns + nr:ns + nr + nc]
        o = ns + nr + nc
        src_refs = refs[o:o + nx]
        out_refs = refs[o + nx:o + nx + n_out]
        dst_refs = refs[o + nx + n_out:o + 2 * nx + n_out]
        state, sems = refs[o + 2 * nx + n_out], refs[o + 2 * nx + n_out + 1:]
        i = pl.program_id(0)

        def compute():
            @pl.when(i == 0)
            def _():
                state[...] = jnp.zeros_like(state)

            out_refs[1][0] = state[...]
            chunk = make_chunk([r[0] for r in row_refs], [r[...] for r in const_refs])
            new_state, out, *saved = chunk(state[...], *[_split_heads(r[...], n_heads) for r in seq_refs])
            state[...] = new_state
            _store_heads(out_refs[0], 0, out)
            if saved_shape is not None:
                out_refs[2][0] = saved[0]

        _carry(comm, src_refs, dst_refs, sems, i == 0, i == n // 2, i == n - 1, compute)

    chunk_spec = lambda shape: pl.BlockSpec((1,) + tuple(shape), lambda i, nd=len(shape): (i,) + (0,) * nd)
    saved = [] if saved_shape is None else [tuple(saved_shape)]
    return _pallas_call(
        body, name=name, grid=(n,),
        in_specs=[pl.BlockSpec((cs, hk), lambda i, b=b: (i, b)) for _, b in seqs]
        + [chunk_spec(a.shape[1:]) for a in rows] + [_full_spec(a) for a in consts] + [_ANY] * nx,
        out_specs=[pl.BlockSpec((cs, hk), lambda i: (i, 0)), chunk_spec((n_heads, HEAD_DIM, HEAD_DIM))]
        + [chunk_spec(s) for s in saved] + [_ANY] * nx,
        out_shape=[jax.ShapeDtypeStruct((t, hk), BF16),
                   jax.ShapeDtypeStruct((n, n_heads, HEAD_DIM, HEAD_DIM), F32)]
        + [jax.ShapeDtypeStruct((n,) + s, F32) for s in saved] + [ex.dst for ex in comm],
        scratch_shapes=[pltpu.VMEM((n_heads, HEAD_DIM, HEAD_DIM), F32)] + _EXCHANGE_SEMS * nx,
        compiler_params=_params("arbitrary"),
    )(*[a for a, _ in seqs], *rows, *consts, *[ex.src for ex in comm])


def _scan_bwd(make_chunk, seqs, rows, consts, states, d_out, n_heads, n_row_grads, n_const_grads, dseq_dtype,
              name, comm=()):
    cs, hk = LA_CHUNK, n_heads * HEAD_DIM
    t = seqs[0][0].shape[0]
    n = t // cs
    ns, nr, nc, nx = len(seqs), len(rows), len(consts), len(comm)
    n_grads = 1 + n_row_grads + n_const_grads

    def body(*refs):
        seq_refs, row_refs, const_refs = refs[:ns], refs[ns:ns + nr], refs[ns + nr:ns + nr + nc]
        st_ref, do_ref = refs[ns + nr + nc:ns + nr + nc + 2]
        o = ns + nr + nc + 2
        src_refs = refs[o:o + nx]
        o += nx
        dseq_ref = refs[o]
        drow_refs = refs[o + 1:o + 1 + n_row_grads]
        dconst_refs = refs[o + 1 + n_row_grads:o + n_grads]
        dst_refs = refs[o + n_grads:o + n_grads + nx]
        d_state, sems = refs[o + n_grads + nx], refs[o + n_grads + nx + 1:]
        i = pl.program_id(0)

        def compute():
            @pl.when(i == 0)
            def _():
                d_state[...] = jnp.zeros_like(d_state)
                for ref in dconst_refs:
                    ref[...] = jnp.zeros_like(ref)

            row_vals = [r[0] for r in row_refs]
            const_vals = [r[...] for r in const_refs]

            def fn(state, seq_vals, row_d, const_d):
                chunk = make_chunk(list(row_d) + row_vals[n_row_grads:], list(const_d) + const_vals[n_const_grads:])
                return chunk(state, *seq_vals)

            _, vjp = jax.vjp(fn, st_ref[0], tuple(_split_heads(r[...], n_heads) for r in seq_refs),
                             tuple(row_vals[:n_row_grads]), tuple(const_vals[:n_const_grads]))
            ds, dseq, drow, dconst = vjp((d_state[...], _split_heads(do_ref[...], n_heads).astype(F32)))
            d_state[...] = ds
            for j, g in enumerate(dseq):
                _store_heads(dseq_ref, j * hk, g)
            for ref, g in zip(drow_refs, drow):
                ref[0] = g
            for ref, g in zip(dconst_refs, dconst):
                ref[...] += g

        _carry(comm, src_refs, dst_refs, sems, i == 0, i == n // 2, i == n - 1, compute)

    rev = lambda i: n - 1 - i
    row_spec = lambda a: pl.BlockSpec((1,) + a.shape[1:], lambda i, nd=a.ndim: (rev(i),) + (0,) * (nd - 1))
    res = _pallas_call(
        body, name=name, grid=(n,),
        in_specs=[pl.BlockSpec((cs, hk), lambda i, b=b: (rev(i), b)) for _, b in seqs]
        + [row_spec(a) for a in rows] + [_full_spec(a) for a in consts]
        + [pl.BlockSpec((1, n_heads, HEAD_DIM, HEAD_DIM), lambda i: (rev(i), 0, 0, 0)),
           pl.BlockSpec((cs, hk), lambda i, b=d_out[1]: (rev(i), b))] + [_ANY] * nx,
        out_specs=[pl.BlockSpec((cs, ns * hk), lambda i: (rev(i), 0))]
        + [row_spec(a) for a in rows[:n_row_grads]] + [_full_spec(a) for a in consts[:n_const_grads]] + [_ANY] * nx,
        out_shape=[jax.ShapeDtypeStruct((t, ns * hk), dseq_dtype)]
        + [jax.ShapeDtypeStruct(a.shape, F32) for a in rows[:n_row_grads]]
        + [jax.ShapeDtypeStruct(a.shape, F32) for a in consts[:n_const_grads]] + [ex.dst for ex in comm],
        scratch_shapes=[pltpu.VMEM((n_heads, HEAD_DIM, HEAD_DIM), F32)] + _EXCHANGE_SEMS * nx,
        compiler_params=_params("arbitrary"),
    )(*[a for a, _ in seqs], *rows, *consts, states, d_out[0], *[ex.src for ex in comm])
    return (res[0], list(res[1:1 + n_row_grads]), list(res[1 + n_row_grads:n_grads]), list(res[n_grads:]))


def _gdn_make_chunk(row_vals, const_vals):
    beta_r, g_r, *saved = row_vals
    (onw,) = const_vals
    t_saved = saved[0] if saved else None
    return lambda state, qc, kc, vc, z: _gdn_chunk(state, qc, kc, vc, z, beta_r, g_r, onw, t_saved)


def _ret_make_chunk(row_vals, const_vals):
    cos2, sin2 = row_vals
    d_mat, dec_q, dec_k, dec_c = const_vals
    return _make_ret_chunk(cos2, sin2, d_mat, dec_q, dec_k, dec_c)


def _adamw_math(w, g, m, v):
    m = ADAM_B1 * m + (1.0 - ADAM_B1) * g
    v = ADAM_B2 * v + (1.0 - ADAM_B2) * jnp.square(g)
    m_hat = m / (1.0 - ADAM_B1 ** ADAM_STEP)
    v_hat = v / (1.0 - ADAM_B2 ** ADAM_STEP)
    delta = -ADAM_LR * (m_hat / (jnp.sqrt(v_hat) + ADAM_EPS) + ADAM_WD * w)
    return delta, m, v


def _sum_parts(parts, name):
    n_parts, rows, cols = parts.shape
    tb = _row_tile(rows, 512)

    def body(p_ref, o_ref):
        g = p_ref[0].astype(F32)
        for s in range(1, n_parts):
            g = g + p_ref[s].astype(F32)
        o_ref[...] = g

    return _pallas_call(
        body, name=name, grid=(rows // tb,),
        in_specs=[pl.BlockSpec((n_parts, tb, cols), lambda i: (0, i, 0))],
        out_specs=pl.BlockSpec((tb, cols), lambda i: (i, 0)),
        out_shape=jax.ShapeDtypeStruct((rows, cols), F32),
        compiler_params=_params("parallel"),
    )(parts)


def _adamw(parts, w, m, v, name):
    rows, cols = w.shape
    n_parts = parts.shape[0]
    tb = _row_tile(rows, max(8, (1 << 17) // cols // 8 * 8))

    def body(p_ref, w_ref, m_ref, v_ref, g_out, d_out, m_out, v_out):
        g = p_ref[0].astype(F32)
        for s in range(1, n_parts):
            g = g + p_ref[s].astype(F32)
        delta, m_new, v_new = _adamw_math(w_ref[...], g, m_ref[...], v_ref[...])
        g_out[...] = g
        d_out[...] = delta
        m_out[...] = m_new
        v_out[...] = v_new

    spec = pl.BlockSpec((tb, cols), lambda i: (i, 0))
    return _pallas_call(
        body, name=name, grid=(rows // tb,),
        in_specs=[pl.BlockSpec((n_parts, tb, cols), lambda i: (0, i, 0)), spec, spec, spec],
        out_specs=[spec] * 4, out_shape=[jax.ShapeDtypeStruct((rows, cols), F32)] * 4,
        compiler_params=_params("parallel"),
    )(parts, w, m, v)


def _pack(arrays, rows_multiple=8):
    flat = jnp.concatenate([a.reshape(-1).astype(F32) for a in arrays])
    n = flat.shape[0]
    rows = -(-n // LANES)
    rows = -(-rows // rows_multiple) * rows_multiple
    return jnp.pad(flat, (0, rows * LANES - n)).reshape(rows, LANES)


def _unpack(packed, shapes):
    flat = packed.reshape(-1)
    out, o = [], 0
    for s in shapes:
        n = int(np.prod(s))
        out.append(flat[o:o + n].reshape(s))
        o += n
    return out


def _gather(shard):
    return _Exchange("gather", shard.astype(BF16))


def _rows_of(gathered):
    return gathered.reshape(gathered.shape[0] * gathered.shape[1], gathered.shape[2])


def _scatter_rows(full):
    k, n = full.shape
    return _Exchange("scatter", full.reshape(N_DEV, k // N_DEV, n))


def _relu2_epilogue(u):
    return u, jnp.square(jax.nn.relu(u))


def _relu2_vjp_epilogue(dr, u):
    return (dr * (2.0 * jax.nn.relu(u.astype(F32))),)


def kernel(x, norm_w, la_w_in, la_conv_w, la_a_log, la_dt_bias, la_out_norm_w, la_w_out, sg_w_in, sg_ln_w, sg_ln_b, sg_w_s, sg_b_s, sg_w_out, ffn_w_up, ffn_w_down, loss_target, m_norm_w, m_la_w_in, m_la_conv_w, m_la_a_log, m_la_dt_bias, m_la_out_norm_w, m_la_w_out, m_sg_w_in, m_sg_ln_w, m_sg_ln_b, m_sg_w_s, m_sg_b_s, m_sg_w_out, m_ffn_w_up, m_ffn_w_down, v_norm_w, v_la_w_in, v_la_conv_w, v_la_a_log, v_la_dt_bias, v_la_out_norm_w, v_la_w_out, v_sg_w_in, v_sg_ln_w, v_sg_ln_b, v_sg_w_s, v_sg_b_s, v_sg_w_out, v_ffn_w_up, v_ffn_w_down):
    weights = dict(norm_w=norm_w, la_w_in=la_w_in, la_conv_w=la_conv_w, la_a_log=la_a_log, la_dt_bias=la_dt_bias,
                   la_out_norm_w=la_out_norm_w, la_w_out=la_w_out, sg_w_in=sg_w_in, sg_ln_w=sg_ln_w, sg_ln_b=sg_ln_b,
                   sg_w_s=sg_w_s, sg_b_s=sg_b_s, sg_w_out=sg_w_out, ffn_w_up=ffn_w_up, ffn_w_down=ffn_w_down)
    mom1 = dict(norm_w=m_norm_w, la_w_in=m_la_w_in, la_conv_w=m_la_conv_w, la_a_log=m_la_a_log, la_dt_bias=m_la_dt_bias,
                la_out_norm_w=m_la_out_norm_w, la_w_out=m_la_w_out, sg_w_in=m_sg_w_in, sg_ln_w=m_sg_ln_w, sg_ln_b=m_sg_ln_b,
                sg_w_s=m_sg_w_s, sg_b_s=m_sg_b_s, sg_w_out=m_sg_w_out, ffn_w_up=m_ffn_w_up, ffn_w_down=m_ffn_w_down)
    mom2 = dict(norm_w=v_norm_w, la_w_in=v_la_w_in, la_conv_w=v_la_conv_w, la_a_log=v_la_a_log, la_dt_bias=v_la_dt_bias,
                la_out_norm_w=v_la_out_norm_w, la_w_out=v_la_w_out, sg_w_in=v_sg_w_in, sg_ln_w=v_sg_ln_w, sg_ln_b=v_sg_ln_b,
                sg_w_s=v_sg_w_s, sg_b_s=v_sg_b_s, sg_w_out=v_sg_w_out, ffn_w_up=v_ffn_w_up, ffn_w_down=v_ffn_w_down)
    names = list(weights)

    t, d = x.shape[1], x.shape[2]
    n_heads = la_a_log.shape[-1]
    hk = n_heads * HEAD_DIM
    cs = LA_CHUNK
    n_chunks = t // cs
    sg_width = sg_w_out.shape[1] * N_DEV
    me = 4 * lax.axis_index("x") + 2 * lax.axis_index("y") + lax.axis_index("c")
    xs = x[0]
    tgt = loss_target[0]

    w_in_g = _all_gather(la_w_in[0].astype(BF16), "ag_la_w_in")
    w_in_full = jnp.concatenate([w_in_g[j] for j in range(N_DEV)], axis=1)
    w_main = jnp.concatenate([w_in_full[:, :4 * hk], w_in_full[:, 4 * hk + 2 * n_heads:]], axis=1)
    w_gate = jnp.pad(w_in_full[:, 4 * hk:4 * hk + 2 * n_heads], ((0, 0), (0, LANES - 2 * n_heads)))
    conv_t = lambda a: jnp.transpose(a[0])
    small_local = lambda src: [src["norm_w"], conv_t(src["la_conv_w"]), src["sg_ln_w"], src["sg_ln_b"]]
    small_all = _all_gather(_pack(small_local(weights)), "ag_small")
    dsh = d // N_DEV
    csh = la_conv_w.shape[1]
    wsh = sg_width // N_DEV
    nw_parts, conv_parts, lnw_parts, lnb_parts = zip(*[
        _unpack(small_all[j], [(2, 4, dsh), (CONV_WIDTH, csh), (wsh,), (wsh,)]) for j in range(N_DEV)])
    nw = jnp.concatenate(nw_parts, axis=-1)
    conv_w = jnp.concatenate(conv_parts, axis=1)
    ln_w = jnp.concatenate(lnw_parts)[None, :]
    ln_b = jnp.concatenate(lnb_parts)[None, :]
    nrm = lambda l, j: nw[l, j][None, :]

    tb_d = _row_tile(t, 256)
    tb_wide = _row_tile(t, 64)

    (y0,) = _rowwise(_seg_norm, [xs], [nrm(0, 0)], [(d, BF16)], tb_d, "f_norm0")
    proj, w_up0_g = _matmul(y0, w_main, "nn", F32, "f_proj",
                            comm=[_gather(ffn_w_up[0])])
    gate_raw = _matmul(y0, w_gate, "nn", F32, "f_gate")
    a_log_v = jnp.pad(la_a_log.reshape(1, n_heads), ((0, 0), (n_heads, LANES - 2 * n_heads)))
    dt_bias_v = jnp.pad(la_dt_bias.reshape(1, n_heads), ((0, 0), (n_heads, LANES - 2 * n_heads)))
    seg_gates = _make_seg_gates(n_heads)
    (bg,) = _rowwise(seg_gates, [gate_raw], [a_log_v, dt_bias_v], [(LANES, F32)], tb_d, "f_gates")
    to_rows = lambda a: jnp.transpose(a.reshape(n_chunks, cs, n_heads), (0, 2, 1))[:, :, None, :]
    beta_r, g_r = to_rows(bg[:, :n_heads]), to_rows(bg[:, n_heads:2 * n_heads])
    qkvc = _conv_silu(proj, conv_w, 3 * hk, "f_conv")
    onw = la_out_norm_w.reshape(1, 1, HEAD_DIM)
    gdn_seqs = [(qkvc, 0), (qkvc, 1), (qkvc, 2), (proj, 3)]
    o_a, gdn_states, gdn_t_inv, w_down0_g = _scan_fwd(
        _gdn_make_chunk, gdn_seqs, [beta_r, g_r], [onw], n_heads, "f_gdn", comm=[_gather(ffn_w_down[0])],
        saved_shape=(n_heads, cs, cs))

    pos = jnp.arange(t, dtype=F32)
    inv_freq = 1.0 / (ROPE_BASE ** jnp.linspace(0.0, 1.0, HEAD_DIM // 2, dtype=F32))
    ang = pos[:, None] * inv_freq[None, :]
    cos2 = jnp.concatenate([jnp.cos(ang), jnp.cos(ang)], axis=-1).reshape(n_chunks, cs, HEAD_DIM)
    sin2 = jnp.concatenate([-jnp.sin(ang), jnp.sin(ang)], axis=-1).reshape(n_chunks, cs, HEAD_DIM)
    log_gamma = jnp.log1p(-jnp.power(2.0, -5.0 - jnp.arange(n_heads, dtype=F32)))
    cpos = jnp.arange(cs, dtype=F32)
    rel = cpos[:, None] - cpos[None, :]
    d_mat = jnp.where(rel >= 0, jnp.exp(jnp.where(rel >= 0, rel, 0.0) * log_gamma[:, None, None]), 0.0)
    dec_q = jnp.exp((cpos + 1.0) * log_gamma[:, None])[..., None]
    dec_k = jnp.exp((cs - 1.0 - cpos) * log_gamma[:, None])[..., None]
    dec_c = jnp.exp(cs * log_gamma)[:, None, None]
    ret_seqs = [(proj, 4), (proj, 5), (proj, 6), (proj, 7)]
    ret_consts = [d_mat, dec_q, dec_k, dec_c]
    o_b, ret_states, w_la_out_g = _scan_fwd(_ret_make_chunk, ret_seqs, [cos2, sin2], ret_consts, n_heads, "f_ret",
                                            comm=[_gather(la_w_out[0])])
    o_mix = jnp.concatenate([o_a, o_b], axis=1)
    y1 = _matmul(o_mix, _rows_of(w_la_out_g), "nn", F32, "f_la_out")
    h1, a1 = _rowwise(_seg_residual, [xs, y1], [nrm(0, 1), nrm(0, 2)], [(d, F32), (d, BF16)], tb_d, "f_res0a")
    u1, r1, w_sg_in_g = _matmul(a1, w_up0_g, "nn", None, "f_up0", b_blocked=True, epilogue=_relu2_epilogue,
                                out_dtypes=[F32, BF16], comm=[_gather(sg_w_in[0])])
    y2, w_sg_out_g = _matmul(r1, _rows_of(w_down0_g), "nn", F32, "f_down0", comm=[_gather(sg_w_out[0])])
    h2, a2 = _rowwise(_seg_residual, [h1, y2], [nrm(0, 3), nrm(1, 0)], [(d, F32), (d, BF16)], tb_d, "f_res0b")

    p_sg, w_up1_g = _matmul(a2, w_sg_in_g, "nn", F32, "f_sg_in", b_blocked=True,
                            comm=[_gather(ffn_w_up[1])])
    ua, vn = _rowwise(_seg_gelu_ln, [(p_sg, 0, sg_width), (p_sg, 1, sg_width)], [ln_w, ln_b],
                      [(sg_width, F32), (sg_width, BF16)], tb_wide, "f_gelu_ln")
    ws = sg_w_s[0]
    bs = sg_b_s[0][:, :, None]
    gated = _spatial_gate(ua, vn, ws, bs, "f_sgate")
    y3 = _matmul(gated, _rows_of(w_sg_out_g), "nn", F32, "f_sg_out")
    h3, a3 = _rowwise(_seg_residual, [h2, y3], [nrm(1, 1), nrm(1, 2)], [(d, F32), (d, BF16)], tb_d, "f_res1a")
    u2, r2, w_down1_g = _matmul(a3, w_up1_g, "nn", None, "f_up1", b_blocked=True, epilogue=_relu2_epilogue,
                                out_dtypes=[F32, BF16], comm=[_gather(ffn_w_down[1])])
    y4 = _matmul(r2, _rows_of(w_down1_g), "nn", F32, "f_down1")

    ones = jnp.ones((t, 1), F32)
    (dh3, dy4), (dnw13,), loss_rows = _rowwise_vjp(
        _seg_loss, [h3, y4, tgt], [nrm(1, 3)], [ones], [F32, BF16, None], tb_d, "b_loss", primal_out=0)
    loss = lax.psum(jnp.sum(loss_rows), ("x", "y", "c"))

    du2 = _matmul(dy4, _rows_of(w_down1_g), "nt", BF16, "b_down1", extra=[u2], epilogue=_relu2_vjp_epilogue)
    dw_down1 = _matmul(r2, dy4, "tn", BF16, "b_dw_down1")
    da3, parts_down1 = _matmul(du2, w_up1_g, "nt", F32, "b_up1", b_blocked=True, comm=[_scatter_rows(dw_down1)])
    dw_up1 = _matmul(a3, du2, "tn", BF16, "b_dw_up1", out_blocked=True)
    (dh2, dy3), (dnw11, dnw12) = _rowwise_vjp(
        _seg_residual, [h2, y3], [nrm(1, 1), nrm(1, 2)], [dh3, da3], [F32, BF16], tb_d, "b_res1a")
    dgated = _matmul(dy3, _rows_of(w_sg_out_g), "nt", F32, "b_sg_out")
    dw_sg_out = _matmul(gated, dy3, "tn", BF16, "b_dw_sg_out")
    dua, dvn, dws, dbs = _spatial_gate_vjp(ua, vn, ws, bs, dgated, "b_sgate")
    (dp_sg,), (dln_w, dln_b) = _rowwise_vjp(
        _seg_gelu_ln, [(p_sg, 0, sg_width), (p_sg, 1, sg_width)], [ln_w, ln_b], [dua, dvn], [BF16, BF16],
        tb_wide, "b_gelu_ln", merge_dx=True)
    da2, parts_sg_out = _matmul(dp_sg, w_sg_in_g, "nt", F32, "b_sg_in", b_blocked=True,
                                comm=[_scatter_rows(dw_sg_out)])
    dw_sg_in, parts_up1 = _matmul(a2, dp_sg, "tn", BF16, "b_dw_sg_in", out_blocked=True,
                                  comm=[_Exchange("scatter", dw_up1)])
    (dh1, dy2), (dnw03, dnw10) = _rowwise_vjp(
        _seg_residual, [h1, y2], [nrm(0, 3), nrm(1, 0)], [dh2, da2], [F32, BF16], tb_d, "b_res0b")
    du1, parts_sg_in = _matmul(dy2, _rows_of(w_down0_g), "nt", BF16, "b_down0", extra=[u1],
                               epilogue=_relu2_vjp_epilogue, comm=[_Exchange("scatter", dw_sg_in)])
    dw_down0 = _matmul(r1, dy2, "tn", BF16, "b_dw_down0")
    da1, parts_down0 = _matmul(du1, w_up0_g, "nt", F32, "b_up0", b_blocked=True, comm=[_scatter_rows(dw_down0)])
    dw_up0 = _matmul(a1, du1, "tn", BF16, "b_dw_up0", out_blocked=True)
    (dx_res, dy1), (dnw01, dnw02) = _rowwise_vjp(
        _seg_residual, [xs, y1], [nrm(0, 1), nrm(0, 2)], [dh1, da1], [F32, BF16], tb_d, "b_res0a")
    do_mix = _matmul(dy1, _rows_of(w_la_out_g), "nt", F32, "b_la_out")
    dw_la_out = _matmul(o_mix, dy1, "tn", BF16, "b_dw_la_out")

    d_ret, _, _, _ = _scan_bwd(_ret_make_chunk, ret_seqs, [cos2, sin2], ret_consts, ret_states, (do_mix, 1),
                               n_heads, 0, 0, BF16, "b_ret")
    d_gdn, (dbeta_r, dg_r), (donw,), (parts_up0, parts_la_out) = _scan_bwd(
        _gdn_make_chunk, gdn_seqs, [beta_r, g_r, gdn_t_inv], [onw], gdn_states, (do_mix, 0), n_heads, 2, 1, F32, "b_gdn",
        comm=[_Exchange("scatter", dw_up0), _scatter_rows(dw_la_out)])
    from_rows = lambda a: jnp.transpose(a[:, :, 0, :], (0, 2, 1)).reshape(t, n_heads)
    dbg = jnp.pad(jnp.concatenate([from_rows(dbeta_r), from_rows(dg_r)], axis=1), ((0, 0), (0, LANES - 2 * n_heads)))
    (dgate_raw,), (da_log_v, ddt_bias_v) = _rowwise_vjp(
        seg_gates, [gate_raw], [a_log_v, dt_bias_v], [dbg], [BF16], tb_d, "b_gates")
    dqkv, dconv_w = _conv_silu_vjp(proj, conv_w, d_gdn, 3 * hk, "b_conv")
    dproj = jnp.concatenate([dqkv, d_gdn[:, 3 * hk:].astype(BF16), d_ret], axis=1)
    dw_main = _matmul(y0, dproj, "tn", BF16, "b_dw_proj")
    dw_gate = _matmul(y0, dgate_raw, "tn", BF16, "b_dw_gate")
    dw_in_full = jnp.concatenate([dw_main[:, :4 * hk], dw_gate[:, :2 * n_heads], dw_main[:, 4 * hk:]], axis=1)
    in_cols = dw_in_full.shape[1] // N_DEV
    dw_in_blocks = jnp.stack([dw_in_full[:, j * in_cols:(j + 1) * in_cols] for j in range(N_DEV)])
    dy0, parts_in = _matmul(dproj, w_main, "nt", F32, "b_proj", comm=[_Exchange("scatter", dw_in_blocks)])
    dy0_gate = _matmul(dgate_raw, w_gate, "nt", F32, "b_gate")
    (grad_x,), (dnw00,) = _rowwise_vjp(
        _seg_norm, [xs], [nrm(0, 0)], [(dy0, dy0_gate)], [F32], tb_d, "b_norm0", adds=[dx_res])

    outs = {}
    for name, parts in (("la_w_in", parts_in), ("la_w_out", parts_la_out), ("sg_w_in", parts_sg_in),
                        ("sg_w_out", parts_sg_out)):
        res = _adamw(parts, weights[name][0], mom1[name][0], mom2[name][0], f"adamw_{name}")
        outs[name] = [r[None] for r in res]
    for name, parts_l in (("ffn_w_up", [parts_up0, parts_up1]), ("ffn_w_down", [parts_down0, parts_down1])):
        res_l = [_adamw(parts_l[l], weights[name][l], mom1[name][l], mom2[name][l], f"adamw_{name}{l}")
                 for l in range(2)]
        outs[name] = [jnp.stack([res_l[0][j], res_l[1][j]]) for j in range(4)]

    dnorm = jnp.stack([jnp.concatenate([dnw00, dnw01, dnw02, dnw03], axis=0),
                       jnp.concatenate([dnw10, dnw11, dnw12, dnw13], axis=0)])
    small_grads = [dnorm, dconv_w, dln_w[0], dln_b[0],
                   da_log_v[:, n_heads:2 * n_heads], ddt_bias_v[:, n_heads:2 * n_heads],
                   donw.reshape(1, HEAD_DIM), dws[None], dbs[None, :, :, 0]]
    small_parts = _all_gather(_pack(small_grads), "ag_small_grads")
    gs = _unpack(_sum_parts(small_parts, "sum_small_grads"), [g.shape for g in small_grads])
    sharded = lambda full, axis, size: lax.dynamic_slice_in_dim(full, me * size, size, axis)
    own = [sharded(gs[0], 2, dsh), sharded(gs[1], 1, csh), sharded(gs[2], 0, wsh)[None], sharded(gs[3], 0, wsh)[None]]
    own += gs[4:]
    replicated = ["la_a_log", "la_dt_bias", "la_out_norm_w", "sg_w_s", "sg_b_s"]
    small_names = ["norm_w", "la_conv_w", "sg_ln_w", "sg_ln_b"] + replicated
    pk = lambda src: _pack(small_local(src) + [src[n] for n in replicated])
    res = _adamw(_pack(own)[None], pk(weights), pk(mom1), pk(mom2), "adamw_small")
    for j, r in enumerate(res):
        vals = _unpack(r, [g.shape for g in own])
        vals[1] = jnp.transpose(vals[1])[None]
        for n, val in zip(small_names, vals):
            outs.setdefault(n, [None] * 4)[j] = val

    grad_x = grad_x[None]
    return (loss, grad_x, *[outs[n][0] for n in names], *[outs[n][1] for n in names],
            *[outs[n][2] for n in names], *[outs[n][3] for n in names])
```

```python
import functools
import math

import jax
import jax.numpy as jnp
import numpy as np
from jax import lax
from jax.experimental import pallas as pl
from jax.experimental.pallas import tpu as pltpu

F32, BF16 = jnp.float32, jnp.bfloat16
_pallas_call = pl.pallas_call

N_DEV = 8
LANES = 128
V7X_VMEM_BYTES = 64 * 1024 * 1024
VMEM_LIMIT = (V7X_VMEM_BYTES * 3) // 4
HEAD_DIM = 128
LA_CHUNK = 64
INV_BLOCK = 8
SG_CHUNK = 128
SG_STEP_CHUNKS = 8
SG_GROUPS = 8
CONV_WIDTH = 4
HALO = 8
ROPE_BASE = 10000.0
EPS = 1e-6
ADAM_LR, ADAM_B1, ADAM_B2, ADAM_EPS, ADAM_WD, ADAM_STEP = 0.001, 0.9, 0.999, 1e-08, 0.01, 10
MESH_ID = pl.DeviceIdType.MESH


def _params(*sem):
    return pltpu.CompilerParams(dimension_semantics=sem or None, vmem_limit_bytes=VMEM_LIMIT)


def _tile(dim, pref):
    if dim <= pref:
        return dim
    t = (pref // LANES) * LANES
    while dim % t:
        t -= LANES
    return t


def _row_tile(rows, pref):
    t = min(rows, pref)
    while rows % t:
        t -= 8
    return t


def _all_gather(shard, name):
    def body(x_ref, out_ref, send_sems, recv_sems, local_sem):
        x, y, c = lax.axis_index("x"), lax.axis_index("y"), lax.axis_index("c")
        me, sibling = (x, y, c), (x, y, 1 - c)
        chips = [(1 - x, y), (x, 1 - y), (1 - x, 1 - y)]

        def rows(px, py, pc):
            return out_ref.at[4 * px + 2 * py + pc]

        def copy(k, block, to, src=None):
            return pltpu.make_async_remote_copy(
                src_ref=rows(*block) if src is None else src, dst_ref=rows(*block),
                send_sem=send_sems.at[k], recv_sem=recv_sems.at[k], device_id=to, device_id_type=MESH_ID)

        mine = pltpu.make_async_copy(x_ref, rows(*me), local_sem)
        mine.start()
        first = [copy(0, me, sibling, src=x_ref)]
        first += [copy(1 + j, me, (*chip, c), src=x_ref) for j, chip in enumerate(chips)]
        for cp in first:
            cp.start()
        passed = [copy(4 + j, (*chip, c), sibling) for j, chip in enumerate(chips)]
        for j, chip in enumerate(chips):
            copy(1 + j, (*chip, c), me).wait_recv()
            passed[j].start()
        copy(0, sibling, me).wait_recv()
        for j, chip in enumerate(chips):
            copy(4 + j, (*chip, 1 - c), me).wait_recv()
        for cp in first + passed:
            cp.wait_send()
        mine.wait()

    return _pallas_call(
        body, name=name,
        out_shape=jax.ShapeDtypeStruct((N_DEV,) + shard.shape, shard.dtype),
        in_specs=[pl.BlockSpec(memory_space=pl.ANY)],
        out_specs=pl.BlockSpec(memory_space=pl.ANY),
        scratch_shapes=[pltpu.SemaphoreType.DMA((7,)), pltpu.SemaphoreType.DMA((7,)), pltpu.SemaphoreType.DMA],
    )(shard)


class _Exchange:
    def __init__(self, kind, src):
        self.kind, self.src = kind, src
        shape = (N_DEV,) + src.shape if kind == "gather" else src.shape
        self.dst = jax.ShapeDtypeStruct(shape, src.dtype)

    def _scatter_copies(self, src_ref, dst_ref, send_sems, recv_sems, local_sem):
        x, y, c = lax.axis_index("x"), lax.axis_index("y"), lax.axis_index("c")
        me = 4 * x + 2 * y + c
        local = pltpu.make_async_copy(src_ref.at[me], dst_ref.at[me], local_sem)
        remote = []
        for k in range(1, N_DEV):
            px, py, pc = x ^ ((k >> 2) & 1), y ^ ((k >> 1) & 1), c ^ (k & 1)
            remote.append(pltpu.make_async_remote_copy(
                src_ref=src_ref.at[4 * px + 2 * py + pc], dst_ref=dst_ref.at[me],
                send_sem=send_sems.at[k - 1], recv_sem=recv_sems.at[k - 1],
                device_id=(px, py, pc), device_id_type=MESH_ID))
        return local, remote

    def _gather_copies(self, src_ref, dst_ref, send_sems, recv_sems, local_sem):
        x, y, c = lax.axis_index("x"), lax.axis_index("y"), lax.axis_index("c")
        me, sibling = (x, y, c), (x, y, 1 - c)
        chips = [(1 - x, y), (x, 1 - y), (1 - x, 1 - y)]
        rows = lambda px, py, pc: dst_ref.at[4 * px + 2 * py + pc]

        def copy(k, block, to, src=None):
            return pltpu.make_async_remote_copy(
                src_ref=rows(*block) if src is None else src, dst_ref=rows(*block),
                send_sem=send_sems.at[k], recv_sem=recv_sems.at[k], device_id=to, device_id_type=MESH_ID)

        makers = {
            "local": lambda: pltpu.make_async_copy(src_ref, rows(*me), local_sem),
            "own": lambda: [copy(0, me, sibling, src=src_ref)] + [copy(1 + j, me, (*chip, c), src=src_ref)
                                                                  for j, chip in enumerate(chips)],
            "passed": lambda: [copy(4 + j, (*chip, c), sibling) for j, chip in enumerate(chips)],
            "landed": lambda: [copy(1 + j, (*chip, c), me) for j, chip in enumerate(chips)],
            "from_sibling": lambda: [copy(0, sibling, me)] + [copy(4 + j, (*chip, 1 - c), me)
                                                              for j, chip in enumerate(chips)],
        }
        return lambda *names: [makers[n]() for n in names]

    def start(self, *refs):
        if self.kind == "gather":
            local, own = self._gather_copies(*refs)("local", "own")
        else:
            local, own = self._scatter_copies(*refs)
        local.start()
        for cp in own:
            cp.start()

    def middle(self, *refs):
        if self.kind == "gather":
            landed, passed = self._gather_copies(*refs)("landed", "passed")
            for arrived, forward in zip(landed, passed):
                arrived.wait_recv()
                forward.start()

    def finish(self, *refs):
        if self.kind == "gather":
            local, own, passed, from_sibling = self._gather_copies(*refs)("local", "own", "passed", "from_sibling")
            for cp in from_sibling:
                cp.wait_recv()
            for cp in own + passed:
                cp.wait_send()
        else:
            local, remote = self._scatter_copies(*refs)
            for cp in remote:
                cp.wait_recv()
            for cp in remote:
                cp.wait_send()
        local.wait()


_ANY = pl.BlockSpec(memory_space=pl.ANY)
_EXCHANGE_SEMS = [pltpu.SemaphoreType.DMA((N_DEV - 1,)), pltpu.SemaphoreType.DMA((N_DEV - 1,)), pltpu.SemaphoreType.DMA]


def _carry(exchanges, src_refs, dst_refs, sem_refs, first, middle, last, compute):
    def each():
        for e, (ex, s, d) in enumerate(zip(exchanges, src_refs, dst_refs)):
            yield ex, (s, d, *sem_refs[3 * e:3 * e + 3])

    if exchanges:
        @pl.when(first)
        def _():
            for ex, refs in each():
                ex.start(*refs)

    compute()

    if any(ex.kind == "gather" for ex in exchanges):
        @pl.when(middle)
        def _():
            for ex, refs in each():
                ex.middle(*refs)

    if exchanges:
        @pl.when(last)
        def _():
            for ex, refs in each():
                ex.finish(*refs)


def _matmul(a, b, mode, out_dtype, name, b_blocked=False, out_blocked=False, extra=(), epilogue=None,
            out_dtypes=None, comm=()):
    if mode == "nn":
        (m, k), n = a.shape, b.shape[-1] * (N_DEV if b_blocked else 1)
    elif mode == "nt":
        (m, k), n = a.shape, b.shape[-2]
    else:
        (k, m), n = a.shape, b.shape[1]
    tm, tn, tk = _tile(m, 1024), _tile(n, 1024), _tile(k, 2048)
    if (b_blocked and mode == "nn") or out_blocked:
        tn = n // N_DEV
    span = 1
    if b_blocked and mode == "nt":
        kb = k // N_DEV
        span = max(1, min(N_DEV, 2048 // kb))
        tk = span * kb
    nj, ni, nk = n // tn, m // tm, k // tk
    dims = {"nn": (((1,), (0,)), ((), ())), "nt": (((1,), (1,)), ((), ())), "tn": (((0,), (0,)), ((), ()))}[mode]
    out_dtypes = out_dtypes or [out_dtype]
    ne, nc, no = len(extra), len(comm), len(out_dtypes)

    def body(*refs):
        a_ref, b_refs = refs[0], refs[1:1 + span]
        refs = refs[span - 1:]
        extra_refs, src_refs = refs[2:2 + ne], refs[2 + ne:2 + ne + nc]
        o = 2 + ne + nc
        out_refs, dst_refs = refs[o:o + no], refs[o + no:o + no + nc]
        rest = refs[o + no + nc:]
        acc, sems = (rest[0], rest[1:]) if nk > 1 else (None, rest)
        j, i, kk = pl.program_id(0), pl.program_id(1), pl.program_id(2)

        def finish(res):
            outs = epilogue(res, *[r[...] for r in extra_refs]) if epilogue else (res,)
            for ref, val in zip(out_refs, outs):
                ref[...] = val.astype(ref.dtype)

        def compute():
            a_tile = a_ref[...].astype(BF16)
            kw = a_tile.shape[1] // span if mode == "nt" else None
            prod = None
            for s, b_ref in enumerate(b_refs):
                a_part = a_tile if span == 1 else a_tile[:, s * kw:(s + 1) * kw]
                part = lax.dot_general(a_part, b_ref[...].astype(BF16), dims, preferred_element_type=F32)
                prod = part if prod is None else prod + part
            if nk == 1:
                finish(prod)
                return

            @pl.when(kk == 0)
            def _():
                acc[...] = prod

            @pl.when(kk > 0)
            def _():
                acc[...] += prod

            @pl.when(kk == nk - 1)
            def _():
                finish(acc[...])

        step = (j * ni + i) * nk + kk
        _carry(comm, src_refs, dst_refs, sems, step == 0, step == (nj * ni * nk) // 2, step == nj * ni * nk - 1, compute)

    a_spec = (pl.BlockSpec((tk, tm), lambda j, i, kk: (kk, i)) if mode == "tn"
              else pl.BlockSpec((tm, tk), lambda j, i, kk: (i, kk)))
    if b_blocked and mode == "nn":
        b_specs = [pl.BlockSpec((None, tk, tn), lambda j, i, kk: (j, kk, 0))]
    elif b_blocked:
        b_specs = [pl.BlockSpec((None, tn, tk // span), lambda j, i, kk, s=s: (kk * span + s, j, 0)) for s in range(span)]
    else:
        b_specs = [pl.BlockSpec((tn, tk), lambda j, i, kk: (j, kk)) if mode == "nt"
                   else pl.BlockSpec((tk, tn), lambda j, i, kk: (kk, j))]
    tile_spec = pl.BlockSpec((tm, tn), lambda j, i, kk: (i, j))
    if out_blocked:
        o_spec = pl.BlockSpec((None, tm, tn), lambda j, i, kk: (j, i, 0))
        o_shape = (N_DEV, m, tn)
    else:
        o_spec, o_shape = tile_spec, (m, n)
    res = _pallas_call(
        body, name=name, grid=(nj, ni, nk),
        in_specs=[a_spec] + b_specs + [tile_spec] * ne + [_ANY] * nc,
        out_specs=[o_spec] * no + [_ANY] * nc,
        out_shape=[jax.ShapeDtypeStruct(o_shape, d) for d in out_dtypes] + [ex.dst for ex in comm],
        scratch_shapes=([pltpu.VMEM((tm, tn), F32)] if nk > 1 else []) + _EXCHANGE_SEMS * nc,
        compiler_params=_params("arbitrary", "arbitrary", "arbitrary"),
    )(a, *[b] * span, *extra, *[ex.src for ex in comm])
    return res[0] if len(res) == 1 else res


def _col_spec(tb, spec):
    if isinstance(spec, tuple):
        arr, blk, width = spec
        return arr, pl.BlockSpec((tb, width), lambda i, blk=blk: (i, blk))
    return spec, pl.BlockSpec((tb, spec.shape[1]), lambda i: (i, 0))


def _full_spec(p):
    return pl.BlockSpec(p.shape, lambda i, nd=p.ndim: (0,) * nd)


def _rowwise(fn, xs, params, outs, tb, name):
    arrs, specs = zip(*[_col_spec(tb, s) for s in xs])
    t = arrs[0].shape[0]
    nx, npar = len(xs), len(params)

    def body(*refs):
        res = fn(*[r[...] for r in refs[:nx + npar]])
        for o_ref, r in zip(refs[nx + npar:], res):
            o_ref[...] = r.astype(o_ref.dtype)

    return _pallas_call(
        body, name=name, grid=(t // tb,),
        in_specs=list(specs) + [_full_spec(p) for p in params],
        out_specs=[pl.BlockSpec((tb, w), lambda i: (i, 0)) for w, _ in outs],
        out_shape=[jax.ShapeDtypeStruct((t, w), d) for w, d in outs],
        compiler_params=_params("parallel"),
    )(*arrs, *params)


def _rowwise_vjp(fn, xs, params, cts, dx_dtypes, tb, name, adds=None, primal_out=None, primal_width=1,
                 merge_dx=False):
    arrs, specs = zip(*[_col_spec(tb, s) for s in xs])
    t = arrs[0].shape[0]
    ct_groups = [c if isinstance(c, tuple) else (c,) for c in cts]
    ct_flat = [a for grp in ct_groups for a in grp]
    nx, npar, nct = len(xs), len(params), len(ct_flat)
    adds = adds or [None] * nx
    add_ix = [i for i in range(nx) if adds[i] is not None]
    dx_ix = [i for i in range(nx) if dx_dtypes[i] is not None]
    widths = [s.block_shape[1] for s in specs]
    n_dx_out = 1 if merge_dx else len(dx_ix)

    def body(*refs):
        x_refs, p_refs = refs[:nx], refs[nx:nx + npar]
        ct_refs = list(refs[nx + npar:nx + npar + nct])
        add_refs = refs[nx + npar + nct:nx + npar + nct + len(add_ix)]
        o = nx + npar + nct + len(add_ix)
        dx_refs, dp_refs = refs[o:o + n_dx_out], refs[o + n_dx_out:o + n_dx_out + npar]
        prim, vjp = jax.vjp(fn, *[r[...] for r in x_refs], *[r[...] for r in p_refs])
        ct_vals = []
        for grp, p in zip(ct_groups, prim):
            val = ct_refs.pop(0)[...].astype(p.dtype)
            for _ in grp[1:]:
                val = val + ct_refs.pop(0)[...].astype(p.dtype)
            ct_vals.append(val)
        grads = vjp(tuple(ct_vals))
        col = 0
        for n, i in enumerate(dx_ix):
            g = grads[i].astype(F32)
            if adds[i] is not None:
                g = g + add_refs[add_ix.index(i)][...].astype(F32)
            if merge_dx:
                dx_refs[0][:, col:col + widths[i]] = g.astype(dx_refs[0].dtype)
                col += widths[i]
            else:
                dx_refs[n][...] = g.astype(dx_refs[n].dtype)

        @pl.when(pl.program_id(0) == 0)
        def _():
            for ref in dp_refs:
                ref[...] = jnp.zeros_like(ref)

        for ref, g in zip(dp_refs, grads[nx:]):
            ref[...] += g.astype(F32)
        if primal_out is not None:
            refs[-1][...] = prim[primal_out].astype(refs[-1].dtype)

    ct_specs = [pl.BlockSpec((tb, c.shape[1]), lambda i: (i, 0)) for c in ct_flat]
    add_specs = [pl.BlockSpec((tb, widths[i]), lambda i_: (i_, 0)) for i in add_ix]
    if merge_dx:
        total = sum(widths[i] for i in dx_ix)
        out_specs = [pl.BlockSpec((tb, total), lambda i_: (i_, 0))]
        out_shape = [jax.ShapeDtypeStruct((t, total), dx_dtypes[dx_ix[0]])]
    else:
        out_specs = [pl.BlockSpec((tb, widths[i]), lambda i_: (i_, 0)) for i in dx_ix]
        out_shape = [jax.ShapeDtypeStruct((t, widths[i]), dx_dtypes[i]) for i in dx_ix]
    out_specs += [_full_spec(p) for p in params]
    out_shape += [jax.ShapeDtypeStruct(p.shape, F32) for p in params]
    if primal_out is not None:
        out_specs.append(pl.BlockSpec((tb, primal_width), lambda i: (i, 0)))
        out_shape.append(jax.ShapeDtypeStruct((t, primal_width), F32))
    res = _pallas_call(
        body, name=name, grid=(t // tb,),
        in_specs=list(specs) + [_full_spec(p) for p in params] + ct_specs + add_specs,
        out_specs=out_specs, out_shape=out_shape,
        compiler_params=_params("arbitrary"),
    )(*arrs, *params, *ct_flat, *[adds[i] for i in add_ix])
    ndx = n_dx_out
    out = (list(res[:ndx]), list(res[ndx:ndx + npar]))
    return out + (res[-1],) if primal_out is not None else out


def _rms(x, w):
    xf = x.astype(F32)
    return xf * lax.rsqrt(jnp.mean(xf * xf, axis=-1, keepdims=True) + EPS) * w


def _seg_norm(x, w):
    return (_rms(x, w),)


def _seg_residual(h, y, w_post, w_pre):
    h2 = h.astype(F32) + _rms(y, w_post)
    return h2, _rms(h2, w_pre)


def _seg_loss(h, y, tgt, w_post):
    err = h.astype(F32) + _rms(y, w_post) - tgt
    return (0.5 * jnp.mean(err * err, axis=-1, keepdims=True),)


GELU_C, GELU_A = math.sqrt(2.0 / math.pi), 0.044715


def _gelu_parts(x):
    x2 = x * x
    th = jnp.tanh(GELU_C * x * (1.0 + GELU_A * x2))
    half = 0.5 * x
    return half * (1.0 + th), 0.5 * (1.0 + th) + half * (1.0 - th * th) * (GELU_C * (1.0 + 3.0 * GELU_A * x2))


def _row_mean(acc, width):
    return jnp.broadcast_to(jnp.sum(acc, axis=-1, keepdims=True) * (1.0 / width), acc.shape)


def _gelu_ln_stats(pv_ref, vbuf, gbuf, chunks, width):
    tb = pv_ref.shape[0]
    acc = jnp.zeros((tb, LANES), F32)
    for sl in chunks:
        v, g = _gelu_parts(pv_ref[:, sl])
        vbuf[:, sl] = v
        if gbuf is not None:
            gbuf[:, sl] = g
        acc = acc + v
    mu = _row_mean(acc, width)
    acc = jnp.zeros((tb, LANES), F32)
    for sl in chunks:
        dev = vbuf[:, sl] - mu
        acc = acc + dev * dev
    return mu, lax.rsqrt(_row_mean(acc, width) + EPS)


def _gelu_ln(p, ln_w, ln_b, tb, name):
    t, w = p.shape[0], p.shape[1] // 2
    chunks = [slice(j, j + LANES) for j in range(0, w, LANES)]

    def body(pu_ref, pv_ref, lnw_ref, lnb_ref, u_ref, vn_ref, vbuf):
        for sl in chunks:
            u_ref[:, sl] = _gelu_parts(pu_ref[:, sl])[0]
        mu, rstd = _gelu_ln_stats(pv_ref, vbuf, None, chunks, w)
        for sl in chunks:
            vn_ref[:, sl] = ((vbuf[:, sl] - mu) * rstd * lnw_ref[:, sl] + lnb_ref[:, sl]).astype(vn_ref.dtype)

    half = lambda b: pl.BlockSpec((tb, w), lambda i, b=b: (i, b))
    vec = pl.BlockSpec((1, w), lambda i: (0, 0))
    return _pallas_call(
        body, name=name, grid=(t // tb,),
        in_specs=[half(0), half(1), vec, vec], out_specs=[half(0), half(0)],
        out_shape=[jax.ShapeDtypeStruct((t, w), F32), jax.ShapeDtypeStruct((t, w), BF16)],
        scratch_shapes=[pltpu.VMEM((tb, w), F32)],
        compiler_params=_params("parallel"),
    )(p, p, ln_w, ln_b)


def _gelu_ln_vjp(p, ln_w, du, dvn, tb, name):
    t, w = p.shape[0], p.shape[1] // 2
    nb = t // tb
    chunks = [slice(j, j + LANES) for j in range(0, w, LANES)]
    fold = lambda a: jnp.sum(a.reshape(tb // 8, 8, LANES), axis=0)

    def body(pu_ref, pv_ref, lnw_ref, du_ref, dvn_ref, dp_ref, dlnw_ref, dlnb_ref, vbuf, gbuf, acc_w, acc_b):
        i = pl.program_id(0)

        @pl.when(i == 0)
        def _():
            acc_w[...] = jnp.zeros_like(acc_w)
            acc_b[...] = jnp.zeros_like(acc_b)

        for sl in chunks:
            dp_ref[:, sl] = (du_ref[:, sl] * _gelu_parts(pu_ref[:, sl])[1]).astype(dp_ref.dtype)
        mu, rstd = _gelu_ln_stats(pv_ref, vbuf, gbuf, chunks, w)
        m1 = jnp.zeros((tb, LANES), F32)
        m2 = jnp.zeros((tb, LANES), F32)
        for sl in chunks:
            xhat = (vbuf[:, sl] - mu) * rstd
            dy = dvn_ref[:, sl]
            dxhat = dy * lnw_ref[:, sl]
            m1 = m1 + dxhat
            m2 = m2 + dxhat * xhat
            acc_w[:, sl] += fold(dy * xhat)
            acc_b[:, sl] += fold(dy)
        m1, m2 = _row_mean(m1, w), _row_mean(m2, w)
        for j, sl in enumerate(chunks):
            xhat = (vbuf[:, sl] - mu) * rstd
            dv = rstd * (dvn_ref[:, sl] * lnw_ref[:, sl] - m1 - xhat * m2)
            dp_ref[:, w + j * LANES:w + (j + 1) * LANES] = (dv * gbuf[:, sl]).astype(dp_ref.dtype)

        @pl.when(i == nb - 1)
        def _():
            dlnw_ref[...] = jnp.sum(acc_w[...], axis=0, keepdims=True)
            dlnb_ref[...] = jnp.sum(acc_b[...], axis=0, keepdims=True)

    half = lambda b: pl.BlockSpec((tb, w), lambda i, b=b: (i, b))
    vec = pl.BlockSpec((1, w), lambda i: (0, 0))
    return _pallas_call(
        body, name=name, grid=(nb,),
        in_specs=[half(0), half(1), vec, half(0), half(0)],
        out_specs=[pl.BlockSpec((tb, 2 * w), lambda i: (i, 0)), vec, vec],
        out_shape=[jax.ShapeDtypeStruct((t, 2 * w), BF16), jax.ShapeDtypeStruct((1, w), F32),
                   jax.ShapeDtypeStruct((1, w), F32)],
        scratch_shapes=[pltpu.VMEM((tb, w), F32), pltpu.VMEM((tb, w), F32), pltpu.VMEM((8, w), F32),
                        pltpu.VMEM((8, w), F32)],
        compiler_params=_params("arbitrary"),
    )(p, p, ln_w, du, dvn)


def _make_seg_gates(n_heads):
    def seg(raw, a_log, dt_bias):
        lane = lax.broadcasted_iota(jnp.int32, raw.shape, 1)
        beta = jax.nn.sigmoid(raw)
        g = -jnp.exp(a_log) * jax.nn.softplus(raw + dt_bias)
        return (jnp.where(lane < n_heads, beta, jnp.where(lane < 2 * n_heads, g, 0.0)),)
    return seg


def _gate_fn(u, v, ws, bs):
    rows, gd = u.shape
    nb = rows // SG_CHUNK
    r = lax.broadcasted_iota(jnp.int32, (SG_CHUNK, SG_CHUNK), 0)
    c = lax.broadcasted_iota(jnp.int32, (SG_CHUNK, SG_CHUNK), 1)
    w = jnp.broadcast_to(jnp.where(r >= c, ws[0], 0.0).astype(BF16), (nb, SG_CHUNK, SG_CHUNK))
    v3 = v.astype(BF16).reshape(nb, SG_CHUNK, gd)
    s = lax.dot_general(w, v3, (((2,), (1,)), ((0,), (0,))), preferred_element_type=F32) + bs[0]
    return u.astype(F32) * s.reshape(rows, gd)


def _sg_rows(t):
    rows = SG_CHUNK * SG_STEP_CHUNKS
    while t % rows:
        rows -= SG_CHUNK
    return rows


def _sg_specs(rows, gd):
    x_spec = pl.BlockSpec((rows, gd), lambda g, i: (i, g))
    ws_spec = pl.BlockSpec((1, SG_CHUNK, SG_CHUNK), lambda g, i: (g, 0, 0))
    bs_spec = pl.BlockSpec((1, SG_CHUNK, 1), lambda g, i: (g, 0, 0))
    return x_spec, ws_spec, bs_spec


def _spatial_gate(u, v, ws, bs, name):
    t, w = u.shape
    gd = w // SG_GROUPS
    rows = _sg_rows(t)
    x_spec, ws_spec, bs_spec = _sg_specs(rows, gd)

    def body(u_ref, v_ref, ws_ref, bs_ref, o_ref):
        o_ref[...] = _gate_fn(u_ref[...], v_ref[...], ws_ref[...], bs_ref[...]).astype(o_ref.dtype)

    return _pallas_call(
        body, name=name, grid=(SG_GROUPS, t // rows),
        in_specs=[x_spec, x_spec, ws_spec, bs_spec], out_specs=x_spec,
        out_shape=jax.ShapeDtypeStruct((t, w), BF16),
        compiler_params=_params("parallel", "parallel"),
    )(u, v, ws, bs)


def _spatial_gate_vjp(u, v, ws, bs, ct, name, comm=()):
    t, w = u.shape
    gd = w // SG_GROUPS
    rows = _sg_rows(t)
    nchunks = t // rows
    nx = len(comm)
    x_spec, ws_spec, bs_spec = _sg_specs(rows, gd)

    def body(*refs):
        u_ref, v_ref, ws_ref, bs_ref, ct_ref = refs[:5]
        src_refs = refs[5:5 + nx]
        du_ref, dv_ref, dws_ref, dbs_ref = refs[5 + nx:9 + nx]
        dst_refs, sems = refs[9 + nx:9 + 2 * nx], refs[9 + 2 * nx:]
        g, i = pl.program_id(0), pl.program_id(1)

        def compute():
            _, vjp = jax.vjp(_gate_fn, u_ref[...], v_ref[...].astype(F32), ws_ref[...], bs_ref[...])
            du, dv, dws, dbs = vjp(ct_ref[...])
            du_ref[...] = du
            dv_ref[...] = dv

            @pl.when(i == 0)
            def _():
                dws_ref[...] = jnp.zeros_like(dws_ref)
                dbs_ref[...] = jnp.zeros_like(dbs_ref)

            dws_ref[...] += dws
            dbs_ref[...] += dbs

        step = g * nchunks + i
        _carry(comm, src_refs, dst_refs, sems, step == 0, step == (SG_GROUPS * nchunks) // 2,
               step == SG_GROUPS * nchunks - 1, compute)

    return _pallas_call(
        body, name=name, grid=(SG_GROUPS, nchunks),
        in_specs=[x_spec, x_spec, ws_spec, bs_spec, x_spec] + [_ANY] * nx,
        out_specs=[x_spec, x_spec, ws_spec, bs_spec] + [_ANY] * nx,
        out_shape=[jax.ShapeDtypeStruct((t, w), F32), jax.ShapeDtypeStruct((t, w), F32),
                   jax.ShapeDtypeStruct(ws.shape, F32), jax.ShapeDtypeStruct(bs.shape, F32)] + [ex.dst for ex in comm],
        scratch_shapes=_EXCHANGE_SEMS * nx,
        compiler_params=_params("arbitrary", "arbitrary"),
    )(u, v, ws, bs, ct, *[ex.src for ex in comm])


def _conv_silu(proj, w, width, name):
    t = proj.shape[0]
    tb = _row_tile(t, 128)
    hb = tb // HALO

    def body(prev_ref, x_ref, w_ref, o_ref, ext):
        keep = (pl.program_id(0) > 0).astype(F32)
        ext[0:HALO, :] = prev_ref[...].astype(F32) * keep
        ext[HALO:HALO + tb, :] = x_ref[...].astype(F32)
        for sl in [slice(c0, c0 + LANES) for c0 in range(0, width, LANES)]:
            acc = jnp.zeros((tb, LANES), F32)
            for j in range(CONV_WIDTH):
                o = HALO - (CONV_WIDTH - 1) + j
                acc = acc + w_ref[j:j + 1, sl] * ext[o:o + tb, sl]
            o_ref[:, sl] = jax.nn.silu(acc).astype(o_ref.dtype)

    return _pallas_call(
        body, name=name, grid=(t // tb,),
        in_specs=[pl.BlockSpec((HALO, width), lambda i: (jnp.maximum(i * hb - 1, 0), 0)),
                  pl.BlockSpec((tb, width), lambda i: (i, 0)),
                  pl.BlockSpec((CONV_WIDTH, width), lambda i: (0, 0))],
        out_specs=pl.BlockSpec((tb, width), lambda i: (i, 0)),
        out_shape=jax.ShapeDtypeStruct((t, width), F32),
        scratch_shapes=[pltpu.VMEM((tb + HALO, width), F32)],
        compiler_params=_params("parallel"),
    )(proj, proj, w)


def _conv_silu_vjp(proj, w, dy, width, name):
    t = proj.shape[0]
    tb = _row_tile(t, 128)
    hb = tb // HALO
    nb = t // tb
    te = tb + HALO

    def body(prev_ref, x_ref, next_ref, w_ref, dy_ref, dyn_ref, dx_ref, dw_ref, ext, dpre):
        i = pl.program_id(0)
        first = (i > 0).astype(F32)
        last = (i < nb - 1).astype(F32)
        ext[0:HALO, :] = prev_ref[...].astype(F32) * first
        ext[HALO:HALO + tb, :] = x_ref[...].astype(F32)
        ext[HALO + tb:, :] = next_ref[...].astype(F32) * last
        @pl.when(i == 0)
        def _():
            dw_ref[...] = jnp.zeros_like(dw_ref)

        for sl in [slice(c0, c0 + LANES) for c0 in range(0, width, LANES)]:
            pre = jnp.zeros((te, LANES), F32)
            for j in range(CONV_WIDTH):
                o = HALO - (CONV_WIDTH - 1) + j
                pre = pre + w_ref[j:j + 1, sl] * ext[o:o + te, sl]
            sig = jax.nn.sigmoid(pre)
            dsilu = sig * (1.0 + pre * (1.0 - sig))
            own = dy_ref[:, sl].astype(F32) * dsilu[0:tb, :]
            dpre[0:tb, sl] = own
            dpre[tb:, sl] = dyn_ref[:, sl].astype(F32) * last * dsilu[tb:, :]
            dx = jnp.zeros((tb, LANES), F32)
            for j in range(CONV_WIDTH):
                o = CONV_WIDTH - 1 - j
                dx = dx + w_ref[j:j + 1, sl] * dpre[o:o + tb, sl]
            dx_ref[:, sl] = dx.astype(dx_ref.dtype)
            for j in range(CONV_WIDTH):
                o = HALO - (CONV_WIDTH - 1) + j
                dw_ref[j:j + 1, sl] += jnp.sum(own * ext[o:o + tb, sl], axis=0, keepdims=True)

    halo_prev = lambda i: (jnp.maximum(i * hb - 1, 0), 0)
    halo_next = lambda i: (jnp.minimum((i + 1) * hb, t // HALO - 1), 0)
    return _pallas_call(
        body, name=name, grid=(nb,),
        in_specs=[pl.BlockSpec((HALO, width), halo_prev),
                  pl.BlockSpec((tb, width), lambda i: (i, 0)),
                  pl.BlockSpec((HALO, width), halo_next),
                  pl.BlockSpec((CONV_WIDTH, width), lambda i: (0, 0)),
                  pl.BlockSpec((tb, width), lambda i: (i, 0)),
                  pl.BlockSpec((HALO, width), halo_next)],
        out_specs=[pl.BlockSpec((tb, width), lambda i: (i, 0)),
                   pl.BlockSpec((CONV_WIDTH, width), lambda i: (0, 0))],
        out_shape=[jax.ShapeDtypeStruct((t, width), BF16), jax.ShapeDtypeStruct((CONV_WIDTH, width), F32)],
        scratch_shapes=[pltpu.VMEM((tb + 2 * HALO, width), F32), pltpu.VMEM((te, width), F32)],
        compiler_params=_params("arbitrary"),
    )(proj, proj, proj, w, dy, dy)


def _bdot(a, b, ca, cb):
    return lax.dot_general(a.astype(BF16), b.astype(BF16), (((ca,), (cb,)), ((0,), (0,))),
                           preferred_element_type=F32)


def _bdot3(a, b, ca, cb):
    a_hi, b_hi = a.astype(BF16), b.astype(BF16)
    a_lo, b_lo = (a - a_hi.astype(F32)).astype(BF16), (b - b_hi.astype(F32)).astype(BF16)
    dot = lambda p, q: lax.dot_general(p, q, (((ca,), (cb,)), ((0,), (0,))), preferred_element_type=F32)
    return dot(a_hi, b_hi) + (dot(a_hi, b_lo) + dot(a_lo, b_hi))


def _unit_lower_inverse(a):
    cs = a.shape[-1]
    r = lax.broadcasted_iota(jnp.int32, (1, cs, cs), 1)
    c = lax.broadcasted_iota(jnp.int32, (1, cs, cs), 2)
    eye = (r == c).astype(F32)
    diag = jnp.where((r // INV_BLOCK) == (c // INV_BLOCK), a, 0.0)

    def nilpotent_inverse(x, order):
        inv, p = eye - x, x
        for _ in range(int(math.log2(order)) - 1):
            p = _bdot3(p, p, 2, 1)
            inv = inv + _bdot3(inv, p, 2, 1)
        return inv

    t_diag = nilpotent_inverse(diag, INV_BLOCK)
    return _bdot3(nilpotent_inverse(_bdot3(t_diag, a - diag, 2, 1), cs // INV_BLOCK), t_diag, 2, 1)


@jax.custom_vjp
def _inverse_given(a, t):
    return t


_inverse_given.defvjp(lambda a, t: (t, t),
                      lambda t, g: (-_bdot3(_bdot3(t, g, 1, 1), t, 2, 2), jnp.zeros_like(t)))


def _bmm(a, b):
    return _bdot(a, b, 2, 1)


def _bmm_nt(a, b):
    return _bdot(a, b, 2, 2)


def _bmm_tn(a, b):
    return _bdot(a, b, 1, 1)


def _head_rms(o):
    return o * lax.rsqrt(jnp.mean(o * o, axis=-1, keepdims=True) + EPS)


def _l2norm(x):
    return x * lax.rsqrt(jnp.sum(x * x, axis=-1, keepdims=True) + 1e-6)


def _gdn_chunk(state, qc, kc, vc, z, beta_r, g_r, onw, t_saved=None):
    cs = LA_CHUNK
    r = lax.broadcasted_iota(jnp.int32, (1, cs, cs), 1)
    c = lax.broadcasted_iota(jnp.int32, (1, cs, cs), 2)
    eye = (r == c).astype(F32)
    causal, strict = r >= c, r > c
    q = _l2norm(qc.astype(F32)) * (HEAD_DIM ** -0.5)
    k = _l2norm(kc.astype(F32))
    v = vc.astype(F32)
    g_col = jnp.sum(eye * g_r, axis=-1, keepdims=True)
    beta_col = jnp.sum(eye * beta_r, axis=-1, keepdims=True)
    gc_col = jnp.sum(causal.astype(F32) * g_r, axis=-1, keepdims=True)
    gc_row = jnp.sum((r <= c).astype(F32) * g_col, axis=-2, keepdims=True)
    gc_last = jnp.sum(g_r, axis=-1, keepdims=True)
    decay = jnp.where(causal, jnp.exp(jnp.where(causal, gc_col - gc_row, 0.0)), 0.0)
    kb = k * beta_col
    a = jnp.where(strict, _bmm_nt(kb, k) * decay, 0.0)
    t_inv = _unit_lower_inverse(a) if t_saved is None else _inverse_given(a, t_saved)
    eg = jnp.exp(gc_col)
    u = _bmm(t_inv, v * beta_col)
    w = _bmm(t_inv, kb * eg)
    qk = jnp.where(causal, _bmm_nt(q, k) * decay, 0.0)
    v_new = u - _bmm(w, state)
    o = _bmm(q * eg, state) + _bmm(qk, v_new)
    new_state = state * jnp.exp(gc_last) + _bmm_tn(k * jnp.exp(gc_last - gc_col), v_new)
    o = _head_rms(o) * onw * jax.nn.silu(z.astype(F32))
    return (new_state, o, t_inv) if t_saved is None else (new_state, o)


@jax.custom_vjp
def _swap_halves(x):
    return pltpu.roll(x, HEAD_DIM // 2, x.ndim - 1)


_swap_halves.defvjp(lambda x: (_swap_halves(x), None), lambda _, g: (_swap_halves(g),))


def _make_ret_chunk(cos2, sin2, d_mat, dec_q, dec_k, dec_c):
    def chunk(state, rq, rk, rv, rg):
        qf, kf = rq.astype(F32), rk.astype(F32)
        q = qf * cos2 + _swap_halves(qf) * sin2
        k = (kf * cos2 + _swap_halves(kf) * sin2) * (HEAD_DIM ** -0.5)
        v = rv.astype(F32)
        inner = _bmm(_bmm_nt(q, k) * d_mat, v)
        cross = _bmm(q * dec_q, state)
        new_state = state * dec_c + _bmm_tn(k * dec_k, v)
        return new_state, jax.nn.silu(rg.astype(F32)) * _head_rms(inner + cross)
    return chunk


def _split_heads(x, n_heads):
    return jnp.stack([x[:, h * HEAD_DIM:(h + 1) * HEAD_DIM] for h in range(n_heads)], axis=0)


def _store_heads(ref, col0, val):
    for h in range(val.shape[0]):
        ref[:, col0 + h * HEAD_DIM:col0 + (h + 1) * HEAD_DIM] = val[h].astype(ref.dtype)


def _scan_fwd(make_chunk, seqs, rows, consts, n_heads, name, comm=(), saved_shape=None):
    cs, hk = LA_CHUNK, n_heads * HEAD_DIM
    t = seqs[0][0].shape[0]
    n = t // cs
    ns, nr, nc, nx = len(seqs), len(rows), len(consts), len(comm)
    n_out = 2 if saved_shape is None else 3

    def body(*refs):
        seq_refs, row_refs, const_refs = refs[:ns], refs[ns:ns + nr], refs[ns + nr:ns + nr + nc]
        o = ns + nr + nc
        src_refs = refs[o:o + nx]
        out_refs = refs[o + nx:o + nx + n_out]
        dst_refs = refs[o + nx + n_out:o + 2 * nx + n_out]
        state, sems = refs[o + 2 * nx + n_out], refs[o + 2 * nx + n_out + 1:]
        i = pl.program_id(0)

        def compute():
            @pl.when(i == 0)
            def _():
                state[...] = jnp.zeros_like(state)

            out_refs[1][0] = state[...]
            chunk = make_chunk([r[0] for r in row_refs], [r[...] for r in const_refs])
            new_state, out, *saved = chunk(state[...], *[_split_heads(r[...], n_heads) for r in seq_refs])
            state[...] = new_state
            _store_heads(out_refs[0], 0, out)
            if saved_shape is not None:
                out_refs[2][0] = saved[0]

        _carry(comm, src_refs, dst_refs, sems, i == 0, i == n // 2, i == n - 1, compute)

    chunk_spec = lambda shape: pl.BlockSpec((1,) + tuple(shape), lambda i, nd=len(shape): (i,) + (0,) * nd)
    saved = [] if saved_shape is None else [tuple(saved_shape)]
    return _pallas_call(
        body, name=name, grid=(n,),
        in_specs=[pl.BlockSpec((cs, hk), lambda i, b=b: (i, b)) for _, b in seqs]
        + [chunk_spec(a.shape[1:]) for a in rows] + [_full_spec(a) for a in consts] + [_ANY] * nx,
        out_specs=[pl.BlockSpec((cs, hk), lambda i: (i, 0)), chunk_spec((n_heads, HEAD_DIM, HEAD_DIM))]
        + [chunk_spec(s) for s in saved] + [_ANY] * nx,
        out_shape=[jax.ShapeDtypeStruct((t, hk), BF16),
                   jax.ShapeDtypeStruct((n, n_heads, HEAD_DIM, HEAD_DIM), F32)]
        + [jax.ShapeDtypeStruct((n,) + s, F32) for s in saved] + [ex.dst for ex in comm],
        scratch_shapes=[pltpu.VMEM((n_heads, HEAD_DIM, HEAD_DIM), F32)] + _EXCHANGE_SEMS * nx,
        compiler_params=_params("arbitrary"),
    )(*[a for a, _ in seqs], *rows, *consts, *[ex.src for ex in comm])


def _scan_bwd(make_chunk, seqs, rows, consts, states, d_out, n_heads, n_row_grads, n_const_grads, dseq_dtype,
              name, comm=()):
    cs, hk = LA_CHUNK, n_heads * HEAD_DIM
    t = seqs[0][0].shape[0]
    n = t // cs
    ns, nr, nc, nx = len(seqs), len(rows), len(consts), len(comm)
    n_grads = 1 + n_row_grads + n_const_grads

    def body(*refs):
        seq_refs, row_refs, const_refs = refs[:ns], refs[ns:ns + nr], refs[ns + nr:ns + nr + nc]
        st_ref, do_ref = refs[ns + nr + nc:ns + nr + nc + 2]
        o = ns + nr + nc + 2
        src_refs = refs[o:o + nx]
        o += nx
        dseq_ref = refs[o]
        drow_refs = refs[o + 1:o + 1 + n_row_grads]
        dconst_refs = refs[o + 1 + n_row_grads:o + n_grads]
        dst_refs = refs[o + n_grads:o + n_grads + nx]
        d_state, sems = refs[o + n_grads + nx], refs[o + n_grads + nx + 1:]
        i = pl.program_id(0)

        def compute():
            @pl.when(i == 0)
            def _():
                d_state[...] = jnp.zeros_like(d_state)
                for ref in dconst_refs:
                    ref[...] = jnp.zeros_like(ref)

            row_vals = [r[0] for r in row_refs]
            const_vals = [r[...] for r in const_refs]

            def fn(state, seq_vals, row_d, const_d):
                chunk = make_chunk(list(row_d) + row_vals[n_row_grads:], list(const_d) + const_vals[n_const_grads:])
                return chunk(state, *seq_vals)

            _, vjp = jax.vjp(fn, st_ref[0], tuple(_split_heads(r[...], n_heads) for r in seq_refs),
                             tuple(row_vals[:n_row_grads]), tuple(const_vals[:n_const_grads]))
            ds, dseq, drow, dconst = vjp((d_state[...], _split_heads(do_ref[...], n_heads).astype(F32)))
            d_state[...] = ds
            for j, g in enumerate(dseq):
                _store_heads(dseq_ref, j * hk, g)
            for ref, g in zip(drow_refs, drow):
                ref[0] = g
            for ref, g in zip(dconst_refs, dconst):
                ref[...] += g

        _carry(comm, src_refs, dst_refs, sems, i == 0, i == n // 2, i == n - 1, compute)

    rev = lambda i: n - 1 - i
    row_spec = lambda a: pl.BlockSpec((1,) + a.shape[1:], lambda i, nd=a.ndim: (rev(i),) + (0,) * (nd - 1))
    res = _pallas_call(
        body, name=name, grid=(n,),
        in_specs=[pl.BlockSpec((cs, hk), lambda i, b=b: (rev(i), b)) for _, b in seqs]
        + [row_spec(a) for a in rows] + [_full_spec(a) for a in consts]
        + [pl.BlockSpec((1, n_heads, HEAD_DIM, HEAD_DIM), lambda i: (rev(i), 0, 0, 0)),
           pl.BlockSpec((cs, hk), lambda i, b=d_out[1]: (rev(i), b))] + [_ANY] * nx,
        out_specs=[pl.BlockSpec((cs, ns * hk), lambda i: (rev(i), 0))]
        + [row_spec(a) for a in rows[:n_row_grads]] + [_full_spec(a) for a in consts[:n_const_grads]] + [_ANY] * nx,
        out_shape=[jax.ShapeDtypeStruct((t, ns * hk), dseq_dtype)]
        + [jax.ShapeDtypeStruct(a.shape, F32) for a in rows[:n_row_grads]]
        + [jax.ShapeDtypeStruct(a.shape, F32) for a in consts[:n_const_grads]] + [ex.dst for ex in comm],
        scratch_shapes=[pltpu.VMEM((n_heads, HEAD_DIM, HEAD_DIM), F32)] + _EXCHANGE_SEMS * nx,
        compiler_params=_params("arbitrary"),
    )(*[a for a, _ in seqs], *rows, *consts, states, d_out[0], *[ex.src for ex in comm])
    return (res[0], list(res[1:1 + n_row_grads]), list(res[1 + n_row_grads:n_grads]), list(res[n_grads:]))


def _gdn_make_chunk(row_vals, const_vals):
    beta_r, g_r, *saved = row_vals
    (onw,) = const_vals
    t_saved = saved[0] if saved else None
    return lambda state, qc, kc, vc, z: _gdn_chunk(state, qc, kc, vc, z, beta_r, g_r, onw, t_saved)


def _ret_make_chunk(row_vals, const_vals):
    cos2, sin2 = row_vals
    d_mat, dec_q, dec_k, dec_c = const_vals
    return _make_ret_chunk(cos2, sin2, d_mat, dec_q, dec_k, dec_c)


def _adamw_math(w, g, m, v):
    m = ADAM_B1 * m + (1.0 - ADAM_B1) * g
    v = ADAM_B2 * v + (1.0 - ADAM_B2) * jnp.square(g)
    m_hat = m / (1.0 - ADAM_B1 ** ADAM_STEP)
    v_hat = v / (1.0 - ADAM_B2 ** ADAM_STEP)
    delta = -ADAM_LR * (m_hat / (jnp.sqrt(v_hat) + ADAM_EPS) + ADAM_WD * w)
    return delta, m, v


def _sum_parts(parts, name):
    n_parts, rows, cols = parts.shape
    tb = _row_tile(rows, 512)

    def body(p_ref, o_ref):
        g = p_ref[0].astype(F32)
        for s in range(1, n_parts):
            g = g + p_ref[s].astype(F32)
        o_ref[...] = g

    return _pallas_call(
        body, name=name, grid=(rows // tb,),
        in_specs=[pl.BlockSpec((n_parts, tb, cols), lambda i: (0, i, 0))],
        out_specs=pl.BlockSpec((tb, cols), lambda i: (i, 0)),
        out_shape=jax.ShapeDtypeStruct((rows, cols), F32),
        compiler_params=_params("parallel"),
    )(parts)


def _adamw(parts, w, m, v, name):
    rows, cols = w.shape
    n_parts = parts.shape[0]
    tb = _row_tile(rows, max(8, (1 << 17) // cols // 8 * 8))

    def body(p_ref, w_ref, m_ref, v_ref, g_out, d_out, m_out, v_out):
        g = p_ref[0].astype(F32)
        for s in range(1, n_parts):
            g = g + p_ref[s].astype(F32)
        delta, m_new, v_new = _adamw_math(w_ref[...], g, m_ref[...], v_ref[...])
        g_out[...] = g
        d_out[...] = delta
        m_out[...] = m_new
        v_out[...] = v_new

    spec = pl.BlockSpec((tb, cols), lambda i: (i, 0))
    return _pallas_call(
        body, name=name, grid=(rows // tb,),
        in_specs=[pl.BlockSpec((n_parts, tb, cols), lambda i: (0, i, 0)), spec, spec, spec],
        out_specs=[spec] * 4, out_shape=[jax.ShapeDtypeStruct((rows, cols), F32)] * 4,
        compiler_params=_params("parallel"),
    )(parts, w, m, v)


def _pack(arrays, rows_multiple=8):
    flat = jnp.concatenate([a.reshape(-1).astype(F32) for a in arrays])
    n = flat.shape[0]
    rows = -(-n // LANES)
    rows = -(-rows // rows_multiple) * rows_multiple
    return jnp.pad(flat, (0, rows * LANES - n)).reshape(rows, LANES)


def _unpack(packed, shapes):
    flat = packed.reshape(-1)
    out, o = [], 0
    for s in shapes:
        n = int(np.prod(s))
        out.append(flat[o:o + n].reshape(s))
        o += n
    return out


def _gather(shard):
    return _Exchange("gather", shard.astype(BF16))


def _rows_of(gathered):
    return gathered.reshape(gathered.shape[0] * gathered.shape[1], gathered.shape[2])


def _scatter_rows(full):
    k, n = full.shape
    return _Exchange("scatter", full.reshape(N_DEV, k // N_DEV, n))


def _relu2_epilogue(u):
    return u, jnp.square(jax.nn.relu(u))


def _relu2_vjp_epilogue(dr, u):
    return (dr * (2.0 * jax.nn.relu(u.astype(F32))),)


def kernel(x, norm_w, la_w_in, la_conv_w, la_a_log, la_dt_bias, la_out_norm_w, la_w_out, sg_w_in, sg_ln_w, sg_ln_b, sg_w_s, sg_b_s, sg_w_out, ffn_w_up, ffn_w_down, loss_target, m_norm_w, m_la_w_in, m_la_conv_w, m_la_a_log, m_la_dt_bias, m_la_out_norm_w, m_la_w_out, m_sg_w_in, m_sg_ln_w, m_sg_ln_b, m_sg_w_s, m_sg_b_s, m_sg_w_out, m_ffn_w_up, m_ffn_w_down, v_norm_w, v_la_w_in, v_la_conv_w, v_la_a_log, v_la_dt_bias, v_la_out_norm_w, v_la_w_out, v_sg_w_in, v_sg_ln_w, v_sg_ln_b, v_sg_w_s, v_sg_b_s, v_sg_w_out, v_ffn_w_up, v_ffn_w_down):
    weights = dict(norm_w=norm_w, la_w_in=la_w_in, la_conv_w=la_conv_w, la_a_log=la_a_log, la_dt_bias=la_dt_bias,
                   la_out_norm_w=la_out_norm_w, la_w_out=la_w_out, sg_w_in=sg_w_in, sg_ln_w=sg_ln_w, sg_ln_b=sg_ln_b,
                   sg_w_s=sg_w_s, sg_b_s=sg_b_s, sg_w_out=sg_w_out, ffn_w_up=ffn_w_up, ffn_w_down=ffn_w_down)
    mom1 = dict(norm_w=m_norm_w, la_w_in=m_la_w_in, la_conv_w=m_la_conv_w, la_a_log=m_la_a_log, la_dt_bias=m_la_dt_bias,
                la_out_norm_w=m_la_out_norm_w, la_w_out=m_la_w_out, sg_w_in=m_sg_w_in, sg_ln_w=m_sg_ln_w, sg_ln_b=m_sg_ln_b,
                sg_w_s=m_sg_w_s, sg_b_s=m_sg_b_s, sg_w_out=m_sg_w_out, ffn_w_up=m_ffn_w_up, ffn_w_down=m_ffn_w_down)
    mom2 = dict(norm_w=v_norm_w, la_w_in=v_la_w_in, la_conv_w=v_la_conv_w, la_a_log=v_la_a_log, la_dt_bias=v_la_dt_bias,
                la_out_norm_w=v_la_out_norm_w, la_w_out=v_la_w_out, sg_w_in=v_sg_w_in, sg_ln_w=v_sg_ln_w, sg_ln_b=v_sg_ln_b,
                sg_w_s=v_sg_w_s, sg_b_s=v_sg_b_s, sg_w_out=v_sg_w_out, ffn_w_up=v_ffn_w_up, ffn_w_down=v_ffn_w_down)
    names = list(weights)

    t, d = x.shape[1], x.shape[2]
    n_heads = la_a_log.shape[-1]
    hk = n_heads * HEAD_DIM
    cs = LA_CHUNK
    n_chunks = t // cs
    sg_width = sg_w_out.shape[1] * N_DEV
    me = 4 * lax.axis_index("x") + 2 * lax.axis_index("y") + lax.axis_index("c")
    xs = x[0]
    tgt = loss_target[0]

    w_in_g = _all_gather(la_w_in[0].astype(BF16), "ag_la_w_in")
    w_in_full = jnp.concatenate([w_in_g[j] for j in range(N_DEV)], axis=1)
    w_main = jnp.concatenate([w_in_full[:, :4 * hk], w_in_full[:, 4 * hk + 2 * n_heads:]], axis=1)
    w_gate = jnp.pad(w_in_full[:, 4 * hk:4 * hk + 2 * n_heads], ((0, 0), (0, LANES - 2 * n_heads)))
    conv_t = lambda a: jnp.transpose(a[0])
    small_local = lambda src: [src["norm_w"], conv_t(src["la_conv_w"]), src["sg_ln_w"], src["sg_ln_b"]]
    small_all = _all_gather(_pack(small_local(weights)), "ag_small")
    dsh = d // N_DEV
    csh = la_conv_w.shape[1]
    wsh = sg_width // N_DEV
    nw_parts, conv_parts, lnw_parts, lnb_parts = zip(*[
        _unpack(small_all[j], [(2, 4, dsh), (CONV_WIDTH, csh), (wsh,), (wsh,)]) for j in range(N_DEV)])
    nw = jnp.concatenate(nw_parts, axis=-1)
    conv_w = jnp.concatenate(conv_parts, axis=1)
    ln_w = jnp.concatenate(lnw_parts)[None, :]
    ln_b = jnp.concatenate(lnb_parts)[None, :]
    nrm = lambda l, j: nw[l, j][None, :]

    tb_d = _row_tile(t, 256)
    tb_wide = _row_tile(t, 64)

    (y0,) = _rowwise(_seg_norm, [xs], [nrm(0, 0)], [(d, BF16)], tb_d, "f_norm0")
    proj, w_up0_g = _matmul(y0, w_main, "nn", F32, "f_proj",
                            comm=[_gather(ffn_w_up[0])])
    gate_raw = _matmul(y0, w_gate, "nn", F32, "f_gate")
    a_log_v = jnp.pad(la_a_log.reshape(1, n_heads), ((0, 0), (n_heads, LANES - 2 * n_heads)))
    dt_bias_v = jnp.pad(la_dt_bias.reshape(1, n_heads), ((0, 0), (n_heads, LANES - 2 * n_heads)))
    seg_gates = _make_seg_gates(n_heads)
    (bg,) = _rowwise(seg_gates, [gate_raw], [a_log_v, dt_bias_v], [(LANES, F32)], tb_d, "f_gates")
    to_rows = lambda a: jnp.transpose(a.reshape(n_chunks, cs, n_heads), (0, 2, 1))[:, :, None, :]
    beta_r, g_r = to_rows(bg[:, :n_heads]), to_rows(bg[:, n_heads:2 * n_heads])
    qkvc = _conv_silu(proj, conv_w, 3 * hk, "f_conv")
    onw = la_out_norm_w.reshape(1, 1, HEAD_DIM)
    gdn_seqs = [(qkvc, 0), (qkvc, 1), (qkvc, 2), (proj, 3)]
    o_a, gdn_states, gdn_t_inv, w_down0_g = _scan_fwd(
        _gdn_make_chunk, gdn_seqs, [beta_r, g_r], [onw], n_heads, "f_gdn", comm=[_gather(ffn_w_down[0])],
        saved_shape=(n_heads, cs, cs))

    pos = jnp.arange(t, dtype=F32)
    inv_freq = 1.0 / (ROPE_BASE ** jnp.linspace(0.0, 1.0, HEAD_DIM // 2, dtype=F32))
    ang = pos[:, None] * inv_freq[None, :]
    cos2 = jnp.concatenate([jnp.cos(ang), jnp.cos(ang)], axis=-1).reshape(n_chunks, cs, HEAD_DIM)
    sin2 = jnp.concatenate([-jnp.sin(ang), jnp.sin(ang)], axis=-1).reshape(n_chunks, cs, HEAD_DIM)
    log_gamma = jnp.log1p(-jnp.power(2.0, -5.0 - jnp.arange(n_heads, dtype=F32)))
    cpos = jnp.arange(cs, dtype=F32)
    rel = cpos[:, None] - cpos[None, :]
    d_mat = jnp.where(rel >= 0, jnp.exp(jnp.where(rel >= 0, rel, 0.0) * log_gamma[:, None, None]), 0.0)
    dec_q = jnp.exp((cpos + 1.0) * log_gamma[:, None])[..., None]
    dec_k = jnp.exp((cs - 1.0 - cpos) * log_gamma[:, None])[..., None]
    dec_c = jnp.exp(cs * log_gamma)[:, None, None]
    ret_seqs = [(proj, 4), (proj, 5), (proj, 6), (proj, 7)]
    ret_consts = [d_mat, dec_q, dec_k, dec_c]
    o_b, ret_states, w_la_out_g = _scan_fwd(_ret_make_chunk, ret_seqs, [cos2, sin2], ret_consts, n_heads, "f_ret",
                                            comm=[_gather(la_w_out[0])])
    o_mix = jnp.concatenate([o_a, o_b], axis=1)
    y1 = _matmul(o_mix, _rows_of(w_la_out_g), "nn", F32, "f_la_out")
    h1, a1 = _rowwise(_seg_residual, [xs, y1], [nrm(0, 1), nrm(0, 2)], [(d, F32), (d, BF16)], tb_d, "f_res0a")
    u1, r1, w_sg_in_g = _matmul(a1, w_up0_g, "nn", None, "f_up0", b_blocked=True, epilogue=_relu2_epilogue,
                                out_dtypes=[F32, BF16], comm=[_gather(sg_w_in[0])])
    y2, w_sg_out_g = _matmul(r1, _rows_of(w_down0_g), "nn", F32, "f_down0", comm=[_gather(sg_w_out[0])])
    h2, a2 = _rowwise(_seg_residual, [h1, y2], [nrm(0, 3), nrm(1, 0)], [(d, F32), (d, BF16)], tb_d, "f_res0b")

    p_sg, w_up1_g = _matmul(a2, w_sg_in_g, "nn", F32, "f_sg_in", b_blocked=True,
                            comm=[_gather(ffn_w_up[1])])
    ua, vn = _gelu_ln(p_sg, ln_w, ln_b, tb_wide, "f_gelu_ln")
    ws = sg_w_s[0]
    bs = sg_b_s[0][:, :, None]
    gated = _spatial_gate(ua, vn, ws, bs, "f_sgate")
    y3 = _matmul(gated, _rows_of(w_sg_out_g), "nn", F32, "f_sg_out")
    h3, a3 = _rowwise(_seg_residual, [h2, y3], [nrm(1, 1), nrm(1, 2)], [(d, F32), (d, BF16)], tb_d, "f_res1a")
    u2, r2, w_down1_g = _matmul(a3, w_up1_g, "nn", None, "f_up1", b_blocked=True, epilogue=_relu2_epilogue,
                                out_dtypes=[F32, BF16], comm=[_gather(ffn_w_down[1])])
    y4 = _matmul(r2, _rows_of(w_down1_g), "nn", F32, "f_down1")

    ones = jnp.ones((t, 1), F32)
    (dh3, dy4), (dnw13,), loss_rows = _rowwise_vjp(
        _seg_loss, [h3, y4, tgt], [nrm(1, 3)], [ones], [F32, BF16, None], tb_d, "b_loss", primal_out=0)
    loss = lax.psum(jnp.sum(loss_rows), ("x", "y", "c"))

    du2 = _matmul(dy4, _rows_of(w_down1_g), "nt", BF16, "b_down1", extra=[u2], epilogue=_relu2_vjp_epilogue)
    dw_down1 = _matmul(r2, dy4, "tn", BF16, "b_dw_down1")
    da3, parts_down1 = _matmul(du2, w_up1_g, "nt", F32, "b_up1", b_blocked=True, comm=[_scatter_rows(dw_down1)])
    dw_up1 = _matmul(a3, du2, "tn", BF16, "b_dw_up1", out_blocked=True)
    (dh2, dy3), (dnw11, dnw12) = _rowwise_vjp(
        _seg_residual, [h2, y3], [nrm(1, 1), nrm(1, 2)], [dh3, da3], [F32, BF16], tb_d, "b_res1a")
    dgated = _matmul(dy3, _rows_of(w_sg_out_g), "nt", F32, "b_sg_out")
    dw_sg_out = _matmul(gated, dy3, "tn", BF16, "b_dw_sg_out")
    dua, dvn, dws, dbs = _spatial_gate_vjp(ua, vn, ws, bs, dgated, "b_sgate")
    dp_sg, dln_w, dln_b = _gelu_ln_vjp(p_sg, ln_w, dua, dvn, tb_wide, "b_gelu_ln")
    da2, parts_sg_out = _matmul(dp_sg, w_sg_in_g, "nt", F32, "b_sg_in", b_blocked=True,
                                comm=[_scatter_rows(dw_sg_out)])
    dw_sg_in, parts_up1 = _matmul(a2, dp_sg, "tn", BF16, "b_dw_sg_in", out_blocked=True,
                                  comm=[_Exchange("scatter", dw_up1)])
    (dh1, dy2), (dnw03, dnw10) = _rowwise_vjp(
        _seg_residual, [h1, y2], [nrm(0, 3), nrm(1, 0)], [dh2, da2], [F32, BF16], tb_d, "b_res0b")
    du1, parts_sg_in = _matmul(dy2, _rows_of(w_down0_g), "nt", BF16, "b_down0", extra=[u1],
                               epilogue=_relu2_vjp_epilogue, comm=[_Exchange("scatter", dw_sg_in)])
    dw_down0 = _matmul(r1, dy2, "tn", BF16, "b_dw_down0")
    da1, parts_down0 = _matmul(du1, w_up0_g, "nt", F32, "b_up0", b_blocked=True, comm=[_scatter_rows(dw_down0)])
    dw_up0 = _matmul(a1, du1, "tn", BF16, "b_dw_up0", out_blocked=True)
    (dx_res, dy1), (dnw01, dnw02) = _rowwise_vjp(
        _seg_residual, [xs, y1], [nrm(0, 1), nrm(0, 2)], [dh1, da1], [F32, BF16], tb_d, "b_res0a")
    do_mix = _matmul(dy1, _rows_of(w_la_out_g), "nt", F32, "b_la_out")
    dw_la_out = _matmul(o_mix, dy1, "tn", BF16, "b_dw_la_out")

    d_ret, _, _, _ = _scan_bwd(_ret_make_chunk, ret_seqs, [cos2, sin2], ret_consts, ret_states, (do_mix, 1),
                               n_heads, 0, 0, BF16, "b_ret")
    d_gdn, (dbeta_r, dg_r), (donw,), (parts_up0, parts_la_out) = _scan_bwd(
        _gdn_make_chunk, gdn_seqs, [beta_r, g_r, gdn_t_inv], [onw], gdn_states, (do_mix, 0), n_heads, 2, 1, F32, "b_gdn",
        comm=[_Exchange("scatter", dw_up0), _scatter_rows(dw_la_out)])
    from_rows = lambda a: jnp.transpose(a[:, :, 0, :], (0, 2, 1)).reshape(t, n_heads)
    dbg = jnp.pad(jnp.concatenate([from_rows(dbeta_r), from_rows(dg_r)], axis=1), ((0, 0), (0, LANES - 2 * n_heads)))
    (dgate_raw,), (da_log_v, ddt_bias_v) = _rowwise_vjp(
        seg_gates, [gate_raw], [a_log_v, dt_bias_v], [dbg], [BF16], tb_d, "b_gates")
    dqkv, dconv_w = _conv_silu_vjp(proj, conv_w, d_gdn, 3 * hk, "b_conv")
    dproj = jnp.concatenate([dqkv, d_gdn[:, 3 * hk:].astype(BF16), d_ret], axis=1)
    dw_main = _matmul(y0, dproj, "tn", BF16, "b_dw_proj")
    dw_gate = _matmul(y0, dgate_raw, "tn", BF16, "b_dw_gate")
    dw_in_full = jnp.concatenate([dw_main[:, :4 * hk], dw_gate[:, :2 * n_heads], dw_main[:, 4 * hk:]], axis=1)
    in_cols = dw_in_full.shape[1] // N_DEV
    dw_in_blocks = jnp.stack([dw_in_full[:, j * in_cols:(j + 1) * in_cols] for j in range(N_DEV)])
    dy0, parts_in = _matmul(dproj, w_main, "nt", F32, "b_proj", comm=[_Exchange("scatter", dw_in_blocks)])
    dy0_gate = _matmul(dgate_raw, w_gate, "nt", F32, "b_gate")
    (grad_x,), (dnw00,) = _rowwise_vjp(
        _seg_norm, [xs], [nrm(0, 0)], [(dy0, dy0_gate)], [F32], tb_d, "b_norm0", adds=[dx_res])

    outs = {}
    for name, parts in (("la_w_in", parts_in), ("la_w_out", parts_la_out), ("sg_w_in", parts_sg_in),
                        ("sg_w_out", parts_sg_out)):
        res = _adamw(parts, weights[name][0], mom1[name][0], mom2[name][0], f"adamw_{name}")
        outs[name] = [r[None] for r in res]
    for name, parts_l in (("ffn_w_up", [parts_up0, parts_up1]), ("ffn_w_down", [parts_down0, parts_down1])):
        res_l = [_adamw(parts_l[l], weights[name][l], mom1[name][l], mom2[name][l], f"adamw_{name}{l}")
                 for l in range(2)]
        outs[name] = [jnp.stack([res_l[0][j], res_l[1][j]]) for j in range(4)]

    dnorm = jnp.stack([jnp.concatenate([dnw00, dnw01, dnw02, dnw03], axis=0),
                       jnp.concatenate([dnw10, dnw11, dnw12, dnw13], axis=0)])
    small_grads = [dnorm, dconv_w, dln_w[0], dln_b[0],
                   da_log_v[:, n_heads:2 * n_heads], ddt_bias_v[:, n_heads:2 * n_heads],
                   donw.reshape(1, HEAD_DIM), dws[None], dbs[None, :, :, 0]]
    small_parts = _all_gather(_pack(small_grads), "ag_small_grads")
    gs = _unpack(_sum_parts(small_parts, "sum_small_grads"), [g.shape for g in small_grads])
    sharded = lambda full, axis, size: lax.dynamic_slice_in_dim(full, me * size, size, axis)
    own = [sharded(gs[0], 2, dsh), sharded(gs[1], 1, csh), sharded(gs[2], 0, wsh)[None], sharded(gs[3], 0, wsh)[None]]
    own += gs[4:]
    replicated = ["la_a_log", "la_dt_bias", "la_out_norm_w", "sg_w_s", "sg_b_s"]
    small_names = ["norm_w", "la_conv_w", "sg_ln_w", "sg_ln_b"] + replicated
    pk = lambda src: _pack(small_local(src) + [src[n] for n in replicated])
    res = _adamw(_pack(own)[None], pk(weights), pk(mom1), pk(mom2), "adamw_small")
    for j, r in enumerate(res):
        vals = _unpack(r, [g.shape for g in own])
        vals[1] = jnp.transpose(vals[1])[None]
        for n, val in zip(small_names, vals):
            outs.setdefault(n, [None] * 4)[j] = val

    grad_x = grad_x[None]
    return (loss, grad_x, *[outs[n][0] for n in names], *[outs[n][1] for n in names],
            *[outs[n][2] for n in names], *[outs[n][3] for n in names])
```

```python
import functools
import math

import jax
import jax.numpy as jnp
import numpy as np
from jax import lax
from jax.experimental import pallas as pl
from jax.experimental.pallas import tpu as pltpu

F32, BF16 = jnp.float32, jnp.bfloat16
_pallas_call = pl.pallas_call

N_DEV = 8
LANES = 128
V7X_VMEM_BYTES = 64 * 1024 * 1024
VMEM_LIMIT = (V7X_VMEM_BYTES * 3) // 4
HEAD_DIM = 128
LA_CHUNK = 64
INV_BLOCK = 8
SG_CHUNK = 128
SG_STEP_CHUNKS = 8
SG_GROUPS = 8
CONV_WIDTH = 4
HALO = 8
ROPE_BASE = 10000.0
EPS = 1e-6
ADAM_LR, ADAM_B1, ADAM_B2, ADAM_EPS, ADAM_WD, ADAM_STEP = 0.001, 0.9, 0.999, 1e-08, 0.01, 10
MESH_ID = pl.DeviceIdType.MESH


def _params(*sem):
    return pltpu.CompilerParams(dimension_semantics=sem or None, vmem_limit_bytes=VMEM_LIMIT)


def _tile(dim, pref):
    if dim <= pref:
        return dim
    t = (pref // LANES) * LANES
    while dim % t:
        t -= LANES
    return t


def _row_tile(rows, pref):
    t = min(rows, pref)
    while rows % t:
        t -= 8
    return t


def _all_gather(shard, name):
    def body(x_ref, out_ref, send_sems, recv_sems, local_sem):
        x, y, c = lax.axis_index("x"), lax.axis_index("y"), lax.axis_index("c")
        me, sibling = (x, y, c), (x, y, 1 - c)
        chips = [(1 - x, y), (x, 1 - y), (1 - x, 1 - y)]

        def rows(px, py, pc):
            return out_ref.at[4 * px + 2 * py + pc]

        def copy(k, block, to, src=None):
            return pltpu.make_async_remote_copy(
                src_ref=rows(*block) if src is None else src, dst_ref=rows(*block),
                send_sem=send_sems.at[k], recv_sem=recv_sems.at[k], device_id=to, device_id_type=MESH_ID)

        mine = pltpu.make_async_copy(x_ref, rows(*me), local_sem)
        mine.start()
        first = [copy(0, me, sibling, src=x_ref)]
        first += [copy(1 + j, me, (*chip, c), src=x_ref) for j, chip in enumerate(chips)]
        for cp in first:
            cp.start()
        passed = [copy(4 + j, (*chip, c), sibling) for j, chip in enumerate(chips)]
        for j, chip in enumerate(chips):
            copy(1 + j, (*chip, c), me).wait_recv()
            passed[j].start()
        copy(0, sibling, me).wait_recv()
        for j, chip in enumerate(chips):
            copy(4 + j, (*chip, 1 - c), me).wait_recv()
        for cp in first + passed:
            cp.wait_send()
        mine.wait()

    return _pallas_call(
        body, name=name,
        out_shape=jax.ShapeDtypeStruct((N_DEV,) + shard.shape, shard.dtype),
        in_specs=[pl.BlockSpec(memory_space=pl.ANY)],
        out_specs=pl.BlockSpec(memory_space=pl.ANY),
        scratch_shapes=[pltpu.SemaphoreType.DMA((7,)), pltpu.SemaphoreType.DMA((7,)), pltpu.SemaphoreType.DMA],
    )(shard)


class _Exchange:
    def __init__(self, kind, src):
        self.kind, self.src = kind, src
        shape = (N_DEV,) + src.shape if kind == "gather" else src.shape
        self.dst = jax.ShapeDtypeStruct(shape, src.dtype)

    def _scatter_copies(self, src_ref, dst_ref, send_sems, recv_sems, local_sem):
        x, y, c = lax.axis_index("x"), lax.axis_index("y"), lax.axis_index("c")
        me = 4 * x + 2 * y + c
        local = pltpu.make_async_copy(src_ref.at[me], dst_ref.at[me], local_sem)
        remote = []
        for k in range(1, N_DEV):
            px, py, pc = x ^ ((k >> 2) & 1), y ^ ((k >> 1) & 1), c ^ (k & 1)
            remote.append(pltpu.make_async_remote_copy(
                src_ref=src_ref.at[4 * px + 2 * py + pc], dst_ref=dst_ref.at[me],
                send_sem=send_sems.at[k - 1], recv_sem=recv_sems.at[k - 1],
                device_id=(px, py, pc), device_id_type=MESH_ID))
        return local, remote

    def _gather_copies(self, src_ref, dst_ref, send_sems, recv_sems, local_sem):
        x, y, c = lax.axis_index("x"), lax.axis_index("y"), lax.axis_index("c")
        me, sibling = (x, y, c), (x, y, 1 - c)
        chips = [(1 - x, y), (x, 1 - y), (1 - x, 1 - y)]
        rows = lambda px, py, pc: dst_ref.at[4 * px + 2 * py + pc]

        def copy(k, block, to, src=None):
            return pltpu.make_async_remote_copy(
                src_ref=rows(*block) if src is None else src, dst_ref=rows(*block),
                send_sem=send_sems.at[k], recv_sem=recv_sems.at[k], device_id=to, device_id_type=MESH_ID)

        makers = {
            "local": lambda: pltpu.make_async_copy(src_ref, rows(*me), local_sem),
            "own": lambda: [copy(0, me, sibling, src=src_ref)] + [copy(1 + j, me, (*chip, c), src=src_ref)
                                                                  for j, chip in enumerate(chips)],
            "passed": lambda: [copy(4 + j, (*chip, c), sibling) for j, chip in enumerate(chips)],
            "landed": lambda: [copy(1 + j, (*chip, c), me) for j, chip in enumerate(chips)],
            "from_sibling": lambda: [copy(0, sibling, me)] + [copy(4 + j, (*chip, 1 - c), me)
                                                              for j, chip in enumerate(chips)],
        }
        return lambda *names: [makers[n]() for n in names]

    def start(self, *refs):
        if self.kind == "gather":
            local, own = self._gather_copies(*refs)("local", "own")
        else:
            local, own = self._scatter_copies(*refs)
        local.start()
        for cp in own:
            cp.start()

    def middle(self, *refs):
        if self.kind == "gather":
            landed, passed = self._gather_copies(*refs)("landed", "passed")
            for arrived, forward in zip(landed, passed):
                arrived.wait_recv()
                forward.start()

    def finish(self, *refs):
        if self.kind == "gather":
            local, own, passed, from_sibling = self._gather_copies(*refs)("local", "own", "passed", "from_sibling")
            for cp in from_sibling:
                cp.wait_recv()
            for cp in own + passed:
                cp.wait_send()
        else:
            local, remote = self._scatter_copies(*refs)
            for cp in remote:
                cp.wait_recv()
            for cp in remote:
                cp.wait_send()
        local.wait()


_ANY = pl.BlockSpec(memory_space=pl.ANY)
_EXCHANGE_SEMS = [pltpu.SemaphoreType.DMA((N_DEV - 1,)), pltpu.SemaphoreType.DMA((N_DEV - 1,)), pltpu.SemaphoreType.DMA]


def _carry(exchanges, src_refs, dst_refs, sem_refs, first, middle, last, compute):
    def each():
        for e, (ex, s, d) in enumerate(zip(exchanges, src_refs, dst_refs)):
            yield ex, (s, d, *sem_refs[3 * e:3 * e + 3])

    if exchanges:
        @pl.when(first)
        def _():
            for ex, refs in each():
                ex.start(*refs)

    compute()

    if any(ex.kind == "gather" for ex in exchanges):
        @pl.when(middle)
        def _():
            for ex, refs in each():
                ex.middle(*refs)

    if exchanges:
        @pl.when(last)
        def _():
            for ex, refs in each():
                ex.finish(*refs)


def _matmul(a, b, mode, out_dtype, name, b_blocked=False, out_blocked=False, extra=(), epilogue=None,
            out_dtypes=None, comm=()):
    if mode == "nn":
        (m, k), n = a.shape, b.shape[-1] * (N_DEV if b_blocked else 1)
    elif mode == "nt":
        (m, k), n = a.shape, b.shape[-2]
    else:
        (k, m), n = a.shape, b.shape[1]
    tm, tn, tk = _tile(m, 1024), _tile(n, 1024), _tile(k, 2048)
    if (b_blocked and mode == "nn") or out_blocked:
        tn = n // N_DEV
    span = 1
    if b_blocked and mode == "nt":
        kb = k // N_DEV
        span = max(1, min(N_DEV, 2048 // kb))
        tk = span * kb
    nj, ni, nk = n // tn, m // tm, k // tk
    dims = {"nn": (((1,), (0,)), ((), ())), "nt": (((1,), (1,)), ((), ())), "tn": (((0,), (0,)), ((), ()))}[mode]
    out_dtypes = out_dtypes or [out_dtype]
    ne, nc, no = len(extra), len(comm), len(out_dtypes)

    def body(*refs):
        a_ref, b_refs = refs[0], refs[1:1 + span]
        refs = refs[span - 1:]
        extra_refs, src_refs = refs[2:2 + ne], refs[2 + ne:2 + ne + nc]
        o = 2 + ne + nc
        out_refs, dst_refs = refs[o:o + no], refs[o + no:o + no + nc]
        rest = refs[o + no + nc:]
        acc, sems = (rest[0], rest[1:]) if nk > 1 else (None, rest)
        j, i, kk = pl.program_id(0), pl.program_id(1), pl.program_id(2)

        def finish(res):
            outs = epilogue(res, *[r[...] for r in extra_refs]) if epilogue else (res,)
            for ref, val in zip(out_refs, outs):
                ref[...] = val.astype(ref.dtype)

        def compute():
            a_tile = a_ref[...].astype(BF16)
            kw = a_tile.shape[1] // span if mode == "nt" else None
            prod = None
            for s, b_ref in enumerate(b_refs):
                a_part = a_tile if span == 1 else a_tile[:, s * kw:(s + 1) * kw]
                part = lax.dot_general(a_part, b_ref[...].astype(BF16), dims, preferred_element_type=F32)
                prod = part if prod is None else prod + part
            if nk == 1:
                finish(prod)
                return

            @pl.when(kk == 0)
            def _():
                acc[...] = prod

            @pl.when(kk > 0)
            def _():
                acc[...] += prod

            @pl.when(kk == nk - 1)
            def _():
                finish(acc[...])

        step = (j * ni + i) * nk + kk
        _carry(comm, src_refs, dst_refs, sems, step == 0, step == (nj * ni * nk) // 2, step == nj * ni * nk - 1, compute)

    a_spec = (pl.BlockSpec((tk, tm), lambda j, i, kk: (kk, i)) if mode == "tn"
              else pl.BlockSpec((tm, tk), lambda j, i, kk: (i, kk)))
    if b_blocked and mode == "nn":
        b_specs = [pl.BlockSpec((None, tk, tn), lambda j, i, kk: (j, kk, 0))]
    elif b_blocked:
        b_specs = [pl.BlockSpec((None, tn, tk // span), lambda j, i, kk, s=s: (kk * span + s, j, 0)) for s in range(span)]
    else:
        b_specs = [pl.BlockSpec((tn, tk), lambda j, i, kk: (j, kk)) if mode == "nt"
                   else pl.BlockSpec((tk, tn), lambda j, i, kk: (kk, j))]
    tile_spec = pl.BlockSpec((tm, tn), lambda j, i, kk: (i, j))
    if out_blocked:
        o_spec = pl.BlockSpec((None, tm, tn), lambda j, i, kk: (j, i, 0))
        o_shape = (N_DEV, m, tn)
    else:
        o_spec, o_shape = tile_spec, (m, n)
    res = _pallas_call(
        body, name=name, grid=(nj, ni, nk),
        in_specs=[a_spec] + b_specs + [tile_spec] * ne + [_ANY] * nc,
        out_specs=[o_spec] * no + [_ANY] * nc,
        out_shape=[jax.ShapeDtypeStruct(o_shape, d) for d in out_dtypes] + [ex.dst for ex in comm],
        scratch_shapes=([pltpu.VMEM((tm, tn), F32)] if nk > 1 else []) + _EXCHANGE_SEMS * nc,
        compiler_params=_params("arbitrary", "arbitrary", "arbitrary"),
    )(a, *[b] * span, *extra, *[ex.src for ex in comm])
    return res[0] if len(res) == 1 else res


def _col_spec(tb, spec):
    if isinstance(spec, tuple):
        arr, blk, width = spec
        return arr, pl.BlockSpec((tb, width), lambda i, blk=blk: (i, blk))
    return spec, pl.BlockSpec((tb, spec.shape[1]), lambda i: (i, 0))


def _full_spec(p):
    return pl.BlockSpec(p.shape, lambda i, nd=p.ndim: (0,) * nd)


def _rowwise(fn, xs, params, outs, tb, name, comm=()):
    arrs, specs = zip(*[_col_spec(tb, s) for s in xs])
    t = arrs[0].shape[0]
    nx, npar, no, nc = len(xs), len(params), len(outs), len(comm)
    nb = t // tb

    def body(*refs):
        src_refs = refs[nx + npar:nx + npar + nc]
        o = nx + npar + nc
        out_refs, dst_refs, sems = refs[o:o + no], refs[o + no:o + no + nc], refs[o + no + nc:]
        i = pl.program_id(0)

        def compute():
            res = fn(*[r[...] for r in refs[:nx + npar]])
            for o_ref, r in zip(out_refs, res):
                o_ref[...] = r.astype(o_ref.dtype)

        _carry(comm, src_refs, dst_refs, sems, i == 0, i == nb // 2, i == nb - 1, compute)

    return _pallas_call(
        body, name=name, grid=(nb,),
        in_specs=list(specs) + [_full_spec(p) for p in params] + [_ANY] * nc,
        out_specs=[pl.BlockSpec((tb, w), lambda i: (i, 0)) for w, _ in outs] + [_ANY] * nc,
        out_shape=[jax.ShapeDtypeStruct((t, w), d) for w, d in outs] + [ex.dst for ex in comm],
        scratch_shapes=_EXCHANGE_SEMS * nc,
        compiler_params=_params("arbitrary" if comm else "parallel"),
    )(*arrs, *params, *[ex.src for ex in comm])


def _rowwise_vjp(fn, xs, params, cts, dx_dtypes, tb, name, adds=None, primal_out=None, primal_width=1,
                 merge_dx=False):
    arrs, specs = zip(*[_col_spec(tb, s) for s in xs])
    t = arrs[0].shape[0]
    ct_groups = [c if isinstance(c, tuple) else (c,) for c in cts]
    ct_flat = [a for grp in ct_groups for a in grp]
    nx, npar, nct = len(xs), len(params), len(ct_flat)
    adds = adds or [None] * nx
    add_ix = [i for i in range(nx) if adds[i] is not None]
    dx_ix = [i for i in range(nx) if dx_dtypes[i] is not None]
    widths = [s.block_shape[1] for s in specs]
    n_dx_out = 1 if merge_dx else len(dx_ix)

    def body(*refs):
        x_refs, p_refs = refs[:nx], refs[nx:nx + npar]
        ct_refs = list(refs[nx + npar:nx + npar + nct])
        add_refs = refs[nx + npar + nct:nx + npar + nct + len(add_ix)]
        o = nx + npar + nct + len(add_ix)
        dx_refs, dp_refs = refs[o:o + n_dx_out], refs[o + n_dx_out:o + n_dx_out + npar]
        prim, vjp = jax.vjp(fn, *[r[...] for r in x_refs], *[r[...] for r in p_refs])
        ct_vals = []
        for grp, p in zip(ct_groups, prim):
            val = ct_refs.pop(0)[...].astype(p.dtype)
            for _ in grp[1:]:
                val = val + ct_refs.pop(0)[...].astype(p.dtype)
            ct_vals.append(val)
        grads = vjp(tuple(ct_vals))
        col = 0
        for n, i in enumerate(dx_ix):
            g = grads[i].astype(F32)
            if adds[i] is not None:
                g = g + add_refs[add_ix.index(i)][...].astype(F32)
            if merge_dx:
                dx_refs[0][:, col:col + widths[i]] = g.astype(dx_refs[0].dtype)
                col += widths[i]
            else:
                dx_refs[n][...] = g.astype(dx_refs[n].dtype)

        @pl.when(pl.program_id(0) == 0)
        def _():
            for ref in dp_refs:
                ref[...] = jnp.zeros_like(ref)

        for ref, g in zip(dp_refs, grads[nx:]):
            ref[...] += g.astype(F32)
        if primal_out is not None:
            refs[-1][...] = prim[primal_out].astype(refs[-1].dtype)

    ct_specs = [pl.BlockSpec((tb, c.shape[1]), lambda i: (i, 0)) for c in ct_flat]
    add_specs = [pl.BlockSpec((tb, widths[i]), lambda i_: (i_, 0)) for i in add_ix]
    if merge_dx:
        total = sum(widths[i] for i in dx_ix)
        out_specs = [pl.BlockSpec((tb, total), lambda i_: (i_, 0))]
        out_shape = [jax.ShapeDtypeStruct((t, total), dx_dtypes[dx_ix[0]])]
    else:
        out_specs = [pl.BlockSpec((tb, widths[i]), lambda i_: (i_, 0)) for i in dx_ix]
        out_shape = [jax.ShapeDtypeStruct((t, widths[i]), dx_dtypes[i]) for i in dx_ix]
    out_specs += [_full_spec(p) for p in params]
    out_shape += [jax.ShapeDtypeStruct(p.shape, F32) for p in params]
    if primal_out is not None:
        out_specs.append(pl.BlockSpec((tb, primal_width), lambda i: (i, 0)))
        out_shape.append(jax.ShapeDtypeStruct((t, primal_width), F32))
    res = _pallas_call(
        body, name=name, grid=(t // tb,),
        in_specs=list(specs) + [_full_spec(p) for p in params] + ct_specs + add_specs,
        out_specs=out_specs, out_shape=out_shape,
        compiler_params=_params("arbitrary"),
    )(*arrs, *params, *ct_flat, *[adds[i] for i in add_ix])
    ndx = n_dx_out
    out = (list(res[:ndx]), list(res[ndx:ndx + npar]))
    return out + (res[-1],) if primal_out is not None else out


def _rms(x, w):
    xf = x.astype(F32)
    return xf * lax.rsqrt(jnp.mean(xf * xf, axis=-1, keepdims=True) + EPS) * w


def _seg_norm(x, w):
    return (_rms(x, w),)


def _seg_residual(h, y, w_post, w_pre):
    h2 = h.astype(F32) + _rms(y, w_post)
    return h2, _rms(h2, w_pre)


def _seg_loss(h, y, tgt, w_post):
    err = h.astype(F32) + _rms(y, w_post) - tgt
    return (0.5 * jnp.mean(err * err, axis=-1, keepdims=True),)


GELU_C, GELU_A = math.sqrt(2.0 / math.pi), 0.044715


def _gelu_parts(x):
    x2 = x * x
    th = jnp.tanh(GELU_C * x * (1.0 + GELU_A * x2))
    half = 0.5 * x
    return half * (1.0 + th), 0.5 * (1.0 + th) + half * (1.0 - th * th) * (GELU_C * (1.0 + 3.0 * GELU_A * x2))


def _row_mean(acc, width):
    return jnp.broadcast_to(jnp.sum(acc, axis=-1, keepdims=True) * (1.0 / width), acc.shape)


def _gelu_ln_stats(pv_ref, vbuf, gbuf, chunks, width):
    tb = pv_ref.shape[0]
    acc = jnp.zeros((tb, LANES), F32)
    for sl in chunks:
        v, g = _gelu_parts(pv_ref[:, sl])
        vbuf[:, sl] = v
        if gbuf is not None:
            gbuf[:, sl] = g
        acc = acc + v
    mu = _row_mean(acc, width)
    acc = jnp.zeros((tb, LANES), F32)
    for sl in chunks:
        dev = vbuf[:, sl] - mu
        acc = acc + dev * dev
    return mu, lax.rsqrt(_row_mean(acc, width) + EPS)


def _gelu_ln(p, ln_w, ln_b, tb, name):
    t, w = p.shape[0], p.shape[1] // 2
    chunks = [slice(j, j + LANES) for j in range(0, w, LANES)]

    def body(pu_ref, pv_ref, lnw_ref, lnb_ref, u_ref, vn_ref, vbuf):
        for sl in chunks:
            u_ref[:, sl] = _gelu_parts(pu_ref[:, sl])[0]
        mu, rstd = _gelu_ln_stats(pv_ref, vbuf, None, chunks, w)
        for sl in chunks:
            vn_ref[:, sl] = ((vbuf[:, sl] - mu) * rstd * lnw_ref[:, sl] + lnb_ref[:, sl]).astype(vn_ref.dtype)

    half = lambda b: pl.BlockSpec((tb, w), lambda i, b=b: (i, b))
    vec = pl.BlockSpec((1, w), lambda i: (0, 0))
    return _pallas_call(
        body, name=name, grid=(t // tb,),
        in_specs=[half(0), half(1), vec, vec], out_specs=[half(0), half(0)],
        out_shape=[jax.ShapeDtypeStruct((t, w), F32), jax.ShapeDtypeStruct((t, w), BF16)],
        scratch_shapes=[pltpu.VMEM((tb, w), F32)],
        compiler_params=_params("parallel"),
    )(p, p, ln_w, ln_b)


def _gelu_ln_vjp(p, ln_w, du, dvn, tb, name):
    t, w = p.shape[0], p.shape[1] // 2
    nb = t // tb
    chunks = [slice(j, j + LANES) for j in range(0, w, LANES)]
    fold = lambda a: jnp.sum(a.reshape(tb // 8, 8, LANES), axis=0)

    def body(pu_ref, pv_ref, lnw_ref, du_ref, dvn_ref, dp_ref, dlnw_ref, dlnb_ref, vbuf, gbuf, acc_w, acc_b):
        i = pl.program_id(0)

        @pl.when(i == 0)
        def _():
            acc_w[...] = jnp.zeros_like(acc_w)
            acc_b[...] = jnp.zeros_like(acc_b)

        for sl in chunks:
            dp_ref[:, sl] = (du_ref[:, sl] * _gelu_parts(pu_ref[:, sl])[1]).astype(dp_ref.dtype)
        mu, rstd = _gelu_ln_stats(pv_ref, vbuf, gbuf, chunks, w)
        m1 = jnp.zeros((tb, LANES), F32)
        m2 = jnp.zeros((tb, LANES), F32)
        for sl in chunks:
            xhat = (vbuf[:, sl] - mu) * rstd
            dy = dvn_ref[:, sl]
            dxhat = dy * lnw_ref[:, sl]
            m1 = m1 + dxhat
            m2 = m2 + dxhat * xhat
            acc_w[:, sl] += fold(dy * xhat)
            acc_b[:, sl] += fold(dy)
        m1, m2 = _row_mean(m1, w), _row_mean(m2, w)
        for j, sl in enumerate(chunks):
            xhat = (vbuf[:, sl] - mu) * rstd
            dv = rstd * (dvn_ref[:, sl] * lnw_ref[:, sl] - m1 - xhat * m2)
            dp_ref[:, w + j * LANES:w + (j + 1) * LANES] = (dv * gbuf[:, sl]).astype(dp_ref.dtype)

        @pl.when(i == nb - 1)
        def _():
            dlnw_ref[...] = jnp.sum(acc_w[...], axis=0, keepdims=True)
            dlnb_ref[...] = jnp.sum(acc_b[...], axis=0, keepdims=True)

    half = lambda b: pl.BlockSpec((tb, w), lambda i, b=b: (i, b))
    vec = pl.BlockSpec((1, w), lambda i: (0, 0))
    return _pallas_call(
        body, name=name, grid=(nb,),
        in_specs=[half(0), half(1), vec, half(0), half(0)],
        out_specs=[pl.BlockSpec((tb, 2 * w), lambda i: (i, 0)), vec, vec],
        out_shape=[jax.ShapeDtypeStruct((t, 2 * w), BF16), jax.ShapeDtypeStruct((1, w), F32),
                   jax.ShapeDtypeStruct((1, w), F32)],
        scratch_shapes=[pltpu.VMEM((tb, w), F32), pltpu.VMEM((tb, w), F32), pltpu.VMEM((8, w), F32),
                        pltpu.VMEM((8, w), F32)],
        compiler_params=_params("arbitrary"),
    )(p, p, ln_w, du, dvn)


def _make_seg_gates(n_heads):
    def seg(raw, a_log, dt_bias):
        lane = lax.broadcasted_iota(jnp.int32, raw.shape, 1)
        beta = jax.nn.sigmoid(raw)
        g = -jnp.exp(a_log) * jax.nn.softplus(raw + dt_bias)
        return (jnp.where(lane < n_heads, beta, jnp.where(lane < 2 * n_heads, g, 0.0)),)
    return seg


def _gate_fn(u, v, ws, bs):
    rows, gd = u.shape
    nb = rows // SG_CHUNK
    r = lax.broadcasted_iota(jnp.int32, (SG_CHUNK, SG_CHUNK), 0)
    c = lax.broadcasted_iota(jnp.int32, (SG_CHUNK, SG_CHUNK), 1)
    w = jnp.broadcast_to(jnp.where(r >= c, ws[0], 0.0).astype(BF16), (nb, SG_CHUNK, SG_CHUNK))
    v3 = v.astype(BF16).reshape(nb, SG_CHUNK, gd)
    s = lax.dot_general(w, v3, (((2,), (1,)), ((0,), (0,))), preferred_element_type=F32) + bs[0]
    return u.astype(F32) * s.reshape(rows, gd)


def _sg_rows(t):
    rows = SG_CHUNK * SG_STEP_CHUNKS
    while t % rows:
        rows -= SG_CHUNK
    return rows


def _sg_specs(rows, gd):
    x_spec = pl.BlockSpec((rows, gd), lambda g, i: (i, g))
    ws_spec = pl.BlockSpec((1, SG_CHUNK, SG_CHUNK), lambda g, i: (g, 0, 0))
    bs_spec = pl.BlockSpec((1, SG_CHUNK, 1), lambda g, i: (g, 0, 0))
    return x_spec, ws_spec, bs_spec


def _spatial_gate(u, v, ws, bs, name):
    t, w = u.shape
    gd = w // SG_GROUPS
    rows = _sg_rows(t)
    x_spec, ws_spec, bs_spec = _sg_specs(rows, gd)

    def body(u_ref, v_ref, ws_ref, bs_ref, o_ref):
        o_ref[...] = _gate_fn(u_ref[...], v_ref[...], ws_ref[...], bs_ref[...]).astype(o_ref.dtype)

    return _pallas_call(
        body, name=name, grid=(SG_GROUPS, t // rows),
        in_specs=[x_spec, x_spec, ws_spec, bs_spec], out_specs=x_spec,
        out_shape=jax.ShapeDtypeStruct((t, w), BF16),
        compiler_params=_params("parallel", "parallel"),
    )(u, v, ws, bs)


def _spatial_gate_vjp(u, v, ws, bs, ct, name, comm=()):
    t, w = u.shape
    gd = w // SG_GROUPS
    rows = _sg_rows(t)
    nchunks = t // rows
    nx = len(comm)
    x_spec, ws_spec, bs_spec = _sg_specs(rows, gd)

    def body(*refs):
        u_ref, v_ref, ws_ref, bs_ref, ct_ref = refs[:5]
        src_refs = refs[5:5 + nx]
        du_ref, dv_ref, dws_ref, dbs_ref = refs[5 + nx:9 + nx]
        dst_refs, sems = refs[9 + nx:9 + 2 * nx], refs[9 + 2 * nx:]
        g, i = pl.program_id(0), pl.program_id(1)

        def compute():
            _, vjp = jax.vjp(_gate_fn, u_ref[...], v_ref[...].astype(F32), ws_ref[...], bs_ref[...])
            du, dv, dws, dbs = vjp(ct_ref[...])
            du_ref[...] = du
            dv_ref[...] = dv

            @pl.when(i == 0)
            def _():
                dws_ref[...] = jnp.zeros_like(dws_ref)
                dbs_ref[...] = jnp.zeros_like(dbs_ref)

            dws_ref[...] += dws
            dbs_ref[...] += dbs

        step = g * nchunks + i
        _carry(comm, src_refs, dst_refs, sems, step == 0, step == (SG_GROUPS * nchunks) // 2,
               step == SG_GROUPS * nchunks - 1, compute)

    return _pallas_call(
        body, name=name, grid=(SG_GROUPS, nchunks),
        in_specs=[x_spec, x_spec, ws_spec, bs_spec, x_spec] + [_ANY] * nx,
        out_specs=[x_spec, x_spec, ws_spec, bs_spec] + [_ANY] * nx,
        out_shape=[jax.ShapeDtypeStruct((t, w), F32), jax.ShapeDtypeStruct((t, w), F32),
                   jax.ShapeDtypeStruct(ws.shape, F32), jax.ShapeDtypeStruct(bs.shape, F32)] + [ex.dst for ex in comm],
        scratch_shapes=_EXCHANGE_SEMS * nx,
        compiler_params=_params("arbitrary", "arbitrary"),
    )(u, v, ws, bs, ct, *[ex.src for ex in comm])


def _conv_silu(proj, w, width, name):
    t = proj.shape[0]
    tb = _row_tile(t, 128)
    hb = tb // HALO

    def body(prev_ref, x_ref, w_ref, o_ref, ext):
        keep = (pl.program_id(0) > 0).astype(F32)
        ext[0:HALO, :] = prev_ref[...].astype(F32) * keep
        ext[HALO:HALO + tb, :] = x_ref[...].astype(F32)
        for sl in [slice(c0, c0 + LANES) for c0 in range(0, width, LANES)]:
            acc = jnp.zeros((tb, LANES), F32)
            for j in range(CONV_WIDTH):
                o = HALO - (CONV_WIDTH - 1) + j
                acc = acc + w_ref[j:j + 1, sl] * ext[o:o + tb, sl]
            o_ref[:, sl] = jax.nn.silu(acc).astype(o_ref.dtype)

    return _pallas_call(
        body, name=name, grid=(t // tb,),
        in_specs=[pl.BlockSpec((HALO, width), lambda i: (jnp.maximum(i * hb - 1, 0), 0)),
                  pl.BlockSpec((tb, width), lambda i: (i, 0)),
                  pl.BlockSpec((CONV_WIDTH, width), lambda i: (0, 0))],
        out_specs=pl.BlockSpec((tb, width), lambda i: (i, 0)),
        out_shape=jax.ShapeDtypeStruct((t, width), F32),
        scratch_shapes=[pltpu.VMEM((tb + HALO, width), F32)],
        compiler_params=_params("parallel"),
    )(proj, proj, w)


def _conv_silu_vjp(proj, w, dy, width, name):
    t = proj.shape[0]
    tb = _row_tile(t, 128)
    hb = tb // HALO
    nb = t // tb
    te = tb + HALO

    def body(prev_ref, x_ref, next_ref, w_ref, dy_ref, dyn_ref, dx_ref, dw_ref, ext, dpre):
        i = pl.program_id(0)
        first = (i > 0).astype(F32)
        last = (i < nb - 1).astype(F32)
        ext[0:HALO, :] = prev_ref[...].astype(F32) * first
        ext[HALO:HALO + tb, :] = x_ref[...].astype(F32)
        ext[HALO + tb:, :] = next_ref[...].astype(F32) * last
        @pl.when(i == 0)
        def _():
            dw_ref[...] = jnp.zeros_like(dw_ref)

        for sl in [slice(c0, c0 + LANES) for c0 in range(0, width, LANES)]:
            pre = jnp.zeros((te, LANES), F32)
            for j in range(CONV_WIDTH):
                o = HALO - (CONV_WIDTH - 1) + j
                pre = pre + w_ref[j:j + 1, sl] * ext[o:o + te, sl]
            sig = jax.nn.sigmoid(pre)
            dsilu = sig * (1.0 + pre * (1.0 - sig))
            own = dy_ref[:, sl].astype(F32) * dsilu[0:tb, :]
            dpre[0:tb, sl] = own
            dpre[tb:, sl] = dyn_ref[:, sl].astype(F32) * last * dsilu[tb:, :]
            dx = jnp.zeros((tb, LANES), F32)
            for j in range(CONV_WIDTH):
                o = CONV_WIDTH - 1 - j
                dx = dx + w_ref[j:j + 1, sl] * dpre[o:o + tb, sl]
            dx_ref[:, sl] = dx.astype(dx_ref.dtype)
            for j in range(CONV_WIDTH):
                o = HALO - (CONV_WIDTH - 1) + j
                dw_ref[j:j + 1, sl] += jnp.sum(own * ext[o:o + tb, sl], axis=0, keepdims=True)

    halo_prev = lambda i: (jnp.maximum(i * hb - 1, 0), 0)
    halo_next = lambda i: (jnp.minimum((i + 1) * hb, t // HALO - 1), 0)
    return _pallas_call(
        body, name=name, grid=(nb,),
        in_specs=[pl.BlockSpec((HALO, width), halo_prev),
                  pl.BlockSpec((tb, width), lambda i: (i, 0)),
                  pl.BlockSpec((HALO, width), halo_next),
                  pl.BlockSpec((CONV_WIDTH, width), lambda i: (0, 0)),
                  pl.BlockSpec((tb, width), lambda i: (i, 0)),
                  pl.BlockSpec((HALO, width), halo_next)],
        out_specs=[pl.BlockSpec((tb, width), lambda i: (i, 0)),
                   pl.BlockSpec((CONV_WIDTH, width), lambda i: (0, 0))],
        out_shape=[jax.ShapeDtypeStruct((t, width), BF16), jax.ShapeDtypeStruct((CONV_WIDTH, width), F32)],
        scratch_shapes=[pltpu.VMEM((tb + 2 * HALO, width), F32), pltpu.VMEM((te, width), F32)],
        compiler_params=_params("arbitrary"),
    )(proj, proj, proj, w, dy, dy)


def _bdot(a, b, ca, cb):
    return lax.dot_general(a.astype(BF16), b.astype(BF16), (((ca,), (cb,)), ((0,), (0,))),
                           preferred_element_type=F32)


def _bdot3(a, b, ca, cb):
    a_hi, b_hi = a.astype(BF16), b.astype(BF16)
    a_lo, b_lo = (a - a_hi.astype(F32)).astype(BF16), (b - b_hi.astype(F32)).astype(BF16)
    dot = lambda p, q: lax.dot_general(p, q, (((ca,), (cb,)), ((0,), (0,))), preferred_element_type=F32)
    return dot(a_hi, b_hi) + (dot(a_hi, b_lo) + dot(a_lo, b_hi))


def _unit_lower_inverse(a):
    cs = a.shape[-1]
    r = lax.broadcasted_iota(jnp.int32, (1, cs, cs), 1)
    c = lax.broadcasted_iota(jnp.int32, (1, cs, cs), 2)
    eye = (r == c).astype(F32)
    diag = jnp.where((r // INV_BLOCK) == (c // INV_BLOCK), a, 0.0)

    def nilpotent_inverse(x, order):
        inv, p = eye - x, x
        for _ in range(int(math.log2(order)) - 1):
            p = _bdot3(p, p, 2, 1)
            inv = inv + _bdot3(inv, p, 2, 1)
        return inv

    t_diag = nilpotent_inverse(diag, INV_BLOCK)
    return _bdot3(nilpotent_inverse(_bdot3(t_diag, a - diag, 2, 1), cs // INV_BLOCK), t_diag, 2, 1)


@jax.custom_vjp
def _inverse_given(a, t):
    return t


_inverse_given.defvjp(lambda a, t: (t, t),
                      lambda t, g: (-_bdot3(_bdot3(t, g, 1, 1), t, 2, 2), jnp.zeros_like(t)))


def _bmm(a, b):
    return _bdot(a, b, 2, 1)


def _bmm_nt(a, b):
    return _bdot(a, b, 2, 2)


def _bmm_tn(a, b):
    return _bdot(a, b, 1, 1)


def _head_rms(o):
    return o * lax.rsqrt(jnp.mean(o * o, axis=-1, keepdims=True) + EPS)


def _l2norm(x):
    return x * lax.rsqrt(jnp.sum(x * x, axis=-1, keepdims=True) + 1e-6)


def _gdn_chunk(state, qc, kc, vc, z, beta_r, g_r, onw, t_saved=None):
    cs = LA_CHUNK
    r = lax.broadcasted_iota(jnp.int32, (1, cs, cs), 1)
    c = lax.broadcasted_iota(jnp.int32, (1, cs, cs), 2)
    eye = (r == c).astype(F32)
    causal, strict = r >= c, r > c
    q = _l2norm(qc.astype(F32)) * (HEAD_DIM ** -0.5)
    k = _l2norm(kc.astype(F32))
    v = vc.astype(F32)
    g_col = jnp.sum(eye * g_r, axis=-1, keepdims=True)
    beta_col = jnp.sum(eye * beta_r, axis=-1, keepdims=True)
    gc_col = jnp.sum(causal.astype(F32) * g_r, axis=-1, keepdims=True)
    gc_row = jnp.sum((r <= c).astype(F32) * g_col, axis=-2, keepdims=True)
    gc_last = jnp.sum(g_r, axis=-1, keepdims=True)
    decay = jnp.where(causal, jnp.exp(jnp.where(causal, gc_col - gc_row, 0.0)), 0.0)
    kb = k * beta_col
    a = jnp.where(strict, _bmm_nt(kb, k) * decay, 0.0)
    t_inv = _unit_lower_inverse(a) if t_saved is None else _inverse_given(a, t_saved)
    eg = jnp.exp(gc_col)
    u = _bmm(t_inv, v * beta_col)
    w = _bmm(t_inv, kb * eg)
    qk = jnp.where(causal, _bmm_nt(q, k) * decay, 0.0)
    v_new = u - _bmm(w, state)
    o = _bmm(q * eg, state) + _bmm(qk, v_new)
    new_state = state * jnp.exp(gc_last) + _bmm_tn(k * jnp.exp(gc_last - gc_col), v_new)
    o = _head_rms(o) * onw * jax.nn.silu(z.astype(F32))
    return (new_state, o, t_inv) if t_saved is None else (new_state, o)


@jax.custom_vjp
def _swap_halves(x):
    return pltpu.roll(x, HEAD_DIM // 2, x.ndim - 1)


_swap_halves.defvjp(lambda x: (_swap_halves(x), None), lambda _, g: (_swap_halves(g),))


def _make_ret_chunk(cos2, sin2, d_mat, dec_q, dec_k, dec_c):
    def chunk(state, rq, rk, rv, rg):
        qf, kf = rq.astype(F32), rk.astype(F32)
        q = qf * cos2 + _swap_halves(qf) * sin2
        k = (kf * cos2 + _swap_halves(kf) * sin2) * (HEAD_DIM ** -0.5)
        v = rv.astype(F32)
        inner = _bmm(_bmm_nt(q, k) * d_mat, v)
        cross = _bmm(q * dec_q, state)
        new_state = state * dec_c + _bmm_tn(k * dec_k, v)
        return new_state, jax.nn.silu(rg.astype(F32)) * _head_rms(inner + cross)
    return chunk


def _split_heads(x, n_heads):
    return jnp.stack([x[:, h * HEAD_DIM:(h + 1) * HEAD_DIM] for h in range(n_heads)], axis=0)


def _store_heads(ref, col0, val):
    for h in range(val.shape[0]):
        ref[:, col0 + h * HEAD_DIM:col0 + (h + 1) * HEAD_DIM] = val[h].astype(ref.dtype)


def _scan_fwd(make_chunk, seqs, rows, consts, n_heads, name, comm=(), saved_shape=None):
    cs, hk = LA_CHUNK, n_heads * HEAD_DIM
    t = seqs[0][0].shape[0]
    n = t // cs
    ns, nr, nc, nx = len(seqs), len(rows), len(consts), len(comm)
    n_out = 2 if saved_shape is None else 3

    def body(*refs):
        seq_refs, row_refs, const_refs = refs[:ns], refs[ns:ns + nr], refs[ns + nr:ns + nr + nc]
        o = ns + nr + nc
        src_refs = refs[o:o + nx]
        out_refs = refs[o + nx:o + nx + n_out]
        dst_refs = refs[o + nx + n_out:o + 2 * nx + n_out]
        state, sems = refs[o + 2 * nx + n_out], refs[o + 2 * nx + n_out + 1:]
        i = pl.program_id(0)

        def compute():
            @pl.when(i == 0)
            def _():
                state[...] = jnp.zeros_like(state)

            out_refs[1][0] = state[...]
            chunk = make_chunk([r[0] for r in row_refs], [r[...] for r in const_refs])
            new_state, out, *saved = chunk(state[...], *[_split_heads(r[...], n_heads) for r in seq_refs])
            state[...] = new_state
            _store_heads(out_refs[0], 0, out)
            if saved_shape is not None:
                out_refs[2][0] = saved[0]

        _carry(comm, src_refs, dst_refs, sems, i == 0, i == n // 2, i == n - 1, compute)

    chunk_spec = lambda shape: pl.BlockSpec((1,) + tuple(shape), lambda i, nd=len(shape): (i,) + (0,) * nd)
    saved = [] if saved_shape is None else [tuple(saved_shape)]
    return _pallas_call(
        body, name=name, grid=(n,),
        in_specs=[pl.BlockSpec((cs, hk), lambda i, b=b: (i, b)) for _, b in seqs]
        + [chunk_spec(a.shape[1:]) for a in rows] + [_full_spec(a) for a in consts] + [_ANY] * nx,
        out_specs=[pl.BlockSpec((cs, hk), lambda i: (i, 0)), chunk_spec((n_heads, HEAD_DIM, HEAD_DIM))]
        + [chunk_spec(s) for s in saved] + [_ANY] * nx,
        out_shape=[jax.ShapeDtypeStruct((t, hk), BF16),
                   jax.ShapeDtypeStruct((n, n_heads, HEAD_DIM, HEAD_DIM), F32)]
        + [jax.ShapeDtypeStruct((n,) + s, F32) for s in saved] + [ex.dst for ex in comm],
        scratch_shapes=[pltpu.VMEM((n_heads, HEAD_DIM, HEAD_DIM), F32)] + _EXCHANGE_SEMS * nx,
        compiler_params=_params("arbitrary"),
    )(*[a for a, _ in seqs], *rows, *consts, *[ex.src for ex in comm])


def _scan_bwd(make_chunk, seqs, rows, consts, states, d_out, n_heads, n_row_grads, n_const_grads, dseq_dtype,
              name, comm=()):
    cs, hk = LA_CHUNK, n_heads * HEAD_DIM
    t = seqs[0][0].shape[0]
    n = t // cs
    ns, nr, nc, nx = len(seqs), len(rows), len(consts), len(comm)
    n_grads = 1 + n_row_grads + n_const_grads

    def body(*refs):
        seq_refs, row_refs, const_refs = refs[:ns], refs[ns:ns + nr], refs[ns + nr:ns + nr + nc]
        st_ref, do_ref = refs[ns + nr + nc:ns + nr + nc + 2]
        o = ns + nr + nc + 2
        src_refs = refs[o:o + nx]
        o += nx
        dseq_ref = refs[o]
        drow_refs = refs[o + 1:o + 1 + n_row_grads]
        dconst_refs = refs[o + 1 + n_row_grads:o + n_grads]
        dst_refs = refs[o + n_grads:o + n_grads + nx]
        d_state, sems = refs[o + n_grads + nx], refs[o + n_grads + nx + 1:]
        i = pl.program_id(0)

        def compute():
            @pl.when(i == 0)
            def _():
                d_state[...] = jnp.zeros_like(d_state)
                for ref in dconst_refs:
                    ref[...] = jnp.zeros_like(ref)

            row_vals = [r[0] for r in row_refs]
            const_vals = [r[...] for r in const_refs]

            def fn(state, seq_vals, row_d, const_d):
                chunk = make_chunk(list(row_d) + row_vals[n_row_grads:], list(const_d) + const_vals[n_const_grads:])
                return chunk(state, *seq_vals)

            _, vjp = jax.vjp(fn, st_ref[0], tuple(_split_heads(r[...], n_heads) for r in seq_refs),
                             tuple(row_vals[:n_row_grads]), tuple(const_vals[:n_const_grads]))
            ds, dseq, drow, dconst = vjp((d_state[...], _split_heads(do_ref[...], n_heads).astype(F32)))
            d_state[...] = ds
            for j, g in enumerate(dseq):
                _store_heads(dseq_ref, j * hk, g)
            for ref, g in zip(drow_refs, drow):
                ref[0] = g
            for ref, g in zip(dconst_refs, dconst):
                ref[...] += g

        _carry(comm, src_refs, dst_refs, sems, i == 0, i == n // 2, i == n - 1, compute)

    rev = lambda i: n - 1 - i
    row_spec = lambda a: pl.BlockSpec((1,) + a.shape[1:], lambda i, nd=a.ndim: (rev(i),) + (0,) * (nd - 1))
    res = _pallas_call(
        body, name=name, grid=(n,),
        in_specs=[pl.BlockSpec((cs, hk), lambda i, b=b: (rev(i), b)) for _, b in seqs]
        + [row_spec(a) for a in rows] + [_full_spec(a) for a in consts]
        + [pl.BlockSpec((1, n_heads, HEAD_DIM, HEAD_DIM), lambda i: (rev(i), 0, 0, 0)),
           pl.BlockSpec((cs, hk), lambda i, b=d_out[1]: (rev(i), b))] + [_ANY] * nx,
        out_specs=[pl.BlockSpec((cs, ns * hk), lambda i: (rev(i), 0))]
        + [row_spec(a) for a in rows[:n_row_grads]] + [_full_spec(a) for a in consts[:n_const_grads]] + [_ANY] * nx,
        out_shape=[jax.ShapeDtypeStruct((t, ns * hk), dseq_dtype)]
        + [jax.ShapeDtypeStruct(a.shape, F32) for a in rows[:n_row_grads]]
        + [jax.ShapeDtypeStruct(a.shape, F32) for a in consts[:n_const_grads]] + [ex.dst for ex in comm],
        scratch_shapes=[pltpu.VMEM((n_heads, HEAD_DIM, HEAD_DIM), F32)] + _EXCHANGE_SEMS * nx,
        compiler_params=_params("arbitrary"),
    )(*[a for a, _ in seqs], *rows, *consts, states, d_out[0], *[ex.src for ex in comm])
    return (res[0], list(res[1:1 + n_row_grads]), list(res[1 + n_row_grads:n_grads]), list(res[n_grads:]))


def _gdn_make_chunk(row_vals, const_vals):
    beta_r, g_r, *saved = row_vals
    (onw,) = const_vals
    t_saved = saved[0] if saved else None
    return lambda state, qc, kc, vc, z: _gdn_chunk(state, qc, kc, vc, z, beta_r, g_r, onw, t_saved)


def _ret_make_chunk(row_vals, const_vals):
    cos2, sin2 = row_vals
    d_mat, dec_q, dec_k, dec_c = const_vals
    return _make_ret_chunk(cos2, sin2, d_mat, dec_q, dec_k, dec_c)


def _adamw_math(w, g, m, v):
    m = ADAM_B1 * m + (1.0 - ADAM_B1) * g
    v = ADAM_B2 * v + (1.0 - ADAM_B2) * jnp.square(g)
    m_hat = m / (1.0 - ADAM_B1 ** ADAM_STEP)
    v_hat = v / (1.0 - ADAM_B2 ** ADAM_STEP)
    delta = -ADAM_LR * (m_hat / (jnp.sqrt(v_hat) + ADAM_EPS) + ADAM_WD * w)
    return delta, m, v


def _sum_parts(parts, name):
    n_parts, rows, cols = parts.shape
    tb = _row_tile(rows, 512)

    def body(p_ref, o_ref):
        g = p_ref[0].astype(F32)
        for s in range(1, n_parts):
            g = g + p_ref[s].astype(F32)
        o_ref[...] = g

    return _pallas_call(
        body, name=name, grid=(rows // tb,),
        in_specs=[pl.BlockSpec((n_parts, tb, cols), lambda i: (0, i, 0))],
        out_specs=pl.BlockSpec((tb, cols), lambda i: (i, 0)),
        out_shape=jax.ShapeDtypeStruct((rows, cols), F32),
        compiler_params=_params("parallel"),
    )(parts)


def _adamw(parts, w, m, v, name, layer=None):
    rows, cols = w.shape[-2:]
    n_parts = parts.shape[0]
    tb = _row_tile(rows, max(8, (1 << 17) // cols // 8 * 8))

    def body(p_ref, w_ref, m_ref, v_ref, g_out, d_out, m_out, v_out):
        g = p_ref[0].astype(F32)
        for s in range(1, n_parts):
            g = g + p_ref[s].astype(F32)
        delta, m_new, v_new = _adamw_math(w_ref[...], g, m_ref[...], v_ref[...])
        g_out[...] = g
        d_out[...] = delta
        m_out[...] = m_new
        v_out[...] = v_new

    spec = pl.BlockSpec((tb, cols), lambda i: (i, 0))
    w_spec = spec if layer is None else pl.BlockSpec((None, tb, cols), lambda i: (layer, i, 0))
    return _pallas_call(
        body, name=name, grid=(rows // tb,),
        in_specs=[pl.BlockSpec((n_parts, tb, cols), lambda i: (0, i, 0)), w_spec, w_spec, w_spec],
        out_specs=[spec] * 4, out_shape=[jax.ShapeDtypeStruct((rows, cols), F32)] * 4,
        compiler_params=_params("parallel"),
    )(parts, w, m, v)


def _pack(arrays, rows_multiple=8):
    flat = jnp.concatenate([a.reshape(-1).astype(F32) for a in arrays])
    n = flat.shape[0]
    rows = -(-n // LANES)
    rows = -(-rows // rows_multiple) * rows_multiple
    return jnp.pad(flat, (0, rows * LANES - n)).reshape(rows, LANES)


def _unpack(packed, shapes):
    flat = packed.reshape(-1)
    out, o = [], 0
    for s in shapes:
        n = int(np.prod(s))
        out.append(flat[o:o + n].reshape(s))
        o += n
    return out


def _gather(shard):
    return _Exchange("gather", shard.astype(BF16))


def _rows_of(gathered):
    return gathered.reshape(gathered.shape[0] * gathered.shape[1], gathered.shape[2])


def _scatter_rows(full):
    k, n = full.shape
    return _Exchange("scatter", full.reshape(N_DEV, k // N_DEV, n))


def _relu2_epilogue(u):
    return u, jnp.square(jax.nn.relu(u))


def _relu2_vjp_epilogue(dr, u):
    return (dr * (2.0 * jax.nn.relu(u.astype(F32))),)


def kernel(x, norm_w, la_w_in, la_conv_w, la_a_log, la_dt_bias, la_out_norm_w, la_w_out, sg_w_in, sg_ln_w, sg_ln_b, sg_w_s, sg_b_s, sg_w_out, ffn_w_up, ffn_w_down, loss_target, m_norm_w, m_la_w_in, m_la_conv_w, m_la_a_log, m_la_dt_bias, m_la_out_norm_w, m_la_w_out, m_sg_w_in, m_sg_ln_w, m_sg_ln_b, m_sg_w_s, m_sg_b_s, m_sg_w_out, m_ffn_w_up, m_ffn_w_down, v_norm_w, v_la_w_in, v_la_conv_w, v_la_a_log, v_la_dt_bias, v_la_out_norm_w, v_la_w_out, v_sg_w_in, v_sg_ln_w, v_sg_ln_b, v_sg_w_s, v_sg_b_s, v_sg_w_out, v_ffn_w_up, v_ffn_w_down):
    weights = dict(norm_w=norm_w, la_w_in=la_w_in, la_conv_w=la_conv_w, la_a_log=la_a_log, la_dt_bias=la_dt_bias,
                   la_out_norm_w=la_out_norm_w, la_w_out=la_w_out, sg_w_in=sg_w_in, sg_ln_w=sg_ln_w, sg_ln_b=sg_ln_b,
                   sg_w_s=sg_w_s, sg_b_s=sg_b_s, sg_w_out=sg_w_out, ffn_w_up=ffn_w_up, ffn_w_down=ffn_w_down)
    mom1 = dict(norm_w=m_norm_w, la_w_in=m_la_w_in, la_conv_w=m_la_conv_w, la_a_log=m_la_a_log, la_dt_bias=m_la_dt_bias,
                la_out_norm_w=m_la_out_norm_w, la_w_out=m_la_w_out, sg_w_in=m_sg_w_in, sg_ln_w=m_sg_ln_w, sg_ln_b=m_sg_ln_b,
                sg_w_s=m_sg_w_s, sg_b_s=m_sg_b_s, sg_w_out=m_sg_w_out, ffn_w_up=m_ffn_w_up, ffn_w_down=m_ffn_w_down)
    mom2 = dict(norm_w=v_norm_w, la_w_in=v_la_w_in, la_conv_w=v_la_conv_w, la_a_log=v_la_a_log, la_dt_bias=v_la_dt_bias,
                la_out_norm_w=v_la_out_norm_w, la_w_out=v_la_w_out, sg_w_in=v_sg_w_in, sg_ln_w=v_sg_ln_w, sg_ln_b=v_sg_ln_b,
                sg_w_s=v_sg_w_s, sg_b_s=v_sg_b_s, sg_w_out=v_sg_w_out, ffn_w_up=v_ffn_w_up, ffn_w_down=v_ffn_w_down)
    names = list(weights)

    t, d = x.shape[1], x.shape[2]
    n_heads = la_a_log.shape[-1]
    hk = n_heads * HEAD_DIM
    cs = LA_CHUNK
    n_chunks = t // cs
    sg_width = sg_w_out.shape[1] * N_DEV
    me = 4 * lax.axis_index("x") + 2 * lax.axis_index("y") + lax.axis_index("c")
    xs = x.reshape(t, d)
    tgt = loss_target.reshape(t, d)

    conv_t = lambda a: jnp.transpose(a[0])
    small_local = lambda src: [src["norm_w"], conv_t(src["la_conv_w"]), src["sg_ln_w"], src["sg_ln_b"]]
    small_all = _all_gather(_pack(small_local(weights)), "ag_small")
    dsh = d // N_DEV
    csh = la_conv_w.shape[1]
    wsh = sg_width // N_DEV
    nw_parts, conv_parts, lnw_parts, lnb_parts = zip(*[
        _unpack(small_all[j], [(2, 4, dsh), (CONV_WIDTH, csh), (wsh,), (wsh,)]) for j in range(N_DEV)])
    nw = jnp.concatenate(nw_parts, axis=-1)
    conv_w = jnp.concatenate(conv_parts, axis=1)
    ln_w = jnp.concatenate(lnw_parts)[None, :]
    ln_b = jnp.concatenate(lnb_parts)[None, :]
    nrm = lambda l, j: nw[l, j][None, :]

    tb_d = _row_tile(t, 256)
    tb_wide = _row_tile(t, 64)

    y0, w_in_g = _rowwise(_seg_norm, [xs], [nrm(0, 0)], [(d, BF16)], tb_d, "f_norm0",
                          comm=[_gather(la_w_in[0])])
    in_cols = w_in_g.shape[2]
    gate_lo, gate_hi = 4 * hk, 4 * hk + 2 * n_heads
    main_pieces, gate_pieces = [], []
    for j in range(N_DEV):
        lo, hi = j * in_cols, (j + 1) * in_cols
        for s, e, dest in ((lo, min(hi, gate_lo), main_pieces), (max(lo, gate_lo), min(hi, gate_hi), gate_pieces),
                           (max(lo, gate_hi), hi, main_pieces)):
            if e > s:
                dest.append(w_in_g[j][:, s - lo:e - lo])
    w_main = jnp.concatenate(main_pieces, axis=1)
    w_gate = jnp.pad(jnp.concatenate(gate_pieces, axis=1), ((0, 0), (0, LANES - 2 * n_heads)))
    proj, w_up0_g = _matmul(y0, w_main, "nn", F32, "f_proj",
                            comm=[_gather(ffn_w_up[0])])
    gate_raw = _matmul(y0, w_gate, "nn", F32, "f_gate")
    a_log_v = jnp.pad(la_a_log.reshape(1, n_heads), ((0, 0), (n_heads, LANES - 2 * n_heads)))
    dt_bias_v = jnp.pad(la_dt_bias.reshape(1, n_heads), ((0, 0), (n_heads, LANES - 2 * n_heads)))
    seg_gates = _make_seg_gates(n_heads)
    (bg,) = _rowwise(seg_gates, [gate_raw], [a_log_v, dt_bias_v], [(LANES, F32)], tb_d, "f_gates")
    to_rows = lambda a: jnp.transpose(a.reshape(n_chunks, cs, n_heads), (0, 2, 1))[:, :, None, :]
    beta_r, g_r = to_rows(bg[:, :n_heads]), to_rows(bg[:, n_heads:2 * n_heads])
    qkvc = _conv_silu(proj, conv_w, 3 * hk, "f_conv")
    onw = la_out_norm_w.reshape(1, 1, HEAD_DIM)
    gdn_seqs = [(qkvc, 0), (qkvc, 1), (qkvc, 2), (proj, 3)]
    o_a, gdn_states, gdn_t_inv, w_down0_g = _scan_fwd(
        _gdn_make_chunk, gdn_seqs, [beta_r, g_r], [onw], n_heads, "f_gdn", comm=[_gather(ffn_w_down[0])],
        saved_shape=(n_heads, cs, cs))

    pos = jnp.arange(t, dtype=F32)
    inv_freq = 1.0 / (ROPE_BASE ** jnp.linspace(0.0, 1.0, HEAD_DIM // 2, dtype=F32))
    ang = pos[:, None] * inv_freq[None, :]
    cos2 = jnp.concatenate([jnp.cos(ang), jnp.cos(ang)], axis=-1).reshape(n_chunks, cs, HEAD_DIM)
    sin2 = jnp.concatenate([-jnp.sin(ang), jnp.sin(ang)], axis=-1).reshape(n_chunks, cs, HEAD_DIM)
    log_gamma = jnp.log1p(-jnp.power(2.0, -5.0 - jnp.arange(n_heads, dtype=F32)))
    cpos = jnp.arange(cs, dtype=F32)
    rel = cpos[:, None] - cpos[None, :]
    d_mat = jnp.where(rel >= 0, jnp.exp(jnp.where(rel >= 0, rel, 0.0) * log_gamma[:, None, None]), 0.0)
    dec_q = jnp.exp((cpos + 1.0) * log_gamma[:, None])[..., None]
    dec_k = jnp.exp((cs - 1.0 - cpos) * log_gamma[:, None])[..., None]
    dec_c = jnp.exp(cs * log_gamma)[:, None, None]
    ret_seqs = [(proj, 4), (proj, 5), (proj, 6), (proj, 7)]
    ret_consts = [d_mat, dec_q, dec_k, dec_c]
    o_b, ret_states, w_la_out_g = _scan_fwd(_ret_make_chunk, ret_seqs, [cos2, sin2], ret_consts, n_heads, "f_ret",
                                            comm=[_gather(la_w_out[0])])
    o_mix = jnp.concatenate([o_a, o_b], axis=1)
    y1 = _matmul(o_mix, _rows_of(w_la_out_g), "nn", F32, "f_la_out")
    h1, a1 = _rowwise(_seg_residual, [xs, y1], [nrm(0, 1), nrm(0, 2)], [(d, F32), (d, BF16)], tb_d, "f_res0a")
    u1, r1, w_sg_in_g = _matmul(a1, w_up0_g, "nn", None, "f_up0", b_blocked=True, epilogue=_relu2_epilogue,
                                out_dtypes=[F32, BF16], comm=[_gather(sg_w_in[0])])
    y2, w_sg_out_g = _matmul(r1, _rows_of(w_down0_g), "nn", F32, "f_down0", comm=[_gather(sg_w_out[0])])
    h2, a2 = _rowwise(_seg_residual, [h1, y2], [nrm(0, 3), nrm(1, 0)], [(d, F32), (d, BF16)], tb_d, "f_res0b")

    p_sg, w_up1_g = _matmul(a2, w_sg_in_g, "nn", F32, "f_sg_in", b_blocked=True,
                            comm=[_gather(ffn_w_up[1])])
    ua, vn = _gelu_ln(p_sg, ln_w, ln_b, tb_wide, "f_gelu_ln")
    ws = sg_w_s[0]
    bs = sg_b_s[0][:, :, None]
    gated = _spatial_gate(ua, vn, ws, bs, "f_sgate")
    y3 = _matmul(gated, _rows_of(w_sg_out_g), "nn", F32, "f_sg_out")
    h3, a3 = _rowwise(_seg_residual, [h2, y3], [nrm(1, 1), nrm(1, 2)], [(d, F32), (d, BF16)], tb_d, "f_res1a")
    u2, r2, w_down1_g = _matmul(a3, w_up1_g, "nn", None, "f_up1", b_blocked=True, epilogue=_relu2_epilogue,
                                out_dtypes=[F32, BF16], comm=[_gather(ffn_w_down[1])])
    y4 = _matmul(r2, _rows_of(w_down1_g), "nn", F32, "f_down1")

    ones = jnp.ones((t, 1), F32)
    (dh3, dy4), (dnw13,), loss_rows = _rowwise_vjp(
        _seg_loss, [h3, y4, tgt], [nrm(1, 3)], [ones], [F32, BF16, None], tb_d, "b_loss", primal_out=0)
    loss = lax.psum(jnp.sum(loss_rows), ("x", "y", "c"))

    du2 = _matmul(dy4, _rows_of(w_down1_g), "nt", BF16, "b_down1", extra=[u2], epilogue=_relu2_vjp_epilogue)
    dw_down1 = _matmul(r2, dy4, "tn", BF16, "b_dw_down1")
    da3, parts_down1 = _matmul(du2, w_up1_g, "nt", F32, "b_up1", b_blocked=True, comm=[_scatter_rows(dw_down1)])
    dw_up1 = _matmul(a3, du2, "tn", BF16, "b_dw_up1", out_blocked=True)
    (dh2, dy3), (dnw11, dnw12) = _rowwise_vjp(
        _seg_residual, [h2, y3], [nrm(1, 1), nrm(1, 2)], [dh3, da3], [F32, BF16], tb_d, "b_res1a")
    dgated = _matmul(dy3, _rows_of(w_sg_out_g), "nt", F32, "b_sg_out")
    dw_sg_out = _matmul(gated, dy3, "tn", BF16, "b_dw_sg_out")
    dua, dvn, dws, dbs = _spatial_gate_vjp(ua, vn, ws, bs, dgated, "b_sgate")
    dp_sg, dln_w, dln_b = _gelu_ln_vjp(p_sg, ln_w, dua, dvn, tb_wide, "b_gelu_ln")
    da2, parts_sg_out = _matmul(dp_sg, w_sg_in_g, "nt", F32, "b_sg_in", b_blocked=True,
                                comm=[_scatter_rows(dw_sg_out)])
    dw_sg_in, parts_up1 = _matmul(a2, dp_sg, "tn", BF16, "b_dw_sg_in", out_blocked=True,
                                  comm=[_Exchange("scatter", dw_up1)])
    (dh1, dy2), (dnw03, dnw10) = _rowwise_vjp(
        _seg_residual, [h1, y2], [nrm(0, 3), nrm(1, 0)], [dh2, da2], [F32, BF16], tb_d, "b_res0b")
    du1, parts_sg_in = _matmul(dy2, _rows_of(w_down0_g), "nt", BF16, "b_down0", extra=[u1],
                               epilogue=_relu2_vjp_epilogue, comm=[_Exchange("scatter", dw_sg_in)])
    dw_down0 = _matmul(r1, dy2, "tn", BF16, "b_dw_down0")
    da1, parts_down0 = _matmul(du1, w_up0_g, "nt", F32, "b_up0", b_blocked=True, comm=[_scatter_rows(dw_down0)])
    dw_up0 = _matmul(a1, du1, "tn", BF16, "b_dw_up0", out_blocked=True)
    (dx_res, dy1), (dnw01, dnw02) = _rowwise_vjp(
        _seg_residual, [xs, y1], [nrm(0, 1), nrm(0, 2)], [dh1, da1], [F32, BF16], tb_d, "b_res0a")
    do_mix = _matmul(dy1, _rows_of(w_la_out_g), "nt", F32, "b_la_out")
    dw_la_out = _matmul(o_mix, dy1, "tn", BF16, "b_dw_la_out")

    d_ret, _, _, _ = _scan_bwd(_ret_make_chunk, ret_seqs, [cos2, sin2], ret_consts, ret_states, (do_mix, 1),
                               n_heads, 0, 0, BF16, "b_ret")
    d_gdn, (dbeta_r, dg_r), (donw,), (parts_up0, parts_la_out) = _scan_bwd(
        _gdn_make_chunk, gdn_seqs, [beta_r, g_r, gdn_t_inv], [onw], gdn_states, (do_mix, 0), n_heads, 2, 1, F32, "b_gdn",
        comm=[_Exchange("scatter", dw_up0), _scatter_rows(dw_la_out)])
    from_rows = lambda a: jnp.transpose(a[:, :, 0, :], (0, 2, 1)).reshape(t, n_heads)
    dbg = jnp.pad(jnp.concatenate([from_rows(dbeta_r), from_rows(dg_r)], axis=1), ((0, 0), (0, LANES - 2 * n_heads)))
    (dgate_raw,), (da_log_v, ddt_bias_v) = _rowwise_vjp(
        seg_gates, [gate_raw], [a_log_v, dt_bias_v], [dbg], [BF16], tb_d, "b_gates")
    dqkv, dconv_w = _conv_silu_vjp(proj, conv_w, d_gdn, 3 * hk, "b_conv")
    dproj = jnp.concatenate([dqkv, d_gdn[:, 3 * hk:].astype(BF16), d_ret], axis=1)
    dw_main = _matmul(y0, dproj, "tn", BF16, "b_dw_proj")
    dw_gate = _matmul(y0, dgate_raw, "tn", BF16, "b_dw_gate")
    blocks = []
    for j in range(N_DEV):
        lo, hi = j * in_cols, (j + 1) * in_cols
        g_s = min(max(lo, gate_lo), gate_hi)
        g_e = max(g_s, min(hi, gate_hi))
        parts = [dw_main[:, lo:max(lo, min(hi, gate_lo))], dw_gate[:, g_s - gate_lo:g_e - gate_lo],
                 dw_main[:, max(lo, gate_hi) - 2 * n_heads:max(hi, gate_hi) - 2 * n_heads]]
        blocks.append(jnp.concatenate([p for p in parts if p.shape[1] > 0], axis=1))
    dw_in_blocks = jnp.stack(blocks)
    dy0, parts_in = _matmul(dproj, w_main, "nt", F32, "b_proj", comm=[_Exchange("scatter", dw_in_blocks)])
    dy0_gate = _matmul(dgate_raw, w_gate, "nt", F32, "b_gate")
    (grad_x,), (dnw00,) = _rowwise_vjp(
        _seg_norm, [xs], [nrm(0, 0)], [(dy0, dy0_gate)], [F32], tb_d, "b_norm0", adds=[dx_res])

    outs = {}
    for name, parts in (("la_w_in", parts_in), ("la_w_out", parts_la_out), ("sg_w_in", parts_sg_in),
                        ("sg_w_out", parts_sg_out)):
        res = _adamw(parts, weights[name], mom1[name], mom2[name], f"adamw_{name}", layer=0)
        outs[name] = [r[None] for r in res]
    for name, parts_l in (("ffn_w_up", [parts_up0, parts_up1]), ("ffn_w_down", [parts_down0, parts_down1])):
        res_l = [_adamw(parts_l[l], weights[name], mom1[name], mom2[name], f"adamw_{name}{l}", layer=l)
                 for l in range(2)]
        outs[name] = [jnp.stack([res_l[0][j], res_l[1][j]]) for j in range(4)]

    dnorm = jnp.stack([jnp.concatenate([dnw00, dnw01, dnw02, dnw03], axis=0),
                       jnp.concatenate([dnw10, dnw11, dnw12, dnw13], axis=0)])
    small_grads = [dnorm, dconv_w, dln_w[0], dln_b[0],
                   da_log_v[:, n_heads:2 * n_heads], ddt_bias_v[:, n_heads:2 * n_heads],
                   donw.reshape(1, HEAD_DIM), dws[None], dbs[None, :, :, 0]]
    small_parts = _all_gather(_pack(small_grads), "ag_small_grads")
    gs = _unpack(_sum_parts(small_parts, "sum_small_grads"), [g.shape for g in small_grads])
    sharded = lambda full, axis, size: lax.dynamic_slice_in_dim(full, me * size, size, axis)
    own = [sharded(gs[0], 2, dsh), sharded(gs[1], 1, csh), sharded(gs[2], 0, wsh)[None], sharded(gs[3], 0, wsh)[None]]
    own += gs[4:]
    replicated = ["la_a_log", "la_dt_bias", "la_out_norm_w", "sg_w_s", "sg_b_s"]
    small_names = ["norm_w", "la_conv_w", "sg_ln_w", "sg_ln_b"] + replicated
    pk = lambda src: _pack(small_local(src) + [src[n] for n in replicated])
    res = _adamw(_pack(own)[None], pk(weights), pk(mom1), pk(mom2), "adamw_small")
    for j, r in enumerate(res):
        vals = _unpack(r, [g.shape for g in own])
        vals[1] = jnp.transpose(vals[1])[None]
        for n, val in zip(small_names, vals):
            outs.setdefault(n, [None] * 4)[j] = val

    grad_x = grad_x[None]
    return (loss, grad_x, *[outs[n][0] for n in names], *[outs[n][1] for n in names],
            *[outs[n][2] for n in names], *[outs[n][3] for n in names])
```

```python
import functools
import math

import jax
import jax.numpy as jnp
import numpy as np
from jax import lax
from jax.experimental import pallas as pl
from jax.experimental.pallas import tpu as pltpu

F32, BF16 = jnp.float32, jnp.bfloat16
_pallas_call = pl.pallas_call

N_DEV = 8
LANES = 128
V7X_VMEM_BYTES = 64 * 1024 * 1024
VMEM_LIMIT = (V7X_VMEM_BYTES * 3) // 4
HEAD_DIM = 128
LA_CHUNK = 64
INV_BLOCK = 8
SG_CHUNK = 128
SG_STEP_CHUNKS = 8
SG_GROUPS = 8
CONV_WIDTH = 4
HALO = 8
ROPE_BASE = 10000.0
EPS = 1e-6
ADAM_LR, ADAM_B1, ADAM_B2, ADAM_EPS, ADAM_WD, ADAM_STEP = 0.001, 0.9, 0.999, 1e-08, 0.01, 10
MESH_ID = pl.DeviceIdType.MESH


def _params(*sem):
    return pltpu.CompilerParams(dimension_semantics=sem or None, vmem_limit_bytes=VMEM_LIMIT)


def _tile(dim, pref):
    if dim <= pref:
        return dim
    t = (pref // LANES) * LANES
    while dim % t:
        t -= LANES
    return t


def _row_tile(rows, pref):
    t = min(rows, pref)
    while rows % t:
        t -= 8
    return t


def _all_gather(shard, name):
    def body(x_ref, out_ref, send_sems, recv_sems, local_sem):
        x, y, c = lax.axis_index("x"), lax.axis_index("y"), lax.axis_index("c")
        me, sibling = (x, y, c), (x, y, 1 - c)
        chips = [(1 - x, y), (x, 1 - y), (1 - x, 1 - y)]

        def rows(px, py, pc):
            return out_ref.at[4 * px + 2 * py + pc]

        def copy(k, block, to, src=None):
            return pltpu.make_async_remote_copy(
                src_ref=rows(*block) if src is None else src, dst_ref=rows(*block),
                send_sem=send_sems.at[k], recv_sem=recv_sems.at[k], device_id=to, device_id_type=MESH_ID)

        mine = pltpu.make_async_copy(x_ref, rows(*me), local_sem)
        mine.start()
        first = [copy(0, me, sibling, src=x_ref)]
        first += [copy(1 + j, me, (*chip, c), src=x_ref) for j, chip in enumerate(chips)]
        for cp in first:
            cp.start()
        passed = [copy(4 + j, (*chip, c), sibling) for j, chip in enumerate(chips)]
        for j, chip in enumerate(chips):
            copy(1 + j, (*chip, c), me).wait_recv()
            passed[j].start()
        copy(0, sibling, me).wait_recv()
        for j, chip in enumerate(chips):
            copy(4 + j, (*chip, 1 - c), me).wait_recv()
        for cp in first + passed:
            cp.wait_send()
        mine.wait()

    return _pallas_call(
        body, name=name,
        out_shape=jax.ShapeDtypeStruct((N_DEV,) + shard.shape, shard.dtype),
        in_specs=[pl.BlockSpec(memory_space=pl.ANY)],
        out_specs=pl.BlockSpec(memory_space=pl.ANY),
        scratch_shapes=[pltpu.SemaphoreType.DMA((7,)), pltpu.SemaphoreType.DMA((7,)), pltpu.SemaphoreType.DMA],
    )(shard)


class _Exchange:
    def __init__(self, kind, src):
        self.kind, self.src = kind, src
        shape = (N_DEV,) + src.shape if kind == "gather" else src.shape
        self.dst = jax.ShapeDtypeStruct(shape, src.dtype)

    def _scatter_copies(self, src_ref, dst_ref, send_sems, recv_sems, local_sem):
        x, y, c = lax.axis_index("x"), lax.axis_index("y"), lax.axis_index("c")
        me = 4 * x + 2 * y + c
        local = pltpu.make_async_copy(src_ref.at[me], dst_ref.at[me], local_sem)
        remote = []
        for k in range(1, N_DEV):
            px, py, pc = x ^ ((k >> 2) & 1), y ^ ((k >> 1) & 1), c ^ (k & 1)
            remote.append(pltpu.make_async_remote_copy(
                src_ref=src_ref.at[4 * px + 2 * py + pc], dst_ref=dst_ref.at[me],
                send_sem=send_sems.at[k - 1], recv_sem=recv_sems.at[k - 1],
                device_id=(px, py, pc), device_id_type=MESH_ID))
        return local, remote

    def _gather_copies(self, src_ref, dst_ref, send_sems, recv_sems, local_sem):
        x, y, c = lax.axis_index("x"), lax.axis_index("y"), lax.axis_index("c")
        me, sibling = (x, y, c), (x, y, 1 - c)
        chips = [(1 - x, y), (x, 1 - y), (1 - x, 1 - y)]
        rows = lambda px, py, pc: dst_ref.at[4 * px + 2 * py + pc]

        def copy(k, block, to, src=None):
            return pltpu.make_async_remote_copy(
                src_ref=rows(*block) if src is None else src, dst_ref=rows(*block),
                send_sem=send_sems.at[k], recv_sem=recv_sems.at[k], device_id=to, device_id_type=MESH_ID)

        makers = {
            "local": lambda: pltpu.make_async_copy(src_ref, rows(*me), local_sem),
            "own": lambda: [copy(0, me, sibling, src=src_ref)] + [copy(1 + j, me, (*chip, c), src=src_ref)
                                                                  for j, chip in enumerate(chips)],
            "passed": lambda: [copy(4 + j, (*chip, c), sibling) for j, chip in enumerate(chips)],
            "landed": lambda: [copy(1 + j, (*chip, c), me) for j, chip in enumerate(chips)],
            "from_sibling": lambda: [copy(0, sibling, me)] + [copy(4 + j, (*chip, 1 - c), me)
                                                              for j, chip in enumerate(chips)],
        }
        return lambda *names: [makers[n]() for n in names]

    def start(self, *refs):
        if self.kind == "gather":
            local, own = self._gather_copies(*refs)("local", "own")
        else:
            local, own = self._scatter_copies(*refs)
        local.start()
        for cp in own:
            cp.start()

    def middle(self, *refs):
        if self.kind == "gather":
            landed, passed = self._gather_copies(*refs)("landed", "passed")
            for arrived, forward in zip(landed, passed):
                arrived.wait_recv()
                forward.start()

    def finish(self, *refs):
        if self.kind == "gather":
            local, own, passed, from_sibling = self._gather_copies(*refs)("local", "own", "passed", "from_sibling")
            for cp in from_sibling:
                cp.wait_recv()
            for cp in own + passed:
                cp.wait_send()
        else:
            local, remote = self._scatter_copies(*refs)
            for cp in remote:
                cp.wait_recv()
            for cp in remote:
                cp.wait_send()
        local.wait()


_ANY = pl.BlockSpec(memory_space=pl.ANY)
_EXCHANGE_SEMS = [pltpu.SemaphoreType.DMA((N_DEV - 1,)), pltpu.SemaphoreType.DMA((N_DEV - 1,)), pltpu.SemaphoreType.DMA]


def _carry(exchanges, src_refs, dst_refs, sem_refs, first, middle, last, compute):
    def each():
        for e, (ex, s, d) in enumerate(zip(exchanges, src_refs, dst_refs)):
            yield ex, (s, d, *sem_refs[3 * e:3 * e + 3])

    if exchanges:
        @pl.when(first)
        def _():
            for ex, refs in each():
                ex.start(*refs)

    compute()

    if any(ex.kind == "gather" for ex in exchanges):
        @pl.when(middle)
        def _():
            for ex, refs in each():
                ex.middle(*refs)

    if exchanges:
        @pl.when(last)
        def _():
            for ex, refs in each():
                ex.finish(*refs)


def _matmul(a, b, mode, out_dtype, name, b_blocked=False, out_blocked=False, extra=(), epilogue=None,
            out_dtypes=None, comm=()):
    if mode == "nn":
        (m, k), n = a.shape, b.shape[-1] * (N_DEV if b_blocked else 1)
    elif mode == "nt":
        (m, k), n = a.shape, b.shape[-2]
    else:
        (k, m), n = a.shape, b.shape[1]
    tm, tn, tk = _tile(m, 1024), _tile(n, 1024), _tile(k, 2048)
    if (b_blocked and mode == "nn") or out_blocked:
        tn = n // N_DEV
    span = 1
    if b_blocked and mode == "nt":
        kb = k // N_DEV
        span = max(1, min(N_DEV, 2048 // kb))
        tk = span * kb
    nj, ni, nk = n // tn, m // tm, k // tk
    dims = {"nn": (((1,), (0,)), ((), ())), "nt": (((1,), (1,)), ((), ())), "tn": (((0,), (0,)), ((), ()))}[mode]
    out_dtypes = out_dtypes or [out_dtype]
    ne, nc, no = len(extra), len(comm), len(out_dtypes)

    def body(*refs):
        a_ref, b_refs = refs[0], refs[1:1 + span]
        refs = refs[span - 1:]
        extra_refs, src_refs = refs[2:2 + ne], refs[2 + ne:2 + ne + nc]
        o = 2 + ne + nc
        out_refs, dst_refs = refs[o:o + no], refs[o + no:o + no + nc]
        rest = refs[o + no + nc:]
        acc, sems = (rest[0], rest[1:]) if nk > 1 else (None, rest)
        j, i, kk = pl.program_id(0), pl.program_id(1), pl.program_id(2)

        def finish(res):
            outs = epilogue(res, *[r[...] for r in extra_refs]) if epilogue else (res,)
            for ref, val in zip(out_refs, outs):
                ref[...] = val.astype(ref.dtype)

        def compute():
            a_tile = a_ref[...].astype(BF16)
            kw = a_tile.shape[1] // span if mode == "nt" else None
            prod = None
            for s, b_ref in enumerate(b_refs):
                a_part = a_tile if span == 1 else a_tile[:, s * kw:(s + 1) * kw]
                part = lax.dot_general(a_part, b_ref[...].astype(BF16), dims, preferred_element_type=F32)
                prod = part if prod is None else prod + part
            if nk == 1:
                finish(prod)
                return

            @pl.when(kk == 0)
            def _():
                acc[...] = prod

            @pl.when(kk > 0)
            def _():
                acc[...] += prod

            @pl.when(kk == nk - 1)
            def _():
                finish(acc[...])

        step = (j * ni + i) * nk + kk
        _carry(comm, src_refs, dst_refs, sems, step == 0, step == (nj * ni * nk) // 2, step == nj * ni * nk - 1, compute)

    a_spec = (pl.BlockSpec((tk, tm), lambda j, i, kk: (kk, i)) if mode == "tn"
              else pl.BlockSpec((tm, tk), lambda j, i, kk: (i, kk)))
    if b_blocked and mode == "nn":
        b_specs = [pl.BlockSpec((None, tk, tn), lambda j, i, kk: (j, kk, 0))]
    elif b_blocked:
        b_specs = [pl.BlockSpec((None, tn, tk // span), lambda j, i, kk, s=s: (kk * span + s, j, 0)) for s in range(span)]
    else:
        b_specs = [pl.BlockSpec((tn, tk), lambda j, i, kk: (j, kk)) if mode == "nt"
                   else pl.BlockSpec((tk, tn), lambda j, i, kk: (kk, j))]
    tile_spec = pl.BlockSpec((tm, tn), lambda j, i, kk: (i, j))
    if out_blocked:
        o_spec = pl.BlockSpec((None, tm, tn), lambda j, i, kk: (j, i, 0))
        o_shape = (N_DEV, m, tn)
    else:
        o_spec, o_shape = tile_spec, (m, n)
    res = _pallas_call(
        body, name=name, grid=(nj, ni, nk),
        in_specs=[a_spec] + b_specs + [tile_spec] * ne + [_ANY] * nc,
        out_specs=[o_spec] * no + [_ANY] * nc,
        out_shape=[jax.ShapeDtypeStruct(o_shape, d) for d in out_dtypes] + [ex.dst for ex in comm],
        scratch_shapes=([pltpu.VMEM((tm, tn), F32)] if nk > 1 else []) + _EXCHANGE_SEMS * nc,
        compiler_params=_params("arbitrary", "arbitrary", "arbitrary"),
    )(a, *[b] * span, *extra, *[ex.src for ex in comm])
    return res[0] if len(res) == 1 else res


def _col_spec(tb, spec):
    if isinstance(spec, tuple):
        arr, blk, width = spec
        return arr, pl.BlockSpec((tb, width), lambda i, blk=blk: (i, blk))
    return spec, pl.BlockSpec((tb, spec.shape[1]), lambda i: (i, 0))


def _full_spec(p):
    return pl.BlockSpec(p.shape, lambda i, nd=p.ndim: (0,) * nd)


def _rowwise(fn, xs, params, outs, tb, name, comm=()):
    arrs, specs = zip(*[_col_spec(tb, s) for s in xs])
    t = arrs[0].shape[0]
    nx, npar, no, nc = len(xs), len(params), len(outs), len(comm)
    nb = t // tb

    def body(*refs):
        src_refs = refs[nx + npar:nx + npar + nc]
        o = nx + npar + nc
        out_refs, dst_refs, sems = refs[o:o + no], refs[o + no:o + no + nc], refs[o + no + nc:]
        i = pl.program_id(0)

        def compute():
            res = fn(*[r[...] for r in refs[:nx + npar]])
            for o_ref, r in zip(out_refs, res):
                o_ref[...] = r.astype(o_ref.dtype)

        _carry(comm, src_refs, dst_refs, sems, i == 0, i == nb // 2, i == nb - 1, compute)

    return _pallas_call(
        body, name=name, grid=(nb,),
        in_specs=list(specs) + [_full_spec(p) for p in params] + [_ANY] * nc,
        out_specs=[pl.BlockSpec((tb, w), lambda i: (i, 0)) for w, _ in outs] + [_ANY] * nc,
        out_shape=[jax.ShapeDtypeStruct((t, w), d) for w, d in outs] + [ex.dst for ex in comm],
        scratch_shapes=_EXCHANGE_SEMS * nc,
        compiler_params=_params("arbitrary" if comm else "parallel"),
    )(*arrs, *params, *[ex.src for ex in comm])


def _rowwise_vjp(fn, xs, params, cts, dx_dtypes, tb, name, adds=None, primal_out=None, primal_width=1):
    arrs, specs = zip(*[_col_spec(tb, s) for s in xs])
    t = arrs[0].shape[0]
    ct_groups = [c if isinstance(c, tuple) else (c,) for c in cts]
    ct_flat = [a for grp in ct_groups for a in grp]
    nx, npar, nct = len(xs), len(params), len(ct_flat)
    adds = adds or [None] * nx
    add_ix = [i for i in range(nx) if adds[i] is not None]
    dx_ix = [i for i in range(nx) if dx_dtypes[i] is not None]
    widths = [s.block_shape[1] for s in specs]
    n_dx_out = len(dx_ix)

    def body(*refs):
        x_refs, p_refs = refs[:nx], refs[nx:nx + npar]
        ct_refs = list(refs[nx + npar:nx + npar + nct])
        add_refs = refs[nx + npar + nct:nx + npar + nct + len(add_ix)]
        o = nx + npar + nct + len(add_ix)
        dx_refs, dp_refs = refs[o:o + n_dx_out], refs[o + n_dx_out:o + n_dx_out + npar]
        prim, vjp = jax.vjp(fn, *[r[...] for r in x_refs], *[r[...] for r in p_refs])
        ct_vals = []
        for grp, p in zip(ct_groups, prim):
            val = ct_refs.pop(0)[...].astype(p.dtype)
            for _ in grp[1:]:
                val = val + ct_refs.pop(0)[...].astype(p.dtype)
            ct_vals.append(val)
        grads = vjp(tuple(ct_vals))
        for n, i in enumerate(dx_ix):
            g = grads[i].astype(F32)
            if adds[i] is not None:
                g = g + add_refs[add_ix.index(i)][...].astype(F32)
            dx_refs[n][...] = g.astype(dx_refs[n].dtype)

        @pl.when(pl.program_id(0) == 0)
        def _():
            for ref in dp_refs:
                ref[...] = jnp.zeros_like(ref)

        for ref, g in zip(dp_refs, grads[nx:]):
            ref[...] += g.astype(F32)
        if primal_out is not None:
            refs[-1][...] = prim[primal_out].astype(refs[-1].dtype)

    ct_specs = [pl.BlockSpec((tb, c.shape[1]), lambda i: (i, 0)) for c in ct_flat]
    add_specs = [pl.BlockSpec((tb, widths[i]), lambda i_: (i_, 0)) for i in add_ix]
    out_specs = [pl.BlockSpec((tb, widths[i]), lambda i_: (i_, 0)) for i in dx_ix]
    out_shape = [jax.ShapeDtypeStruct((t, widths[i]), dx_dtypes[i]) for i in dx_ix]
    out_specs += [_full_spec(p) for p in params]
    out_shape += [jax.ShapeDtypeStruct(p.shape, F32) for p in params]
    if primal_out is not None:
        out_specs.append(pl.BlockSpec((tb, primal_width), lambda i: (i, 0)))
        out_shape.append(jax.ShapeDtypeStruct((t, primal_width), F32))
    res = _pallas_call(
        body, name=name, grid=(t // tb,),
        in_specs=list(specs) + [_full_spec(p) for p in params] + ct_specs + add_specs,
        out_specs=out_specs, out_shape=out_shape,
        compiler_params=_params("arbitrary"),
    )(*arrs, *params, *ct_flat, *[adds[i] for i in add_ix])
    ndx = n_dx_out
    out = (list(res[:ndx]), list(res[ndx:ndx + npar]))
    return out + (res[-1],) if primal_out is not None else out


def _rms(x, w):
    xf = x.astype(F32)
    return xf * lax.rsqrt(jnp.mean(xf * xf, axis=-1, keepdims=True) + EPS) * w


def _seg_norm(x, w):
    return (_rms(x, w),)


def _seg_residual(h, y, w_post, w_pre):
    h2 = h.astype(F32) + _rms(y, w_post)
    return h2, _rms(h2, w_pre)


def _seg_loss(h, y, tgt, w_post):
    err = h.astype(F32) + _rms(y, w_post) - tgt
    return (0.5 * jnp.mean(err * err, axis=-1, keepdims=True),)


GELU_C, GELU_A = math.sqrt(2.0 / math.pi), 0.044715


def _gelu_parts(x):
    x2 = x * x
    th = jnp.tanh(GELU_C * x * (1.0 + GELU_A * x2))
    half = 0.5 * x
    return half * (1.0 + th), 0.5 * (1.0 + th) + half * (1.0 - th * th) * (GELU_C * (1.0 + 3.0 * GELU_A * x2))


def _row_mean(acc, width):
    return jnp.broadcast_to(jnp.sum(acc, axis=-1, keepdims=True) * (1.0 / width), acc.shape)


def _gelu_ln_stats(pv_ref, vbuf, gbuf, chunks, width):
    tb = pv_ref.shape[0]
    acc = jnp.zeros((tb, LANES), F32)
    for sl in chunks:
        v, g = _gelu_parts(pv_ref[:, sl])
        vbuf[:, sl] = v
        if gbuf is not None:
            gbuf[:, sl] = g
        acc = acc + v
    mu = _row_mean(acc, width)
    acc = jnp.zeros((tb, LANES), F32)
    for sl in chunks:
        dev = vbuf[:, sl] - mu
        acc = acc + dev * dev
    return mu, lax.rsqrt(_row_mean(acc, width) + EPS)


def _gelu_ln(p, ln_w, ln_b, tb, name):
    t, w = p.shape[0], p.shape[1] // 2
    chunks = [slice(j, j + LANES) for j in range(0, w, LANES)]

    def body(pu_ref, pv_ref, lnw_ref, lnb_ref, u_ref, vn_ref, vbuf):
        for sl in chunks:
            u_ref[:, sl] = _gelu_parts(pu_ref[:, sl])[0]
        mu, rstd = _gelu_ln_stats(pv_ref, vbuf, None, chunks, w)
        for sl in chunks:
            vn_ref[:, sl] = ((vbuf[:, sl] - mu) * rstd * lnw_ref[:, sl] + lnb_ref[:, sl]).astype(vn_ref.dtype)

    half = lambda b: pl.BlockSpec((tb, w), lambda i, b=b: (i, b))
    vec = pl.BlockSpec((1, w), lambda i: (0, 0))
    return _pallas_call(
        body, name=name, grid=(t // tb,),
        in_specs=[half(0), half(1), vec, vec], out_specs=[half(0), half(0)],
        out_shape=[jax.ShapeDtypeStruct((t, w), F32), jax.ShapeDtypeStruct((t, w), BF16)],
        scratch_shapes=[pltpu.VMEM((tb, w), F32)],
        compiler_params=_params("parallel"),
    )(p, p, ln_w, ln_b)


def _gelu_ln_vjp(p, ln_w, du, dvn, tb, name):
    t, w = p.shape[0], p.shape[1] // 2
    nb = t // tb
    chunks = [slice(j, j + LANES) for j in range(0, w, LANES)]
    fold = lambda a: jnp.sum(a.reshape(tb // 8, 8, LANES), axis=0)

    def body(pu_ref, pv_ref, lnw_ref, du_ref, dvn_ref, dp_ref, dlnw_ref, dlnb_ref, vbuf, gbuf, acc_w, acc_b):
        i = pl.program_id(0)

        @pl.when(i == 0)
        def _():
            acc_w[...] = jnp.zeros_like(acc_w)
            acc_b[...] = jnp.zeros_like(acc_b)

        for sl in chunks:
            dp_ref[:, sl] = (du_ref[:, sl] * _gelu_parts(pu_ref[:, sl])[1]).astype(dp_ref.dtype)
        mu, rstd = _gelu_ln_stats(pv_ref, vbuf, gbuf, chunks, w)
        m1 = jnp.zeros((tb, LANES), F32)
        m2 = jnp.zeros((tb, LANES), F32)
        for sl in chunks:
            xhat = (vbuf[:, sl] - mu) * rstd
            dy = dvn_ref[:, sl]
            dxhat = dy * lnw_ref[:, sl]
            m1 = m1 + dxhat
            m2 = m2 + dxhat * xhat
            acc_w[:, sl] += fold(dy * xhat)
            acc_b[:, sl] += fold(dy)
        m1, m2 = _row_mean(m1, w), _row_mean(m2, w)
        for j, sl in enumerate(chunks):
            xhat = (vbuf[:, sl] - mu) * rstd
            dv = rstd * (dvn_ref[:, sl] * lnw_ref[:, sl] - m1 - xhat * m2)
            dp_ref[:, w + j * LANES:w + (j + 1) * LANES] = (dv * gbuf[:, sl]).astype(dp_ref.dtype)

        @pl.when(i == nb - 1)
        def _():
            dlnw_ref[...] = jnp.sum(acc_w[...], axis=0, keepdims=True)
            dlnb_ref[...] = jnp.sum(acc_b[...], axis=0, keepdims=True)

    half = lambda b: pl.BlockSpec((tb, w), lambda i, b=b: (i, b))
    vec = pl.BlockSpec((1, w), lambda i: (0, 0))
    return _pallas_call(
        body, name=name, grid=(nb,),
        in_specs=[half(0), half(1), vec, half(0), half(0)],
        out_specs=[pl.BlockSpec((tb, 2 * w), lambda i: (i, 0)), vec, vec],
        out_shape=[jax.ShapeDtypeStruct((t, 2 * w), BF16), jax.ShapeDtypeStruct((1, w), F32),
                   jax.ShapeDtypeStruct((1, w), F32)],
        scratch_shapes=[pltpu.VMEM((tb, w), F32), pltpu.VMEM((tb, w), F32), pltpu.VMEM((8, w), F32),
                        pltpu.VMEM((8, w), F32)],
        compiler_params=_params("arbitrary"),
    )(p, p, ln_w, du, dvn)


def _make_seg_gates(n_heads):
    def seg(raw, a_log, dt_bias):
        lane = lax.broadcasted_iota(jnp.int32, raw.shape, 1)
        beta = jax.nn.sigmoid(raw)
        g = -jnp.exp(a_log) * jax.nn.softplus(raw + dt_bias)
        return (jnp.where(lane < n_heads, beta, jnp.where(lane < 2 * n_heads, g, 0.0)),)
    return seg


def _gate_fn(u, v, ws, bs):
    rows, gd = u.shape
    nb = rows // SG_CHUNK
    r = lax.broadcasted_iota(jnp.int32, (SG_CHUNK, SG_CHUNK), 0)
    c = lax.broadcasted_iota(jnp.int32, (SG_CHUNK, SG_CHUNK), 1)
    w = jnp.broadcast_to(jnp.where(r >= c, ws[0], 0.0).astype(BF16), (nb, SG_CHUNK, SG_CHUNK))
    v3 = v.astype(BF16).reshape(nb, SG_CHUNK, gd)
    s = lax.dot_general(w, v3, (((2,), (1,)), ((0,), (0,))), preferred_element_type=F32) + bs[0]
    return u.astype(F32) * s.reshape(rows, gd)


def _sg_rows(t):
    rows = SG_CHUNK * SG_STEP_CHUNKS
    while t % rows:
        rows -= SG_CHUNK
    return rows


def _sg_specs(rows, gd):
    x_spec = pl.BlockSpec((rows, gd), lambda g, i: (i, g))
    ws_spec = pl.BlockSpec((1, SG_CHUNK, SG_CHUNK), lambda g, i: (g, 0, 0))
    bs_spec = pl.BlockSpec((1, SG_CHUNK, 1), lambda g, i: (g, 0, 0))
    return x_spec, ws_spec, bs_spec


def _spatial_gate(u, v, ws, bs, name):
    t, w = u.shape
    gd = w // SG_GROUPS
    rows = _sg_rows(t)
    x_spec, ws_spec, bs_spec = _sg_specs(rows, gd)

    def body(u_ref, v_ref, ws_ref, bs_ref, o_ref):
        o_ref[...] = _gate_fn(u_ref[...], v_ref[...], ws_ref[...], bs_ref[...]).astype(o_ref.dtype)

    return _pallas_call(
        body, name=name, grid=(SG_GROUPS, t // rows),
        in_specs=[x_spec, x_spec, ws_spec, bs_spec], out_specs=x_spec,
        out_shape=jax.ShapeDtypeStruct((t, w), BF16),
        compiler_params=_params("parallel", "parallel"),
    )(u, v, ws, bs)


def _spatial_gate_vjp(u, v, ws, bs, ct, name, comm=()):
    t, w = u.shape
    gd = w // SG_GROUPS
    rows = _sg_rows(t)
    nchunks = t // rows
    nx = len(comm)
    x_spec, ws_spec, bs_spec = _sg_specs(rows, gd)

    def body(*refs):
        u_ref, v_ref, ws_ref, bs_ref, ct_ref = refs[:5]
        src_refs = refs[5:5 + nx]
        du_ref, dv_ref, dws_ref, dbs_ref = refs[5 + nx:9 + nx]
        dst_refs, sems = refs[9 + nx:9 + 2 * nx], refs[9 + 2 * nx:]
        g, i = pl.program_id(0), pl.program_id(1)

        def compute():
            _, vjp = jax.vjp(_gate_fn, u_ref[...], v_ref[...].astype(F32), ws_ref[...], bs_ref[...])
            du, dv, dws, dbs = vjp(ct_ref[...])
            du_ref[...] = du
            dv_ref[...] = dv

            @pl.when(i == 0)
            def _():
                dws_ref[...] = jnp.zeros_like(dws_ref)
                dbs_ref[...] = jnp.zeros_like(dbs_ref)

            dws_ref[...] += dws
            dbs_ref[...] += dbs

        step = g * nchunks + i
        _carry(comm, src_refs, dst_refs, sems, step == 0, step == (SG_GROUPS * nchunks) // 2,
               step == SG_GROUPS * nchunks - 1, compute)

    return _pallas_call(
        body, name=name, grid=(SG_GROUPS, nchunks),
        in_specs=[x_spec, x_spec, ws_spec, bs_spec, x_spec] + [_ANY] * nx,
        out_specs=[x_spec, x_spec, ws_spec, bs_spec] + [_ANY] * nx,
        out_shape=[jax.ShapeDtypeStruct((t, w), F32), jax.ShapeDtypeStruct((t, w), F32),
                   jax.ShapeDtypeStruct(ws.shape, F32), jax.ShapeDtypeStruct(bs.shape, F32)] + [ex.dst for ex in comm],
        scratch_shapes=_EXCHANGE_SEMS * nx,
        compiler_params=_params("arbitrary", "arbitrary"),
    )(u, v, ws, bs, ct, *[ex.src for ex in comm])


def _conv_silu(proj, w, width, name):
    t = proj.shape[0]
    tb = _row_tile(t, 128)
    hb = tb // HALO

    def body(prev_ref, x_ref, w_ref, o_ref, ext):
        keep = (pl.program_id(0) > 0).astype(F32)
        ext[0:HALO, :] = prev_ref[...].astype(F32) * keep
        ext[HALO:HALO + tb, :] = x_ref[...].astype(F32)
        for sl in [slice(c0, c0 + LANES) for c0 in range(0, width, LANES)]:
            acc = jnp.zeros((tb, LANES), F32)
            for j in range(CONV_WIDTH):
                o = HALO - (CONV_WIDTH - 1) + j
                acc = acc + w_ref[j:j + 1, sl] * ext[o:o + tb, sl]
            o_ref[:, sl] = jax.nn.silu(acc).astype(o_ref.dtype)

    return _pallas_call(
        body, name=name, grid=(t // tb,),
        in_specs=[pl.BlockSpec((HALO, width), lambda i: (jnp.maximum(i * hb - 1, 0), 0)),
                  pl.BlockSpec((tb, width), lambda i: (i, 0)),
                  pl.BlockSpec((CONV_WIDTH, width), lambda i: (0, 0))],
        out_specs=pl.BlockSpec((tb, width), lambda i: (i, 0)),
        out_shape=jax.ShapeDtypeStruct((t, width), F32),
        scratch_shapes=[pltpu.VMEM((tb + HALO, width), F32)],
        compiler_params=_params("parallel"),
    )(proj, proj, w)


def _conv_silu_vjp(proj, w, dy, width, name):
    t = proj.shape[0]
    tb = _row_tile(t, 128)
    hb = tb // HALO
    nb = t // tb
    te = tb + HALO

    def body(prev_ref, x_ref, next_ref, w_ref, dy_ref, dyn_ref, dx_ref, dw_ref, ext, dpre):
        i = pl.program_id(0)
        first = (i > 0).astype(F32)
        last = (i < nb - 1).astype(F32)
        ext[0:HALO, :] = prev_ref[...].astype(F32) * first
        ext[HALO:HALO + tb, :] = x_ref[...].astype(F32)
        ext[HALO + tb:, :] = next_ref[...].astype(F32) * last
        @pl.when(i == 0)
        def _():
            dw_ref[...] = jnp.zeros_like(dw_ref)

        for sl in [slice(c0, c0 + LANES) for c0 in range(0, width, LANES)]:
            pre = jnp.zeros((te, LANES), F32)
            for j in range(CONV_WIDTH):
                o = HALO - (CONV_WIDTH - 1) + j
                pre = pre + w_ref[j:j + 1, sl] * ext[o:o + te, sl]
            sig = jax.nn.sigmoid(pre)
            dsilu = sig * (1.0 + pre * (1.0 - sig))
            own = dy_ref[:, sl].astype(F32) * dsilu[0:tb, :]
            dpre[0:tb, sl] = own
            dpre[tb:, sl] = dyn_ref[:, sl].astype(F32) * last * dsilu[tb:, :]
            dx = jnp.zeros((tb, LANES), F32)
            for j in range(CONV_WIDTH):
                o = CONV_WIDTH - 1 - j
                dx = dx + w_ref[j:j + 1, sl] * dpre[o:o + tb, sl]
            dx_ref[:, sl] = dx.astype(dx_ref.dtype)
            for j in range(CONV_WIDTH):
                o = HALO - (CONV_WIDTH - 1) + j
                dw_ref[j:j + 1, sl] += jnp.sum(own * ext[o:o + tb, sl], axis=0, keepdims=True)

    halo_prev = lambda i: (jnp.maximum(i * hb - 1, 0), 0)
    halo_next = lambda i: (jnp.minimum((i + 1) * hb, t // HALO - 1), 0)
    return _pallas_call(
        body, name=name, grid=(nb,),
        in_specs=[pl.BlockSpec((HALO, width), halo_prev),
                  pl.BlockSpec((tb, width), lambda i: (i, 0)),
                  pl.BlockSpec((HALO, width), halo_next),
                  pl.BlockSpec((CONV_WIDTH, width), lambda i: (0, 0)),
                  pl.BlockSpec((tb, width), lambda i: (i, 0)),
                  pl.BlockSpec((HALO, width), halo_next)],
        out_specs=[pl.BlockSpec((tb, width), lambda i: (i, 0)),
                   pl.BlockSpec((CONV_WIDTH, width), lambda i: (0, 0))],
        out_shape=[jax.ShapeDtypeStruct((t, width), BF16), jax.ShapeDtypeStruct((CONV_WIDTH, width), F32)],
        scratch_shapes=[pltpu.VMEM((tb + 2 * HALO, width), F32), pltpu.VMEM((te, width), F32)],
        compiler_params=_params("arbitrary"),
    )(proj, proj, proj, w, dy, dy)


def _bdot(a, b, ca, cb):
    return lax.dot_general(a.astype(BF16), b.astype(BF16), (((ca,), (cb,)), ((0,), (0,))),
                           preferred_element_type=F32)


def _bdot3(a, b, ca, cb):
    a_hi, b_hi = a.astype(BF16), b.astype(BF16)
    a_lo, b_lo = (a - a_hi.astype(F32)).astype(BF16), (b - b_hi.astype(F32)).astype(BF16)
    dot = lambda p, q: lax.dot_general(p, q, (((ca,), (cb,)), ((0,), (0,))), preferred_element_type=F32)
    return dot(a_hi, b_hi) + (dot(a_hi, b_lo) + dot(a_lo, b_hi))


def _unit_lower_inverse(a):
    cs = a.shape[-1]
    r = lax.broadcasted_iota(jnp.int32, (1, cs, cs), 1)
    c = lax.broadcasted_iota(jnp.int32, (1, cs, cs), 2)
    eye = (r == c).astype(F32)
    diag = jnp.where((r // INV_BLOCK) == (c // INV_BLOCK), a, 0.0)

    def nilpotent_inverse(x, order):
        inv, p = eye - x, x
        for _ in range(int(math.log2(order)) - 1):
            p = _bdot3(p, p, 2, 1)
            inv = inv + _bdot3(inv, p, 2, 1)
        return inv

    t_diag = nilpotent_inverse(diag, INV_BLOCK)
    return _bdot3(nilpotent_inverse(_bdot3(t_diag, a - diag, 2, 1), cs // INV_BLOCK), t_diag, 2, 1)


@jax.custom_vjp
def _inverse_given(a, t):
    return t


_inverse_given.defvjp(lambda a, t: (t, t),
                      lambda t, g: (-_bdot3(_bdot3(t, g, 1, 1), t, 2, 2), jnp.zeros_like(t)))


def _bmm(a, b):
    return _bdot(a, b, 2, 1)


def _bmm_nt(a, b):
    return _bdot(a, b, 2, 2)


def _bmm_tn(a, b):
    return _bdot(a, b, 1, 1)


def _head_rms(o):
    return o * lax.rsqrt(jnp.mean(o * o, axis=-1, keepdims=True) + EPS)


def _l2norm(x):
    return x * lax.rsqrt(jnp.sum(x * x, axis=-1, keepdims=True) + 1e-6)


def _gdn_chunk(state, qc, kc, vc, z, beta_r, g_r, onw, t_saved=None):
    cs = LA_CHUNK
    r = lax.broadcasted_iota(jnp.int32, (1, cs, cs), 1)
    c = lax.broadcasted_iota(jnp.int32, (1, cs, cs), 2)
    eye = (r == c).astype(F32)
    causal, strict = r >= c, r > c
    q = _l2norm(qc.astype(F32)) * (HEAD_DIM ** -0.5)
    k = _l2norm(kc.astype(F32))
    v = vc.astype(F32)
    g_col = jnp.sum(eye * g_r, axis=-1, keepdims=True)
    beta_col = jnp.sum(eye * beta_r, axis=-1, keepdims=True)
    gc_col = jnp.sum(causal.astype(F32) * g_r, axis=-1, keepdims=True)
    gc_row = jnp.sum((r <= c).astype(F32) * g_col, axis=-2, keepdims=True)
    gc_last = jnp.sum(g_r, axis=-1, keepdims=True)
    decay = jnp.where(causal, jnp.exp(jnp.where(causal, gc_col - gc_row, 0.0)), 0.0)
    kb = k * beta_col
    a = jnp.where(strict, _bmm_nt(kb, k) * decay, 0.0)
    t_inv = _unit_lower_inverse(a) if t_saved is None else _inverse_given(a, t_saved)
    eg = jnp.exp(gc_col)
    u = _bmm(t_inv, v * beta_col)
    w = _bmm(t_inv, kb * eg)
    qk = jnp.where(causal, _bmm_nt(q, k) * decay, 0.0)
    v_new = u - _bmm(w, state)
    o = _bmm(q * eg, state) + _bmm(qk, v_new)
    new_state = state * jnp.exp(gc_last) + _bmm_tn(k * jnp.exp(gc_last - gc_col), v_new)
    o = _head_rms(o) * onw * jax.nn.silu(z.astype(F32))
    return (new_state, o, t_inv) if t_saved is None else (new_state, o)


@jax.custom_vjp
def _swap_halves(x):
    return pltpu.roll(x, HEAD_DIM // 2, x.ndim - 1)


_swap_halves.defvjp(lambda x: (_swap_halves(x), None), lambda _, g: (_swap_halves(g),))


def _make_ret_chunk(cos2, sin2, d_mat, dec_q, dec_k, dec_c):
    def chunk(state, rq, rk, rv, rg):
        qf, kf = rq.astype(F32), rk.astype(F32)
        q = qf * cos2 + _swap_halves(qf) * sin2
        k = (kf * cos2 + _swap_halves(kf) * sin2) * (HEAD_DIM ** -0.5)
        v = rv.astype(F32)
        inner = _bmm(_bmm_nt(q, k) * d_mat, v)
        cross = _bmm(q * dec_q, state)
        new_state = state * dec_c + _bmm_tn(k * dec_k, v)
        return new_state, jax.nn.silu(rg.astype(F32)) * _head_rms(inner + cross)
    return chunk


def _split_heads(x, n_heads):
    return jnp.stack([x[:, h * HEAD_DIM:(h + 1) * HEAD_DIM] for h in range(n_heads)], axis=0)


def _store_heads(ref, col0, val):
    for h in range(val.shape[0]):
        ref[:, col0 + h * HEAD_DIM:col0 + (h + 1) * HEAD_DIM] = val[h].astype(ref.dtype)


def _scan_fwd(make_chunk, seqs, rows, consts, n_heads, name, comm=(), saved_shape=None):
    cs, hk = LA_CHUNK, n_heads * HEAD_DIM
    t = seqs[0][0].shape[0]
    n = t // cs
    ns, nr, nc, nx = len(seqs), len(rows), len(consts), len(comm)
    n_out = 2 if saved_shape is None else 3

    def body(*refs):
        seq_refs, row_refs, const_refs = refs[:ns], refs[ns:ns + nr], refs[ns + nr:ns + nr + nc]
        o = ns + nr + nc
        src_refs = refs[o:o + nx]
        out_refs = refs[o + nx:o + nx + n_out]
        dst_refs = refs[o + nx + n_out:o + 2 * nx + n_out]
        state, sems = refs[o + 2 * nx + n_out], refs[o + 2 * nx + n_out + 1:]
        i = pl.program_id(0)

        def compute():
            @pl.when(i == 0)
            def _():
                state[...] = jnp.zeros_like(state)

            out_refs[1][0] = state[...]
            chunk = make_chunk([r[0] for r in row_refs], [r[...] for r in const_refs])
            new_state, out, *saved = chunk(state[...], *[_split_heads(r[...], n_heads) for r in seq_refs])
            state[...] = new_state
            _store_heads(out_refs[0], 0, out)
            if saved_shape is not None:
                out_refs[2][0] = saved[0]

        _carry(comm, src_refs, dst_refs, sems, i == 0, i == n // 2, i == n - 1, compute)

    chunk_spec = lambda shape: pl.BlockSpec((1,) + tuple(shape), lambda i, nd=len(shape): (i,) + (0,) * nd)
    saved = [] if saved_shape is None else [tuple(saved_shape)]
    return _pallas_call(
        body, name=name, grid=(n,),
        in_specs=[pl.BlockSpec((cs, hk), lambda i, b=b: (i, b)) for _, b in seqs]
        + [chunk_spec(a.shape[1:]) for a in rows] + [_full_spec(a) for a in consts] + [_ANY] * nx,
        out_specs=[pl.BlockSpec((cs, hk), lambda i: (i, 0)), chunk_spec((n_heads, HEAD_DIM, HEAD_DIM))]
        + [chunk_spec(s) for s in saved] + [_ANY] * nx,
        out_shape=[jax.ShapeDtypeStruct((t, hk), BF16),
                   jax.ShapeDtypeStruct((n, n_heads, HEAD_DIM, HEAD_DIM), F32)]
        + [jax.ShapeDtypeStruct((n,) + s, F32) for s in saved] + [ex.dst for ex in comm],
        scratch_shapes=[pltpu.VMEM((n_heads, HEAD_DIM, HEAD_DIM), F32)] + _EXCHANGE_SEMS * nx,
        compiler_params=_params("arbitrary"),
    )(*[a for a, _ in seqs], *rows, *consts, *[ex.src for ex in comm])


def _scan_bwd(make_chunk, seqs, rows, consts, states, d_out, n_heads, n_row_grads, n_const_grads, dseq_dtype,
              name, comm=()):
    cs, hk = LA_CHUNK, n_heads * HEAD_DIM
    t = seqs[0][0].shape[0]
    n = t // cs
    ns, nr, nc, nx = len(seqs), len(rows), len(consts), len(comm)
    n_grads = 1 + n_row_grads + n_const_grads

    def body(*refs):
        seq_refs, row_refs, const_refs = refs[:ns], refs[ns:ns + nr], refs[ns + nr:ns + nr + nc]
        st_ref, do_ref = refs[ns + nr + nc:ns + nr + nc + 2]
        o = ns + nr + nc + 2
        src_refs = refs[o:o + nx]
        o += nx
        dseq_ref = refs[o]
        drow_refs = refs[o + 1:o + 1 + n_row_grads]
        dconst_refs = refs[o + 1 + n_row_grads:o + n_grads]
        dst_refs = refs[o + n_grads:o + n_grads + nx]
        d_state, sems = refs[o + n_grads + nx], refs[o + n_grads + nx + 1:]
        i = pl.program_id(0)

        def compute():
            @pl.when(i == 0)
            def _():
                d_state[...] = jnp.zeros_like(d_state)
                for ref in dconst_refs:
                    ref[...] = jnp.zeros_like(ref)

            row_vals = [r[0] for r in row_refs]
            const_vals = [r[...] for r in const_refs]

            def fn(state, seq_vals, row_d, const_d):
                chunk = make_chunk(list(row_d) + row_vals[n_row_grads:], list(const_d) + const_vals[n_const_grads:])
                return chunk(state, *seq_vals)

            _, vjp = jax.vjp(fn, st_ref[0], tuple(_split_heads(r[...], n_heads) for r in seq_refs),
                             tuple(row_vals[:n_row_grads]), tuple(const_vals[:n_const_grads]))
            ds, dseq, drow, dconst = vjp((d_state[...], _split_heads(do_ref[...], n_heads).astype(F32)))
            d_state[...] = ds
            for j, g in enumerate(dseq):
                _store_heads(dseq_ref, j * hk, g)
            for ref, g in zip(drow_refs, drow):
                ref[0] = g
            for ref, g in zip(dconst_refs, dconst):
                ref[...] += g

        _carry(comm, src_refs, dst_refs, sems, i == 0, i == n // 2, i == n - 1, compute)

    rev = lambda i: n - 1 - i
    row_spec = lambda a: pl.BlockSpec((1,) + a.shape[1:], lambda i, nd=a.ndim: (rev(i),) + (0,) * (nd - 1))
    res = _pallas_call(
        body, name=name, grid=(n,),
        in_specs=[pl.BlockSpec((cs, hk), lambda i, b=b: (rev(i), b)) for _, b in seqs]
        + [row_spec(a) for a in rows] + [_full_spec(a) for a in consts]
        + [pl.BlockSpec((1, n_heads, HEAD_DIM, HEAD_DIM), lambda i: (rev(i), 0, 0, 0)),
           pl.BlockSpec((cs, hk), lambda i, b=d_out[1]: (rev(i), b))] + [_ANY] * nx,
        out_specs=[pl.BlockSpec((cs, ns * hk), lambda i: (rev(i), 0))]
        + [row_spec(a) for a in rows[:n_row_grads]] + [_full_spec(a) for a in consts[:n_const_grads]] + [_ANY] * nx,
        out_shape=[jax.ShapeDtypeStruct((t, ns * hk), dseq_dtype)]
        + [jax.ShapeDtypeStruct(a.shape, F32) for a in rows[:n_row_grads]]
        + [jax.ShapeDtypeStruct(a.shape, F32) for a in consts[:n_const_grads]] + [ex.dst for ex in comm],
        scratch_shapes=[pltpu.VMEM((n_heads, HEAD_DIM, HEAD_DIM), F32)] + _EXCHANGE_SEMS * nx,
        compiler_params=_params("arbitrary"),
    )(*[a for a, _ in seqs], *rows, *consts, states, d_out[0], *[ex.src for ex in comm])
    return (res[0], list(res[1:1 + n_row_grads]), list(res[1 + n_row_grads:n_grads]), list(res[n_grads:]))


def _gdn_make_chunk(row_vals, const_vals):
    beta_r, g_r, *saved = row_vals
    (onw,) = const_vals
    t_saved = saved[0] if saved else None
    return lambda state, qc, kc, vc, z: _gdn_chunk(state, qc, kc, vc, z, beta_r, g_r, onw, t_saved)


def _ret_make_chunk(row_vals, const_vals):
    cos2, sin2 = row_vals
    d_mat, dec_q, dec_k, dec_c = const_vals
    return _make_ret_chunk(cos2, sin2, d_mat, dec_q, dec_k, dec_c)


def _adamw_math(w, g, m, v):
    m = ADAM_B1 * m + (1.0 - ADAM_B1) * g
    v = ADAM_B2 * v + (1.0 - ADAM_B2) * jnp.square(g)
    m_hat = m / (1.0 - ADAM_B1 ** ADAM_STEP)
    v_hat = v / (1.0 - ADAM_B2 ** ADAM_STEP)
    delta = -ADAM_LR * (m_hat / (jnp.sqrt(v_hat) + ADAM_EPS) + ADAM_WD * w)
    return delta, m, v


def _sum_parts(parts, name):
    n_parts, rows, cols = parts.shape
    tb = _row_tile(rows, 512)

    def body(p_ref, o_ref):
        g = p_ref[0].astype(F32)
        for s in range(1, n_parts):
            g = g + p_ref[s].astype(F32)
        o_ref[...] = g

    return _pallas_call(
        body, name=name, grid=(rows // tb,),
        in_specs=[pl.BlockSpec((n_parts, tb, cols), lambda i: (0, i, 0))],
        out_specs=pl.BlockSpec((tb, cols), lambda i: (i, 0)),
        out_shape=jax.ShapeDtypeStruct((rows, cols), F32),
        compiler_params=_params("parallel"),
    )(parts)


def _adamw(parts, w, m, v, name, layer=None):
    rows, cols = w.shape[-2:]
    n_parts = parts.shape[0]
    tb = _row_tile(rows, max(8, (1 << 18) // cols // 8 * 8))

    def body(p_ref, w_ref, m_ref, v_ref, g_out, d_out, m_out, v_out):
        g = p_ref[0].astype(F32)
        for s in range(1, n_parts):
            g = g + p_ref[s].astype(F32)
        delta, m_new, v_new = _adamw_math(w_ref[...], g, m_ref[...], v_ref[...])
        g_out[...] = g
        d_out[...] = delta
        m_out[...] = m_new
        v_out[...] = v_new

    spec = pl.BlockSpec((tb, cols), lambda i: (i, 0))
    w_spec = spec if layer is None else pl.BlockSpec((None, tb, cols), lambda i: (layer, i, 0))
    return _pallas_call(
        body, name=name, grid=(rows // tb,),
        in_specs=[pl.BlockSpec((n_parts, tb, cols), lambda i: (0, i, 0)), w_spec, w_spec, w_spec],
        out_specs=[spec] * 4, out_shape=[jax.ShapeDtypeStruct((rows, cols), F32)] * 4,
        compiler_params=_params("parallel"),
    )(parts, w, m, v)


def _pack(arrays, rows_multiple=8):
    flat = jnp.concatenate([a.reshape(-1).astype(F32) for a in arrays])
    n = flat.shape[0]
    rows = -(-n // LANES)
    rows = -(-rows // rows_multiple) * rows_multiple
    return jnp.pad(flat, (0, rows * LANES - n)).reshape(rows, LANES)


def _unpack(packed, shapes):
    flat = packed.reshape(-1)
    out, o = [], 0
    for s in shapes:
        n = int(np.prod(s))
        out.append(flat[o:o + n].reshape(s))
        o += n
    return out


def _gather(shard):
    return _Exchange("gather", shard.astype(BF16))


def _rows_of(gathered):
    return gathered.reshape(gathered.shape[0] * gathered.shape[1], gathered.shape[2])


def _scatter_rows(full):
    k, n = full.shape
    return _Exchange("scatter", full.reshape(N_DEV, k // N_DEV, n))


def _relu2_epilogue(u):
    return u, jnp.square(jax.nn.relu(u))


def _relu2_vjp_epilogue(dr, u):
    return (dr * (2.0 * jax.nn.relu(u.astype(F32))),)


def kernel(x, norm_w, la_w_in, la_conv_w, la_a_log, la_dt_bias, la_out_norm_w, la_w_out, sg_w_in, sg_ln_w, sg_ln_b, sg_w_s, sg_b_s, sg_w_out, ffn_w_up, ffn_w_down, loss_target, m_norm_w, m_la_w_in, m_la_conv_w, m_la_a_log, m_la_dt_bias, m_la_out_norm_w, m_la_w_out, m_sg_w_in, m_sg_ln_w, m_sg_ln_b, m_sg_w_s, m_sg_b_s, m_sg_w_out, m_ffn_w_up, m_ffn_w_down, v_norm_w, v_la_w_in, v_la_conv_w, v_la_a_log, v_la_dt_bias, v_la_out_norm_w, v_la_w_out, v_sg_w_in, v_sg_ln_w, v_sg_ln_b, v_sg_w_s, v_sg_b_s, v_sg_w_out, v_ffn_w_up, v_ffn_w_down):
    weights = dict(norm_w=norm_w, la_w_in=la_w_in, la_conv_w=la_conv_w, la_a_log=la_a_log, la_dt_bias=la_dt_bias,
                   la_out_norm_w=la_out_norm_w, la_w_out=la_w_out, sg_w_in=sg_w_in, sg_ln_w=sg_ln_w, sg_ln_b=sg_ln_b,
                   sg_w_s=sg_w_s, sg_b_s=sg_b_s, sg_w_out=sg_w_out, ffn_w_up=ffn_w_up, ffn_w_down=ffn_w_down)
    mom1 = dict(norm_w=m_norm_w, la_w_in=m_la_w_in, la_conv_w=m_la_conv_w, la_a_log=m_la_a_log, la_dt_bias=m_la_dt_bias,
                la_out_norm_w=m_la_out_norm_w, la_w_out=m_la_w_out, sg_w_in=m_sg_w_in, sg_ln_w=m_sg_ln_w, sg_ln_b=m_sg_ln_b,
                sg_w_s=m_sg_w_s, sg_b_s=m_sg_b_s, sg_w_out=m_sg_w_out, ffn_w_up=m_ffn_w_up, ffn_w_down=m_ffn_w_down)
    mom2 = dict(norm_w=v_norm_w, la_w_in=v_la_w_in, la_conv_w=v_la_conv_w, la_a_log=v_la_a_log, la_dt_bias=v_la_dt_bias,
                la_out_norm_w=v_la_out_norm_w, la_w_out=v_la_w_out, sg_w_in=v_sg_w_in, sg_ln_w=v_sg_ln_w, sg_ln_b=v_sg_ln_b,
                sg_w_s=v_sg_w_s, sg_b_s=v_sg_b_s, sg_w_out=v_sg_w_out, ffn_w_up=v_ffn_w_up, ffn_w_down=v_ffn_w_down)
    names = list(weights)

    t, d = x.shape[1], x.shape[2]
    n_heads = la_a_log.shape[-1]
    hk = n_heads * HEAD_DIM
    cs = LA_CHUNK
    n_chunks = t // cs
    sg_width = sg_w_out.shape[1] * N_DEV
    me = 4 * lax.axis_index("x") + 2 * lax.axis_index("y") + lax.axis_index("c")
    xs = x.reshape(t, d)
    tgt = loss_target.reshape(t, d)

    conv_t = lambda a: jnp.transpose(a[0])
    small_local = lambda src: [src["norm_w"], conv_t(src["la_conv_w"]), src["sg_ln_w"], src["sg_ln_b"]]
    small_all = _all_gather(_pack(small_local(weights)), "ag_small")
    dsh = d // N_DEV
    csh = la_conv_w.shape[1]
    wsh = sg_width // N_DEV
    nw_parts, conv_parts, lnw_parts, lnb_parts = zip(*[
        _unpack(small_all[j], [(2, 4, dsh), (CONV_WIDTH, csh), (wsh,), (wsh,)]) for j in range(N_DEV)])
    nw = jnp.concatenate(nw_parts, axis=-1)
    conv_w = jnp.concatenate(conv_parts, axis=1)
    ln_w = jnp.concatenate(lnw_parts)[None, :]
    ln_b = jnp.concatenate(lnb_parts)[None, :]
    nrm = lambda l, j: nw[l, j][None, :]

    tb_d = _row_tile(t, 256)
    tb_wide = _row_tile(t, 64)

    y0, w_in_g = _rowwise(_seg_norm, [xs], [nrm(0, 0)], [(d, BF16)], tb_d, "f_norm0",
                          comm=[_gather(la_w_in[0])])
    in_cols = w_in_g.shape[2]
    gate_lo, gate_hi = 4 * hk, 4 * hk + 2 * n_heads
    main_pieces, gate_pieces = [], []
    for j in range(N_DEV):
        lo, hi = j * in_cols, (j + 1) * in_cols
        for s, e, dest in ((lo, min(hi, gate_lo), main_pieces), (max(lo, gate_lo), min(hi, gate_hi), gate_pieces),
                           (max(lo, gate_hi), hi, main_pieces)):
            if e > s:
                dest.append(w_in_g[j][:, s - lo:e - lo])
    w_main = jnp.concatenate(main_pieces, axis=1)
    w_gate = jnp.pad(jnp.concatenate(gate_pieces, axis=1), ((0, 0), (0, LANES - 2 * n_heads)))
    proj, w_up0_g = _matmul(y0, w_main, "nn", F32, "f_proj",
                            comm=[_gather(ffn_w_up[0])])
    gate_raw = _matmul(y0, w_gate, "nn", F32, "f_gate")
    a_log_v = jnp.pad(la_a_log.reshape(1, n_heads), ((0, 0), (n_heads, LANES - 2 * n_heads)))
    dt_bias_v = jnp.pad(la_dt_bias.reshape(1, n_heads), ((0, 0), (n_heads, LANES - 2 * n_heads)))
    seg_gates = _make_seg_gates(n_heads)
    (bg,) = _rowwise(seg_gates, [gate_raw], [a_log_v, dt_bias_v], [(LANES, F32)], tb_d, "f_gates")
    to_rows = lambda a: jnp.transpose(a.reshape(n_chunks, cs, n_heads), (0, 2, 1))[:, :, None, :]
    beta_r, g_r = to_rows(bg[:, :n_heads]), to_rows(bg[:, n_heads:2 * n_heads])
    qkvc = _conv_silu(proj, conv_w, 3 * hk, "f_conv")
    onw = la_out_norm_w.reshape(1, 1, HEAD_DIM)
    gdn_seqs = [(qkvc, 0), (qkvc, 1), (qkvc, 2), (proj, 3)]
    o_a, gdn_states, gdn_t_inv, w_down0_g = _scan_fwd(
        _gdn_make_chunk, gdn_seqs, [beta_r, g_r], [onw], n_heads, "f_gdn", comm=[_gather(ffn_w_down[0])],
        saved_shape=(n_heads, cs, cs))

    pos = jnp.arange(t, dtype=F32)
    inv_freq = 1.0 / (ROPE_BASE ** jnp.linspace(0.0, 1.0, HEAD_DIM // 2, dtype=F32))
    ang = pos[:, None] * inv_freq[None, :]
    cos2 = jnp.concatenate([jnp.cos(ang), jnp.cos(ang)], axis=-1).reshape(n_chunks, cs, HEAD_DIM)
    sin2 = jnp.concatenate([-jnp.sin(ang), jnp.sin(ang)], axis=-1).reshape(n_chunks, cs, HEAD_DIM)
    log_gamma = jnp.log1p(-jnp.power(2.0, -5.0 - jnp.arange(n_heads, dtype=F32)))
    cpos = jnp.arange(cs, dtype=F32)
    rel = cpos[:, None] - cpos[None, :]
    d_mat = jnp.where(rel >= 0, jnp.exp(jnp.where(rel >= 0, rel, 0.0) * log_gamma[:, None, None]), 0.0)
    dec_q = jnp.exp((cpos + 1.0) * log_gamma[:, None])[..., None]
    dec_k = jnp.exp((cs - 1.0 - cpos) * log_gamma[:, None])[..., None]
    dec_c = jnp.exp(cs * log_gamma)[:, None, None]
    ret_seqs = [(proj, 4), (proj, 5), (proj, 6), (proj, 7)]
    ret_consts = [d_mat, dec_q, dec_k, dec_c]
    o_b, ret_states, w_la_out_g = _scan_fwd(_ret_make_chunk, ret_seqs, [cos2, sin2], ret_consts, n_heads, "f_ret",
                                            comm=[_gather(la_w_out[0])])
    o_mix = jnp.concatenate([o_a, o_b], axis=1)
    y1 = _matmul(o_mix, _rows_of(w_la_out_g), "nn", F32, "f_la_out")
    h1, a1 = _rowwise(_seg_residual, [xs, y1], [nrm(0, 1), nrm(0, 2)], [(d, F32), (d, BF16)], tb_d, "f_res0a")
    u1, r1, w_sg_in_g = _matmul(a1, w_up0_g, "nn", None, "f_up0", b_blocked=True, epilogue=_relu2_epilogue,
                                out_dtypes=[F32, BF16], comm=[_gather(sg_w_in[0])])
    y2, w_sg_out_g = _matmul(r1, _rows_of(w_down0_g), "nn", F32, "f_down0", comm=[_gather(sg_w_out[0])])
    h2, a2 = _rowwise(_seg_residual, [h1, y2], [nrm(0, 3), nrm(1, 0)], [(d, F32), (d, BF16)], tb_d, "f_res0b")

    p_sg, w_up1_g = _matmul(a2, w_sg_in_g, "nn", F32, "f_sg_in", b_blocked=True,
                            comm=[_gather(ffn_w_up[1])])
    ua, vn = _gelu_ln(p_sg, ln_w, ln_b, tb_wide, "f_gelu_ln")
    ws = sg_w_s[0]
    bs = sg_b_s[0][:, :, None]
    gated = _spatial_gate(ua, vn, ws, bs, "f_sgate")
    y3 = _matmul(gated, _rows_of(w_sg_out_g), "nn", F32, "f_sg_out")
    h3, a3 = _rowwise(_seg_residual, [h2, y3], [nrm(1, 1), nrm(1, 2)], [(d, F32), (d, BF16)], tb_d, "f_res1a")
    u2, r2, w_down1_g = _matmul(a3, w_up1_g, "nn", None, "f_up1", b_blocked=True, epilogue=_relu2_epilogue,
                                out_dtypes=[F32, BF16], comm=[_gather(ffn_w_down[1])])
    y4 = _matmul(r2, _rows_of(w_down1_g), "nn", F32, "f_down1")

    ones = jnp.ones((t, 1), F32)
    (dh3, dy4), (dnw13,), loss_rows = _rowwise_vjp(
        _seg_loss, [h3, y4, tgt], [nrm(1, 3)], [ones], [F32, BF16, None], tb_d, "b_loss", primal_out=0)
    loss = lax.psum(jnp.sum(loss_rows), ("x", "y", "c"))

    du2 = _matmul(dy4, _rows_of(w_down1_g), "nt", BF16, "b_down1", extra=[u2], epilogue=_relu2_vjp_epilogue)
    dw_down1 = _matmul(r2, dy4, "tn", BF16, "b_dw_down1")
    da3, parts_down1 = _matmul(du2, w_up1_g, "nt", F32, "b_up1", b_blocked=True, comm=[_scatter_rows(dw_down1)])
    dw_up1 = _matmul(a3, du2, "tn", BF16, "b_dw_up1", out_blocked=True)
    (dh2, dy3), (dnw11, dnw12) = _rowwise_vjp(
        _seg_residual, [h2, y3], [nrm(1, 1), nrm(1, 2)], [dh3, da3], [F32, BF16], tb_d, "b_res1a")
    dgated = _matmul(dy3, _rows_of(w_sg_out_g), "nt", F32, "b_sg_out")
    dw_sg_out = _matmul(gated, dy3, "tn", BF16, "b_dw_sg_out")
    dua, dvn, dws, dbs = _spatial_gate_vjp(ua, vn, ws, bs, dgated, "b_sgate")
    dp_sg, dln_w, dln_b = _gelu_ln_vjp(p_sg, ln_w, dua, dvn, tb_wide, "b_gelu_ln")
    da2, parts_sg_out = _matmul(dp_sg, w_sg_in_g, "nt", F32, "b_sg_in", b_blocked=True,
                                comm=[_scatter_rows(dw_sg_out)])
    dw_sg_in, parts_up1 = _matmul(a2, dp_sg, "tn", BF16, "b_dw_sg_in", out_blocked=True,
                                  comm=[_Exchange("scatter", dw_up1)])
    (dh1, dy2), (dnw03, dnw10) = _rowwise_vjp(
        _seg_residual, [h1, y2], [nrm(0, 3), nrm(1, 0)], [dh2, da2], [F32, BF16], tb_d, "b_res0b")
    du1, parts_sg_in = _matmul(dy2, _rows_of(w_down0_g), "nt", BF16, "b_down0", extra=[u1],
                               epilogue=_relu2_vjp_epilogue, comm=[_Exchange("scatter", dw_sg_in)])
    dw_down0 = _matmul(r1, dy2, "tn", BF16, "b_dw_down0")
    da1, parts_down0 = _matmul(du1, w_up0_g, "nt", F32, "b_up0", b_blocked=True, comm=[_scatter_rows(dw_down0)])
    dw_up0 = _matmul(a1, du1, "tn", BF16, "b_dw_up0", out_blocked=True)
    (dx_res, dy1), (dnw01, dnw02) = _rowwise_vjp(
        _seg_residual, [xs, y1], [nrm(0, 1), nrm(0, 2)], [dh1, da1], [F32, BF16], tb_d, "b_res0a")
    do_mix = _matmul(dy1, _rows_of(w_la_out_g), "nt", F32, "b_la_out")
    dw_la_out = _matmul(o_mix, dy1, "tn", BF16, "b_dw_la_out")

    d_ret, _, _, _ = _scan_bwd(_ret_make_chunk, ret_seqs, [cos2, sin2], ret_consts, ret_states, (do_mix, 1),
                               n_heads, 0, 0, BF16, "b_ret")
    d_gdn, (dbeta_r, dg_r), (donw,), (parts_up0, parts_la_out) = _scan_bwd(
        _gdn_make_chunk, gdn_seqs, [beta_r, g_r, gdn_t_inv], [onw], gdn_states, (do_mix, 0), n_heads, 2, 1, F32, "b_gdn",
        comm=[_Exchange("scatter", dw_up0), _scatter_rows(dw_la_out)])
    from_rows = lambda a: jnp.transpose(a[:, :, 0, :], (0, 2, 1)).reshape(t, n_heads)
    dbg = jnp.pad(jnp.concatenate([from_rows(dbeta_r), from_rows(dg_r)], axis=1), ((0, 0), (0, LANES - 2 * n_heads)))
    (dgate_raw,), (da_log_v, ddt_bias_v) = _rowwise_vjp(
        seg_gates, [gate_raw], [a_log_v, dt_bias_v], [dbg], [BF16], tb_d, "b_gates")
    dqkv, dconv_w = _conv_silu_vjp(proj, conv_w, d_gdn, 3 * hk, "b_conv")
    dproj = jnp.concatenate([dqkv, d_gdn[:, 3 * hk:].astype(BF16), d_ret], axis=1)
    dw_main = _matmul(y0, dproj, "tn", BF16, "b_dw_proj")
    dw_gate = _matmul(y0, dgate_raw, "tn", BF16, "b_dw_gate")
    blocks = []
    for j in range(N_DEV):
        lo, hi = j * in_cols, (j + 1) * in_cols
        g_s = min(max(lo, gate_lo), gate_hi)
        g_e = max(g_s, min(hi, gate_hi))
        parts = [dw_main[:, lo:max(lo, min(hi, gate_lo))], dw_gate[:, g_s - gate_lo:g_e - gate_lo],
                 dw_main[:, max(lo, gate_hi) - 2 * n_heads:max(hi, gate_hi) - 2 * n_heads]]
        blocks.append(jnp.concatenate([p for p in parts if p.shape[1] > 0], axis=1))
    dw_in_blocks = jnp.stack(blocks)
    dy0, parts_in = _matmul(dproj, w_main, "nt", F32, "b_proj", comm=[_Exchange("scatter", dw_in_blocks)])
    dy0_gate = _matmul(dgate_raw, w_gate, "nt", F32, "b_gate")
    (grad_x,), (dnw00,) = _rowwise_vjp(
        _seg_norm, [xs], [nrm(0, 0)], [(dy0, dy0_gate)], [F32], tb_d, "b_norm0", adds=[dx_res])

    outs = {}
    for name, parts in (("la_w_in", parts_in), ("la_w_out", parts_la_out), ("sg_w_in", parts_sg_in),
                        ("sg_w_out", parts_sg_out)):
        res = _adamw(parts, weights[name], mom1[name], mom2[name], f"adamw_{name}", layer=0)
        outs[name] = [r[None] for r in res]
    for name, parts_l in (("ffn_w_up", [parts_up0, parts_up1]), ("ffn_w_down", [parts_down0, parts_down1])):
        res_l = [_adamw(parts_l[l], weights[name], mom1[name], mom2[name], f"adamw_{name}{l}", layer=l)
                 for l in range(2)]
        outs[name] = [jnp.stack([res_l[0][j], res_l[1][j]]) for j in range(4)]

    dnorm = jnp.stack([jnp.concatenate([dnw00, dnw01, dnw02, dnw03], axis=0),
                       jnp.concatenate([dnw10, dnw11, dnw12, dnw13], axis=0)])
    small_grads = [dnorm, dconv_w, dln_w[0], dln_b[0],
                   da_log_v[:, n_heads:2 * n_heads], ddt_bias_v[:, n_heads:2 * n_heads],
                   donw.reshape(1, HEAD_DIM), dws[None], dbs[None, :, :, 0]]
    small_parts = _all_gather(_pack(small_grads), "ag_small_grads")
    gs = _unpack(_sum_parts(small_parts, "sum_small_grads"), [g.shape for g in small_grads])
    sharded = lambda full, axis, size: lax.dynamic_slice_in_dim(full, me * size, size, axis)
    own = [sharded(gs[0], 2, dsh), sharded(gs[1], 1, csh), sharded(gs[2], 0, wsh)[None], sharded(gs[3], 0, wsh)[None]]
    own += gs[4:]
    replicated = ["la_a_log", "la_dt_bias", "la_out_norm_w", "sg_w_s", "sg_b_s"]
    small_names = ["norm_w", "la_conv_w", "sg_ln_w", "sg_ln_b"] + replicated
    pk = lambda src: _pack(small_local(src) + [src[n] for n in replicated])
    res = _adamw(_pack(own)[None], pk(weights), pk(mom1), pk(mom2), "adamw_small")
    for j, r in enumerate(res):
        vals = _unpack(r, [g.shape for g in own])
        vals[1] = jnp.transpose(vals[1])[None]
        for n, val in zip(small_names, vals):
            outs.setdefault(n, [None] * 4)[j] = val

    grad_x = grad_x[None]
    return (loss, grad_x, *[outs[n][0] for n in names], *[outs[n][1] for n in names],
            *[outs[n][2] for n in names], *[outs[n][3] for n in names])
```
